```python
import math
import jax
import jax.numpy as jnp
from jax import lax
import numpy as np

D_MODEL = 1024
BATCH = 32
SEQ = 256
DEPTH = 2
DEC_BATCH = 2
DEC_SEQ = 2048
PAST_LEN = 256

GRID_W = 64
HA = 4
DK_A = 64
DV_A = 64
WA = HA * DK_A
HB = 4
DQK_B = 32
DV_B = 64
WB = HB * 2 * DQK_B
HC = 4
DK_C = 64
DV_C = 64
WC = HC * DK_C
HD = 4
DK_D = 64
DV_D = 64
WD = HD * DK_D
N_BRANCH = 4
BRANCH_W = 256
FFN_DIM = 2816
SHORT_CONV = 3
CHUNK = 64
Q_BLOCK = 128
ROPE_BASE = 10000.0
EPS = 1e-6
N_MOD = 9
IN_SIZES = (WA, WA, WA, WA, 2 * HA, 2 * HA,
            WB, WB, HB * DV_B,
            WC, WC, WC, WC, 2 * HC, 2 * HC,
            WD, WD, WD, WD,
            N_BRANCH * D_MODEL)
N_IN = sum(IN_SIZES)

kernel_name = 'hybrid_diffusion_trunk_step'


def rms_norm(x, g):
    xf = x.astype(jnp.float32)
    y = xf * lax.rsqrt(jnp.mean(xf * xf, axis=-1, keepdims=True) + EPS)
    return (y * g.astype(jnp.float32)).astype(x.dtype)


def l2_norm(x):
    xf = x.astype(jnp.float32)
    return (xf * lax.rsqrt(jnp.sum(xf * xf, axis=-1, keepdims=True) + EPS)).astype(x.dtype)


def to_heads(x, n_heads):
    b, t, w = x.shape
    return x.reshape(b, t, n_heads, w // n_heads).transpose(0, 2, 1, 3)


def from_heads(x):
    b, h, t, d = x.shape
    return x.transpose(0, 2, 1, 3).reshape(b, t, h * d)


def bidir(x):
    return jnp.stack([x, jnp.flip(x, axis=-2)])


def bidir_gate(a):
    a = a.transpose(2, 0, 3, 1)
    return jnp.stack([a[0], jnp.flip(a[1], axis=-1)])


def merge_dirs(o):
    return o[0] + jnp.flip(o[1], axis=-2)


def short_conv(x, w):
    k = w.shape[0]
    pad = k // 2
    t = x.shape[1]
    xp = jnp.pad(x, ((0, 0), (pad, k - 1 - pad), (0, 0)))
    return sum(w[j] * xp[:, j:j + t] for j in range(k))


def swiglu(h, w_gate, w_up, w_down):
    return jnp.einsum('btf,fd->btd', jax.nn.silu(h @ w_gate) * (h @ w_up), w_down)


def rope_2d(x):
    t = x.shape[1]
    n_rows = t // GRID_W
    rows = jnp.repeat(jnp.arange(n_rows), GRID_W).astype(jnp.float32)
    cols = jnp.tile(jnp.arange(GRID_W), n_rows).astype(jnp.float32)
    n_freq = DQK_B // 4
    freqs = ROPE_BASE ** (-jnp.arange(n_freq, dtype=jnp.float32) / n_freq)

    def rot(xh, pos):
        ang = pos[:, None] * freqs
        cos = jnp.cos(ang)[None, :, None, None, :]
        sin = jnp.sin(ang)[None, :, None, None, :]
        x1, x2 = xh[..., :n_freq], xh[..., n_freq:]
        return jnp.concatenate([x1 * cos - x2 * sin, x2 * cos + x1 * sin], axis=-1)

    half = DQK_B // 2
    out = jnp.concatenate([rot(x[..., :half], rows), rot(x[..., half:], cols)], axis=-1)
    return out.astype(x.dtype)


def _tril(strict):
    return jnp.tril(jnp.ones((CHUNK, CHUNK), dtype=bool), -1 if strict else 0)


def _chunk(a):
    n = a.shape[-2] // CHUNK
    return jnp.moveaxis(a.reshape(a.shape[:-2] + (n, CHUNK, a.shape[-1])), -3, 0)


def _chunk_gate(a):
    n = a.shape[-1] // CHUNK
    return jnp.moveaxis(a.reshape(a.shape[:-1] + (n, CHUNK)), -2, 0)


def _unchunk(o):
    o = jnp.moveaxis(o, 0, -3)
    return o.reshape(o.shape[:-3] + (o.shape[-3] * CHUNK, o.shape[-1]))


def gated_delta_chunked(q, k, v, beta, g, s0):
    f32 = jnp.float32
    q, k, v = _chunk(q.astype(f32)), _chunk(k.astype(f32)), _chunk(v.astype(f32))
    beta, g = _chunk_gate(beta.astype(f32)), _chunk_gate(g.astype(f32))
    cg = jnp.cumsum(g, axis=-1)
    decay = jnp.exp(jnp.where(_tril(False), cg[..., :, None] - cg[..., None, :], -jnp.inf))
    kb = k * beta[..., None]
    vb = v * beta[..., None]
    a_low = jnp.where(_tril(True), jnp.einsum('...id,...jd->...ij', kb, k) * decay, 0.0)
    m_unit = a_low + jnp.eye(CHUNK, dtype=f32)
    u = lax.linalg.triangular_solve(m_unit, vb, left_side=True, lower=True, unit_diagonal=True)
    w = lax.linalg.triangular_solve(m_unit, kb * jnp.exp(cg)[..., None], left_side=True,
                                    lower=True, unit_diagonal=True)

    def step(s, inp):
        qc, kc, uc, wc, gc, dc = inp
        v_new = uc - jnp.einsum('...cd,...de->...ce', wc, s)
        attn = jnp.einsum('...id,...jd->...ij', qc, kc) * dc
        o = (jnp.einsum('...id,...de->...ie', qc * jnp.exp(gc)[..., None], s)
             + jnp.einsum('...ij,...je->...ie', attn, v_new))
        g_last = gc[..., -1]
        s = (s * jnp.exp(g_last)[..., None, None]
             + jnp.einsum('...cd,...ce->...de', kc * jnp.exp(g_last[..., None] - gc)[..., None], v_new))
        return s, o

    s_fin, o = lax.scan(step, s0.astype(f32), (q, k, u, w, cg, decay))
    return _unchunk(o), s_fin


def mlstm_chunked(q, k, v, ig, lf, c0, n0, m0):
    f32 = jnp.float32
    q, k, v = _chunk(q.astype(f32)), _chunk(k.astype(f32)), _chunk(v.astype(f32))
    ig, lf = _chunk_gate(ig.astype(f32)), _chunk_gate(lf.astype(f32))
    b = jnp.cumsum(lf, axis=-1)
    dmat = jnp.where(_tril(False), b[..., :, None] - b[..., None, :] + ig[..., None, :], -jnp.inf)

    def step(carry, inp):
        cp, np_, mp = carry
        qc, kc, vc, bc, igc, dc = inp
        inter = bc + mp[..., None]
        m_i = jnp.maximum(inter, jnp.max(dc, axis=-1))
        s = jnp.einsum('...id,...jd->...ij', qc, kc) * jnp.exp(dc - m_i[..., None])
        si = jnp.exp(inter - m_i)
        num = (si[..., None] * jnp.einsum('...id,...de->...ie', qc, cp)
               + jnp.einsum('...ij,...je->...ie', s, vc))
        den = si * jnp.einsum('...id,...d->...i', qc, np_) + jnp.sum(s, axis=-1)
        h = num / jnp.maximum(jnp.abs(den), jnp.exp(-m_i))[..., None]
        bl = bc[..., -1]
        logw = bl[..., None] - bc + igc
        m_new = jnp.maximum(bl + mp, jnp.max(logw, axis=-1))
        wk = kc * jnp.exp(logw - m_new[..., None])[..., None]
        dec = jnp.exp(bl + mp - m_new)
        c_new = dec[..., None, None] * cp + jnp.einsum('...cd,...ce->...de', wk, vc)
        n_new = dec[..., None] * np_ + jnp.sum(wk, axis=-2)
        return (c_new, n_new, m_new), h

    init = (c0.astype(f32), n0.astype(f32), m0.astype(f32))
    fin, h = lax.scan(step, init, (q, k, v, b, ig, dmat))
    return _unchunk(h), fin


def retention_chunked(q, k, v, log_gamma, r0):
    f32 = jnp.float32
    q, k, v = _chunk(q.astype(f32)), _chunk(k.astype(f32)), _chunk(v.astype(f32))
    pos = jnp.arange(CHUNK, dtype=f32)
    lg = log_gamma[..., None]
    rel = pos[:, None] - pos[None, :]
    decay = jnp.exp(jnp.where(_tril(False), rel * lg[..., None], -jnp.inf))
    q_dec = jnp.exp((pos + 1.0) * lg)
    k_dec = jnp.exp((CHUNK - 1.0 - pos) * lg)
    c_dec = jnp.exp(CHUNK * log_gamma)[..., None, None]

    def step(r, inp):
        qc, kc, vc = inp
        attn = jnp.einsum('...id,...jd->...ij', qc, kc) * decay
        o = (jnp.einsum('...ij,...je->...ie', attn, vc)
             + jnp.einsum('...id,...de->...ie', qc * q_dec[..., None], r))
        r = c_dec * r + jnp.einsum('...cd,...ce->...de', kc * k_dec[..., None], vc)
        return r, o

    r_fin, o = lax.scan(step, r0.astype(f32), (q, k, v))
    return _unchunk(o), r_fin


def diff_attention_blocked(q, keys, vals, lam):
    b, h, t, dq = q.shape
    nb = t // Q_BLOCK
    qb = jnp.moveaxis(q.reshape(b, h, nb, Q_BLOCK, dq), 2, 0)
    k1, k2 = keys[..., :DQK_B], keys[..., DQK_B:]
    scale = DQK_B ** -0.5

    def one(qblk):
        q1, q2 = qblk[..., :DQK_B], qblk[..., DQK_B:]
        s1 = jnp.einsum('bhqd,bhkd->bhqk', q1, k1).astype(jnp.float32) * scale
        s2 = jnp.einsum('bhqd,bhkd->bhqk', q2, k2).astype(jnp.float32) * scale
        p = jax.nn.softmax(s1, axis=-1) - lam * jax.nn.softmax(s2, axis=-1)
        return jnp.einsum('bhqk,bhkd->bhqd', p.astype(vals.dtype), vals)

    o = lax.map(one, qb)
    return jnp.moveaxis(o, 0, 2).reshape(b, h, t, vals.shape[-1])


def token_mix(h, lp, l, cache):
    f32 = jnp.float32
    b, t, _ = h.shape
    proj = jnp.einsum('btd,dn->btn', h, lp['w_in'])
    (a_q, a_k, a_v, a_z, a_beta, a_alpha,
     b_q, b_k, b_v,
     c_q, c_k, c_v, c_o, c_i, c_f,
     d_q, d_k, d_v, d_g, merge_logits) = jnp.split(proj, np.cumsum(IN_SIZES)[:-1].tolist(), axis=-1)
    if cache is None:
        ctx_k = None
        ctx_v = None
        s_dn0 = jnp.zeros((2, b, HA, DK_A, DV_A), f32)
        c0 = jnp.zeros((2, b, HC, DK_C, DV_C), f32)
        n0 = jnp.zeros((2, b, HC, DK_C), f32)
        m0 = jnp.zeros((2, b, HC), f32)
        r0 = jnp.zeros((2, b, HD, DK_D, DV_D), f32)
    else:
        ctx_k, ctx_v, st_dn, st_c, st_n, st_m, st_r = cache
        s_dn0 = jnp.moveaxis(st_dn, 1, 0)
        c0 = jnp.moveaxis(st_c, 1, 0)
        n0 = jnp.moveaxis(st_n, 1, 0)
        m0 = jnp.moveaxis(st_m, 1, 0)
        r0 = jnp.moveaxis(st_r, 1, 0)

    qkv = jax.nn.silu(short_conv(jnp.concatenate([a_q, a_k, a_v], axis=-1), lp['dn_conv_w']))
    qa, ka, va = jnp.split(qkv, 3, axis=-1)
    qa = l2_norm(to_heads(qa, HA)) * DK_A ** -0.5
    ka = l2_norm(to_heads(ka, HA))
    va = to_heads(va, HA)
    beta = bidir_gate(jax.nn.sigmoid(a_beta.astype(f32)).reshape(b, t, 2, HA))
    g = bidir_gate(-jnp.exp(lp['dn_a_log'].astype(f32))
                   * jax.nn.softplus(a_alpha.astype(f32).reshape(b, t, 2, HA) + lp['dn_dt_bias'].astype(f32)))
    o_a, s_dn = gated_delta_chunked(bidir(qa), bidir(ka), bidir(va), beta, g, s_dn0)
    out_a = from_heads(rms_norm(merge_dirs(o_a).astype(h.dtype), lp['dn_norm_g'])
                       * jax.nn.silu(to_heads(a_z, HA)))

    qd = rms_norm(b_q.reshape(b, t, HB, 2, DQK_B), lp['da_qn_g'])
    kd = rms_norm(b_k.reshape(b, t, HB, 2, DQK_B), lp['da_kn_g'])
    if cache is not None:
        qd = rope_2d(qd)
        kd = rope_2d(kd)
    qd = qd.reshape(b, t, HB, 2 * DQK_B).transpose(0, 2, 1, 3)
    kd = kd.reshape(b, t, HB, 2 * DQK_B).transpose(0, 2, 1, 3)
    vd = to_heads(b_v, HB)
    if ctx_k is None:
        keys, vals = kd, vd
    else:
        keys = jnp.concatenate([ctx_k.astype(kd.dtype), kd], axis=2)
        vals = jnp.concatenate([ctx_v.astype(vd.dtype), vd], axis=2)
    lam_init = 0.8 - 0.6 * math.exp(-0.3 * l)
    lam_p = lp['da_lambda'].astype(f32)
    lam = jnp.exp(jnp.sum(lam_p[0] * lam_p[1])) - jnp.exp(jnp.sum(lam_p[2] * lam_p[3])) + lam_init
    o_b = diff_attention_blocked(qd, keys, vals, lam)
    out_b = from_heads(rms_norm(o_b, lp['da_norm_g']) * (1.0 - lam_init))

    qc = to_heads(c_q, HC) * DK_C ** -0.5
    kc = to_heads(c_k, HC)
    vc = to_heads(c_v, HC)
    ig = bidir_gate(c_i.astype(f32).reshape(b, t, 2, HC) + lp['ml_i_bias'].astype(f32))
    lf = bidir_gate(jax.nn.log_sigmoid(c_f.astype(f32).reshape(b, t, 2, HC) + lp['ml_f_bias'].astype(f32)))
    o_c, (s_c, s_n, s_m) = mlstm_chunked(bidir(qc), bidir(kc), bidir(vc), ig, lf, c0, n0, m0)
    out_c = from_heads(jax.nn.sigmoid(to_heads(c_o, HC))
                       * rms_norm(merge_dirs(o_c).astype(h.dtype), lp['ml_norm_g']))

    qr = to_heads(d_q, HD)
    kr = to_heads(d_k, HD) * DK_D ** -0.5
    vr = to_heads(d_v, HD)
    log_gamma = jax.nn.log_sigmoid(lp['ret_decay_logit'].astype(f32))[:, None, :]
    o_d, s_r = retention_chunked(bidir(qr), bidir(kr), bidir(vr), log_gamma, r0)
    out_d = from_heads(jax.nn.silu(to_heads(d_g, HD))
                       * rms_norm(merge_dirs(o_d).astype(h.dtype), lp['ret_norm_g']))

    branches = jnp.stack([out_a, out_b, out_c, out_d])
    proj_b = jnp.einsum('mbtc,mcd->btmd', branches, lp['w_branch'])
    gates = jax.nn.sigmoid(merge_logits.reshape(b, t, N_BRANCH, D_MODEL))
    y = jnp.einsum('btd,de->bte', jnp.sum(gates * proj_b, axis=2), lp['w_out'])
    dt = h.dtype
    state = (kd, vd,
             jnp.moveaxis(s_dn, 0, 1).astype(dt), jnp.moveaxis(s_c, 0, 1).astype(dt),
             jnp.moveaxis(s_n, 0, 1).astype(dt), jnp.moveaxis(s_m, 0, 1).astype(dt),
             jnp.moveaxis(s_r, 0, 1).astype(dt))
    return y, state


def trunk_layer(x, cond, lp, l, cache):
    mod = jnp.einsum('bd,de->be', jax.nn.silu(cond), lp['w_ada']) + lp['b_ada']
    mod = mod.reshape(cond.shape[0], N_MOD, 1, D_MODEL)
    h = rms_norm(x, lp['norm_g'][0]) * (1.0 + mod[:, 1]) + mod[:, 0]
    x = x + 0.5 * mod[:, 2] * swiglu(h, lp['ffn_w_gate'][0], lp['ffn_w_up'][0], lp['ffn_w_down'][0])
    h = rms_norm(x, lp['norm_g'][1]) * (1.0 + mod[:, 4]) + mod[:, 3]
    mix, state = token_mix(h, lp, l, cache)
    x = x + mod[:, 5] * mix
    h = rms_norm(x, lp['norm_g'][2]) * (1.0 + mod[:, 7]) + mod[:, 6]
    x = x + 0.5 * mod[:, 8] * swiglu(h, lp['ffn_w_gate'][1], lp['ffn_w_up'][1], lp['ffn_w_down'][1])
    return x, state


def setup_inputs(seed: int = 0) -> dict:
    key = jax.random.key(seed)
    ks = iter(jax.random.split(key, 40))
    f32 = jnp.float32
    D = D_MODEL

    def nrm(shape, s):
        return jax.random.normal(next(ks), shape, f32) * s

    dt0 = jax.random.uniform(next(ks), (DEPTH, 2, HA), f32, 0.001, 0.1)
    return {
        'x_prompt': nrm((BATCH, SEQ, D), 1.0),
        'x_sample': nrm((DEC_BATCH, DEC_SEQ, D), 1.0),
        'cache_diff_k': nrm((DEC_BATCH, DEPTH, HB, PAST_LEN, 2 * DQK_B), 1.0),
        'cache_diff_v': nrm((DEC_BATCH, DEPTH, HB, PAST_LEN, DV_B), 1.0),
        'state_delta': nrm((DEC_BATCH, DEPTH, 2, HA, DK_A, DV_A), 0.1),
        'state_mlstm_C': nrm((DEC_BATCH, DEPTH, 2, HC, DK_C, DV_C), 0.5),
        'state_mlstm_n': nrm((DEC_BATCH, DEPTH, 2, HC, DK_C), 0.5),
        'state_mlstm_m': nrm((DEC_BATCH, DEPTH, 2, HC), 1.0),
        'state_ret': nrm((DEC_BATCH, DEPTH, 2, HD, DK_D, DV_D), 0.1),
        'c': nrm((DEC_BATCH, D), 1.0),
        'c_ctx': nrm((D,), 1.0),
        'w_ada': nrm((DEPTH, D, N_MOD * D), 0.5 * D ** -0.5),
        'b_ada': nrm((DEPTH, N_MOD * D), 0.02),
        'norm_g': 1.0 + nrm((DEPTH, 3, D), 0.02),
        'ffn_w_gate': nrm((DEPTH, 2, D, FFN_DIM), D ** -0.5),
        'ffn_w_up': nrm((DEPTH, 2, D, FFN_DIM), D ** -0.5),
        'ffn_w_down': nrm((DEPTH, 2, FFN_DIM, D), FFN_DIM ** -0.5),
        'w_in': nrm((DEPTH, D, N_IN), D ** -0.5),
        'dn_conv_w': nrm((DEPTH, SHORT_CONV, 3 * WA), 0.5),
        'dn_a_log': jnp.log(jax.random.uniform(next(ks), (DEPTH, 2, HA), f32, 1.0, 16.0)),
        'dn_dt_bias': dt0 + jnp.log(-jnp.expm1(-dt0)),
        'dn_norm_g': 1.0 + nrm((DEPTH, DV_A), 0.02),
        'da_qn_g': 1.0 + nrm((DEPTH, DQK_B), 0.02),
        'da_kn_g': 1.0 + nrm((DEPTH, DQK_B), 0.02),
        'da_lambda': nrm((DEPTH, 4, DQK_B), 0.1),
        'da_norm_g': 1.0 + nrm((DEPTH, DV_B), 0.02),
        'ml_i_bias': nrm((DEPTH, 2, HC), 0.1),
        'ml_f_bias': jnp.linspace(3.0, 6.0, HC, dtype=f32) + nrm((DEPTH, 2, HC), 0.1),
        'ml_norm_g': 1.0 + nrm((DEPTH, DV_C), 0.02),
        'ret_decay_logit': jnp.log(2.0 ** (5.0 + jnp.arange(HD, dtype=f32)) - 1.0) + nrm((DEPTH, 2, HD), 0.1),
        'ret_norm_g': 1.0 + nrm((DEPTH, DV_D), 0.02),
        'w_branch': nrm((DEPTH, N_BRANCH, BRANCH_W, D), BRANCH_W ** -0.5),
        'w_out': nrm((DEPTH, D, D), D ** -0.5),
    }


def reference(x_prompt, x_sample, cache_diff_k, cache_diff_v, state_delta, state_mlstm_C,
              state_mlstm_n, state_mlstm_m, state_ret, c, c_ctx, w_ada, b_ada, norm_g,
              ffn_w_gate, ffn_w_up, ffn_w_down, w_in, dn_conv_w, dn_a_log, dn_dt_bias, dn_norm_g,
              da_qn_g, da_kn_g, da_lambda, da_norm_g, ml_i_bias, ml_f_bias, ml_norm_g,
              ret_decay_logit, ret_norm_g, w_branch, w_out):
    yp = x_prompt
    ys = x_sample
    st_k, st_v, st_dn, st_c, st_n, st_m, st_r = [], [], [], [], [], [], []
    for l in range(DEPTH):
        lp = {
            'w_ada': w_ada[l], 'b_ada': b_ada[l], 'norm_g': norm_g[l],
            'ffn_w_gate': ffn_w_gate[l], 'ffn_w_up': ffn_w_up[l], 'ffn_w_down': ffn_w_down[l],
            'w_in': w_in[l], 'dn_conv_w': dn_conv_w[l], 'dn_a_log': dn_a_log[l],
            'dn_dt_bias': dn_dt_bias[l], 'dn_norm_g': dn_norm_g[l], 'da_qn_g': da_qn_g[l],
            'da_kn_g': da_kn_g[l], 'da_lambda': da_lambda[l], 'da_norm_g': da_norm_g[l],
            'ml_i_bias': ml_i_bias[l], 'ml_f_bias': ml_f_bias[l], 'ml_norm_g': ml_norm_g[l],
            'ret_decay_logit': ret_decay_logit[l], 'ret_norm_g': ret_norm_g[l],
            'w_branch': w_branch[l], 'w_out': w_out[l],
        }
        yp, st = trunk_layer(yp, c_ctx[None, :], lp, l, None)
        st_k.append(st[0])
        st_v.append(st[1])
        st_dn.append(st[2])
        st_c.append(st[3])
        st_n.append(st[4])
        st_m.append(st[5])
        st_r.append(st[6])
        cache_l = (cache_diff_k[:, l], cache_diff_v[:, l], state_delta[:, l], state_mlstm_C[:, l],
                   state_mlstm_n[:, l], state_mlstm_m[:, l], state_ret[:, l])
        ys, _ = trunk_layer(ys, c, lp, l, cache_l)
    new_diff_k = jnp.stack(st_k, axis=1)
    new_diff_v = jnp.stack(st_v, axis=1)
    new_delta = jnp.stack(st_dn, axis=1)
    new_mlstm_C = jnp.stack(st_c, axis=1)
    new_mlstm_n = jnp.stack(st_n, axis=1)
    new_mlstm_m = jnp.stack(st_m, axis=1)
    new_ret = jnp.stack(st_r, axis=1)
    return (yp, ys, new_diff_k, new_diff_v, new_delta, new_mlstm_C, new_mlstm_n, new_mlstm_m, new_ret)
```

```python
import functools
import math

import numpy as np
import jax
import jax.numpy as jnp
from jax import lax
from jax.experimental import pallas as pl
from jax.experimental.pallas import tpu as pltpu

F32 = jnp.float32
BF16 = jnp.bfloat16

D_MODEL = 1024
FFN_DIM = 2816
N_MOD = 9
DEPTH = 2
N_HEADS = 4
HEAD_DIM = 64
SEG = N_HEADS * HEAD_DIM
N_SEG = 15
CHUNK = 64
DQK = 32
GRID_W = 64
ROPE_BASE = 10000.0
EPS = 1e-6
N_BRANCH = 4
GATE_LANES = 128
NEG = -1e30

TM = 256
Q_BLOCK = 256
ROW_BLOCK = 128
VMEM_LIMIT = 56 * 1024 * 1024

NN = (((1,), (0,)), ((), ()))
NT = (((1,), (1,)), ((), ()))
TN = (((0,), (0,)), ((), ()))

_IN_SIZES = (256, 256, 256, 256, 8, 8, 256, 256, 256, 256, 256, 256, 256, 8, 8, 256, 256, 256, 256, 4096)
_IN_OFFS = np.concatenate([[0], np.cumsum(_IN_SIZES)]).tolist()
_SEG_FIELDS = (0, 1, 2, 3, 6, 7, 8, 9, 10, 11, 12, 15, 16, 17, 18)
_GATE_FIELDS = (4, 5, 13, 14)
(S_AQ, S_AK, S_AV, S_AZ, S_BQ, S_BK, S_BV, S_CQ, S_CK, S_CV, S_CO, S_DQ, S_DK, S_DV, S_DG) = range(N_SEG)
L_BETA, L_ALPHA, L_CI, L_CF = 0, 8, 16, 24


def _dg(a, b, dims):
    return lax.dot_general(a, b, dims, preferred_element_type=F32)


def _split2(x):
    hi = x.astype(BF16)
    lo = (x - hi.astype(F32)).astype(BF16)
    return hi, lo


def _mm1(a, b, dims=NN):
    return _dg(a.astype(BF16), b.astype(BF16), dims)


def _mm3(a, b, dims=NN):
    ah, al = _split2(a)
    bh, bl = _split2(b)
    return _dg(ah, bh, dims) + (_dg(ah, bl, dims) + _dg(al, bh, dims))


def _sel_mm(sel, x, dims=NN):
    h0 = x.astype(BF16)
    r1 = x - h0.astype(F32)
    h1 = r1.astype(BF16)
    h2 = (r1 - h1.astype(F32)).astype(BF16)
    return _dg(sel, h0, dims) + (_dg(sel, h1, dims) + _dg(sel, h2, dims))


def _gsum(x, bd):
    hi, lo = _split2(x)
    return _dg(hi, bd, NN) + _dg(lo, bd, NN)


def _sigmoid(x):
    return 1.0 / (1.0 + jnp.exp(-x))


def _silu(x):
    return x * _sigmoid(x)


def _softplus(x):
    return jnp.maximum(x, 0.0) + jnp.log1p(jnp.exp(-jnp.abs(x)))


def _log_sigmoid(x):
    return -_softplus(-x)


def _norm_mod(x, g, shift, scale):
    ms = jnp.mean(x * x, axis=-1, keepdims=True)
    return (x * lax.rsqrt(ms + EPS) * g) * (1.0 + scale) + shift


def _tri_masks():
    ii = lax.broadcasted_iota(jnp.int32, (CHUNK, CHUNK), 0)
    jj = lax.broadcasted_iota(jnp.int32, (CHUNK, CHUNK), 1)
    return ((jj <= ii, jj < ii), (jj >= ii, jj > ii))


def _eye(n):
    ii = lax.broadcasted_iota(jnp.int32, (n, n), 0)
    jj = lax.broadcasted_iota(jnp.int32, (n, n), 1)
    return (ii == jj).astype(BF16)


def _cparams(n_grid=1):
    return pltpu.CompilerParams(dimension_semantics=("arbitrary",) * n_grid,
                                vmem_limit_bytes=VMEM_LIMIT)


def _const_spec(shape):
    nd = len(shape)
    return pl.BlockSpec(shape, lambda *_: (0,) * nd)


def _ada_kernel(s_ref, w_ref, b_ref, o_ref):
    s = s_ref[...]
    o_ref[...] = _mm3(_silu(s), w_ref[...]) + b_ref[...]


def _ada(cond8, w_ada, b_ada):
    tn = 1536
    n_t = (N_MOD * D_MODEL) // tn
    return pl.pallas_call(
        _ada_kernel,
        grid=(DEPTH, n_t),
        in_specs=[pl.BlockSpec((8, D_MODEL), lambda l, j: (0, 0)),
                  pl.BlockSpec((None, D_MODEL, tn), lambda l, j: (l, 0, j)),
                  pl.BlockSpec((None, 1, tn), lambda l, j: (l, 0, j))],
        out_specs=pl.BlockSpec((None, 8, tn), lambda l, j: (l, 0, j)),
        out_shape=jax.ShapeDtypeStruct((DEPTH, 8, N_MOD * D_MODEL), F32),
        compiler_params=_cparams(2),
        name="ada_mod",
    )(cond8, w_ada, b_ada.reshape(DEPTH, 1, N_MOD * D_MODEL))


def _ffn_kernel(x_ref, mod_ref, g_ref, wg_ref, wu_ref, wd_ref, o_ref, *, mi):
    x = x_ref[...]
    h = _norm_mod(x, g_ref[...], mod_ref[mi:mi + 1, :], mod_ref[mi + 1:mi + 2, :]).astype(BF16)
    gate = jnp.dot(h, wg_ref[...], preferred_element_type=F32)
    up = jnp.dot(h, wu_ref[...], preferred_element_type=F32)
    act = (_silu(gate) * up).astype(BF16)
    y = jnp.dot(act, wd_ref[...], preferred_element_type=F32)
    o_ref[...] = x + (0.5 * mod_ref[mi + 2:mi + 3, :]) * y


def _mod_spec(rows_per_cond, first_cond):
    per = rows_per_cond // TM
    return pl.BlockSpec((None, N_MOD, D_MODEL), lambda i: (first_cond + i // per, 0, 0))


def _ffn(x, mod, mod_spec, g, wg, wu, wd, mi):
    n = x.shape[0]
    return pl.pallas_call(
        functools.partial(_ffn_kernel, mi=mi),
        grid=(n // TM,),
        in_specs=[pl.BlockSpec((TM, D_MODEL), lambda i: (i, 0)),
                  mod_spec,
                  _const_spec((1, D_MODEL)),
                  _const_spec((D_MODEL, FFN_DIM)),
                  _const_spec((D_MODEL, FFN_DIM)),
                  _const_spec((FFN_DIM, D_MODEL))],
        out_specs=pl.BlockSpec((TM, D_MODEL), lambda i: (i, 0)),
        out_shape=jax.ShapeDtypeStruct((n, D_MODEL), F32),
        compiler_params=_cparams(1),
        name="ffn",
    )(x, mod, g, wg, wu, wd)


def _inproj_kernel(x_ref, mod_ref, g_ref, w_ref, wgt_ref, p_ref, gt_ref):
    h = _norm_mod(x_ref[...], g_ref[...], mod_ref[3:4, :], mod_ref[4:5, :]).astype(BF16)
    step = 3 * SEG
    for j in range(0, N_SEG * SEG, step):
        p_ref[:, j:j + step] = jnp.dot(h, w_ref[:, j:j + step], preferred_element_type=F32)
    gt_ref[...] = jnp.dot(h, wgt_ref[...], preferred_element_type=F32)


def _inproj(x, mod, mod_spec, g, w_mix, w_gate):
    n = x.shape[0]
    return pl.pallas_call(
        _inproj_kernel,
        grid=(n // TM,),
        in_specs=[pl.BlockSpec((TM, D_MODEL), lambda i: (i, 0)),
                  mod_spec,
                  _const_spec((1, D_MODEL)),
                  _const_spec((D_MODEL, N_SEG * SEG)),
                  _const_spec((D_MODEL, GATE_LANES))],
        out_specs=[pl.BlockSpec((TM, N_SEG * SEG), lambda i: (i, 0)),
                   pl.BlockSpec((TM, GATE_LANES), lambda i: (i, 0))],
        out_shape=[jax.ShapeDtypeStruct((n, N_SEG * SEG), F32),
                   jax.ShapeDtypeStruct((n, GATE_LANES), F32)],
        compiler_params=_cparams(1),
        name="in_proj",
    )(x, mod, g, w_mix, w_gate)


def _merge_kernel(x_ref, mod_ref, g_ref, ba_ref, bb_ref, bc_ref, bd_ref, wm_ref, wb_ref, wo_ref, o_ref):
    x = x_ref[...]
    h = _norm_mod(x, g_ref[...], mod_ref[3:4, :], mod_ref[4:5, :]).astype(BF16)
    mixed = None
    for m, b_ref in enumerate((ba_ref, bb_ref, bc_ref, bd_ref)):
        logits = jnp.dot(h, wm_ref[:, m * D_MODEL:(m + 1) * D_MODEL], preferred_element_type=F32)
        pb = jnp.dot(b_ref[...].astype(BF16), wb_ref[m], preferred_element_type=F32)
        term = _sigmoid(logits) * pb
        mixed = term if mixed is None else mixed + term
    y = jnp.dot(mixed.astype(BF16), wo_ref[...], preferred_element_type=F32)
    o_ref[...] = x + mod_ref[5:6, :] * y


def _merge(x, mod, mod_spec, g, branches, w_merge, w_branch, w_out):
    n = x.shape[0]
    row = lambda w: pl.BlockSpec((TM, w), lambda i: (i, 0))
    return pl.pallas_call(
        _merge_kernel,
        grid=(n // TM,),
        in_specs=[row(D_MODEL), mod_spec, _const_spec((1, D_MODEL)),
                  row(SEG), row(SEG), row(SEG), row(SEG),
                  _const_spec((D_MODEL, N_BRANCH * D_MODEL)),
                  _const_spec((N_BRANCH, SEG, D_MODEL)),
                  _const_spec((D_MODEL, D_MODEL))],
        out_specs=row(D_MODEL),
        out_shape=jax.ShapeDtypeStruct((n, D_MODEL), F32),
        compiler_params=_cparams(1),
        name="merge",
    )(x, mod, g, *branches, w_merge, w_branch, w_out)


def _seq_specs(T, segs):
    return [pl.BlockSpec((T, SEG), lambda s, j=j: (s, j)) for j in segs]


def _gate_spec(T):
    return pl.BlockSpec((T, GATE_LANES), lambda s: (s, 0))


def _state_in_spec(l, tail):
    nd = len(tail)
    return pl.BlockSpec((None, None) + tail, lambda s: (s, l) + (0,) * nd)


def _lead_spec(tail):
    nd = len(tail)
    return pl.BlockSpec((None,) + tail, lambda s: (s,) + (0,) * nd)


def _mixer_call(kernel, name, n_seq, in_specs, out_specs, out_shape, scratch, args):
    return pl.pallas_call(
        kernel, grid=(n_seq,), in_specs=in_specs, out_specs=out_specs, out_shape=out_shape,
        scratch_shapes=scratch, compiler_params=_cparams(1), name=name)(*args)


def _head_norm_gate(o, bd, ng, gate):
    ss = _gsum(o * o, bd)
    return o * lax.rsqrt(ss * (1.0 / HEAD_DIM) + EPS) * ng * gate


def _epilogue(T, of_s, ob_s, gate_ref, gate_fn, bd_ref, ng_ref, o_ref):
    rb = 256
    bd = bd_ref[...]
    ng = ng_ref[...]

    def blk(i, carry):
        r0 = pl.multiple_of(i * rb, rb)
        o = of_s[pl.ds(r0, rb), :] + ob_s[pl.ds(r0, rb), :]
        o_ref[pl.ds(r0, rb), :] = _head_norm_gate(o, bd, ng, gate_fn(gate_ref[pl.ds(r0, rb), :]))
        return carry

    lax.fori_loop(0, T // rb, blk, 0)


def _chunk_rows(d, n, n_chunks):
    c = n if d == 0 else n_chunks - 1 - n
    return pl.multiple_of(c * CHUNK, CHUNK)


def _hs(h):
    return slice(h * HEAD_DIM, (h + 1) * HEAD_DIM)


def _ret_kernel(*refs, T, has_cache):
    if has_cache:
        q_ref, k_ref, v_ref, gt_ref, dl_ref, ng_ref, bd_ref, r0_ref, o_ref, of_s, ob_s, st_s = refs
    else:
        q_ref, k_ref, v_ref, gt_ref, dl_ref, ng_ref, bd_ref, o_ref, rf_ref, of_s, ob_s, st_s = refs
    n_chunks = T // CHUNK
    masks = _tri_masks()
    ii = lax.broadcasted_iota(jnp.int32, (CHUNK, CHUNK), 0)
    jj = lax.broadcasted_iota(jnp.int32, (CHUNK, CHUNK), 1)
    rel = (ii - jj).astype(F32)
    pos = lax.broadcasted_iota(jnp.int32, (CHUNK, 1), 0).astype(F32)
    lg_all = _log_sigmoid(dl_ref[...])

    dec, qdec, kdec, cdec = {}, {}, {}, {}
    for d in range(2):
        for h in range(N_HEADS):
            lg = lg_all[d:d + 1, h:h + 1]
            if d == 0:
                e, qd, kd = rel * lg, (pos + 1.0) * lg, (CHUNK - 1.0 - pos) * lg
            else:
                e, qd, kd = -rel * lg, (CHUNK - pos) * lg, pos * lg
            dec[d, h] = jnp.exp(jnp.where(masks[d][0], e, NEG))
            qdec[d, h], kdec[d, h], cdec[d, h] = jnp.exp(qd), jnp.exp(kd), jnp.exp(CHUNK * lg)
            st_s[d, h] = r0_ref[d, h] if has_cache else jnp.zeros((HEAD_DIM, HEAD_DIM), F32)

    def body(n, carry):
        for d in range(2):
            r0 = _chunk_rows(d, n, n_chunks)
            qc = q_ref[pl.ds(r0, CHUNK), :]
            kc = k_ref[pl.ds(r0, CHUNK), :] * (HEAD_DIM ** -0.5)
            vc = v_ref[pl.ds(r0, CHUNK), :]
            outs = []
            for h in range(N_HEADS):
                qh, kh, vh = qc[:, _hs(h)], kc[:, _hs(h)], vc[:, _hs(h)]
                r = st_s[d, h]
                attn = _mm3(qh, kh, NT) * dec[d, h]
                outs.append(_mm3(attn, vh) + _mm3(qh * qdec[d, h], r))
                st_s[d, h] = cdec[d, h] * r + _mm3(kh * kdec[d, h], vh, TN)
            (of_s if d == 0 else ob_s)[pl.ds(r0, CHUNK), :] = jnp.concatenate(outs, axis=1)
        return carry

    lax.fori_loop(0, n_chunks, body, 0)
    if not has_cache:
        rf_ref[...] = st_s[...]
    _epilogue(T, of_s, ob_s, gt_ref, _silu, bd_ref, ng_ref, o_ref)


def _retention(P, T, n_seq, decay_logit, ng, bd, l, state):
    has_cache = state is not None
    st = (2, N_HEADS, HEAD_DIM, HEAD_DIM)
    in_specs = _seq_specs(T, (S_DQ, S_DK, S_DV, S_DG)) + [
        _const_spec((2, N_HEADS)), _const_spec((1, SEG)), _const_spec((SEG, SEG))]
    args = [P, P, P, P, decay_logit, ng, bd]
    out_specs = [pl.BlockSpec((T, SEG), lambda s: (s, 0))]
    out_shape = [jax.ShapeDtypeStruct((n_seq * T, SEG), F32)]
    if has_cache:
        in_specs.append(_state_in_spec(l, st))
        args.append(state)
    else:
        out_specs.append(_lead_spec(st))
        out_shape.append(jax.ShapeDtypeStruct((n_seq,) + st, F32))
    scratch = [pltpu.VMEM((T, SEG), F32), pltpu.VMEM((T, SEG), F32), pltpu.VMEM(st, F32)]
    return _mixer_call(functools.partial(_ret_kernel, T=T, has_cache=has_cache), "retention",
                       n_seq, in_specs, out_specs, out_shape, scratch, args)


def _mlstm_kernel(*refs, T, has_cache):
    if has_cache:
        (q_ref, k_ref, v_ref, og_ref, gt_ref, gp_ref, ng_ref, bd_ref, c0_ref, m0_ref,
         o_ref, of_s, ob_s, st_s, m_s) = refs
    else:
        (q_ref, k_ref, v_ref, og_ref, gt_ref, gp_ref, ng_ref, bd_ref,
         o_ref, cf_ref, mf_ref, of_s, ob_s, st_s, m_s) = refs
    n_chunks = T // CHUNK
    masks = _tri_masks()
    tri = (masks[0][0].astype(BF16), masks[1][0].astype(BF16))
    eye = _eye(GATE_LANES)
    ones_col = (lax.broadcasted_iota(jnp.int32, (CHUNK, HEAD_DIM), 1) == 0).astype(F32)
    bias = gp_ref[0:1, :]

    if has_cache:
        st_s[...] = c0_ref[...]
        m_s[...] = m0_ref[...]
    else:
        st_s[...] = jnp.zeros(st_s.shape, F32)
        m_s[...] = jnp.zeros(m_s.shape, F32)

    def body(n, carry):
        for d in range(2):
            r0 = _chunk_rows(d, n, n_chunks)
            qc = q_ref[pl.ds(r0, CHUNK), :] * (HEAD_DIM ** -0.5)
            kc = k_ref[pl.ds(r0, CHUNK), :]
            vc = v_ref[pl.ds(r0, CHUNK), :]
            pre = gt_ref[pl.ds(r0, CHUNK), :] + bias
            b = _sel_mm(tri[d], _log_sigmoid(pre))
            b_t = _sel_mm(eye, b, NT)
            ig_t = _sel_mm(eye, pre, NT)
            outs = []
            for h in range(N_HEADS):
                r = d * N_HEADS + h
                qh, kh, vh = qc[:, _hs(h)], kc[:, _hs(h)], vc[:, _hs(h)]
                b_col, b_row = b[:, L_CF + r:L_CF + r + 1], b_t[L_CF + r:L_CF + r + 1, :]
                ig_col, ig_row = pre[:, L_CI + r:L_CI + r + 1], ig_t[L_CI + r:L_CI + r + 1, :]
                b_last = b_row[:, CHUNK - 1:CHUNK] if d == 0 else b_row[:, 0:1]
                m_prev = m_s[r:r + 1, 0:1]
                caug = st_s[d, h]
                vaug = jnp.concatenate([vh, ones_col], axis=1)

                dm = jnp.where(masks[d][0], b_col - b_row + ig_row, NEG)
                inter = b_col + m_prev
                m_i = jnp.maximum(inter, jnp.max(dm, axis=1, keepdims=True))
                s = _mm3(qh, kh, NT) * jnp.exp(dm - m_i)
                acc = jnp.exp(inter - m_i) * _mm3(qh, caug) + _mm3(s, vaug)
                den = acc[:, HEAD_DIM:HEAD_DIM + 1]
                outs.append(acc[:, :HEAD_DIM] / jnp.maximum(jnp.abs(den), jnp.exp(-m_i)))

                m_new = jnp.maximum(b_last + m_prev,
                                    jnp.max(b_last - b_row + ig_row, axis=1, keepdims=True))
                wk = kh * jnp.exp(b_last - b_col + ig_col - m_new)
                st_s[d, h] = jnp.exp(b_last + m_prev - m_new) * caug + _mm3(wk, vaug, TN)
                m_s[r:r + 1, :] = jnp.broadcast_to(m_new, (1, GATE_LANES))
            (of_s if d == 0 else ob_s)[pl.ds(r0, CHUNK), :] = jnp.concatenate(outs, axis=1)
        return carry

    lax.fori_loop(0, n_chunks, body, 0)
    if not has_cache:
        cf_ref[...] = st_s[...]
        mf_ref[...] = m_s[...]
    _epilogue(T, of_s, ob_s, og_ref, _sigmoid, bd_ref, ng_ref, o_ref)


def _mlstm(P, G, T, n_seq, gate_par, ng, bd, state):
    has_cache = state is not None
    st = (2, N_HEADS, HEAD_DIM, 2 * HEAD_DIM)
    ms = (2 * N_HEADS, GATE_LANES)
    in_specs = _seq_specs(T, (S_CQ, S_CK, S_CV, S_CO)) + [
        _gate_spec(T), _const_spec((8, GATE_LANES)), _const_spec((1, SEG)), _const_spec((SEG, SEG))]
    args = [P, P, P, P, G, gate_par, ng, bd]
    out_specs = [pl.BlockSpec((T, SEG), lambda s: (s, 0))]
    out_shape = [jax.ShapeDtypeStruct((n_seq * T, SEG), F32)]
    if has_cache:
        in_specs += [_lead_spec(st), _lead_spec(ms)]
        args += list(state)
    else:
        out_specs += [_lead_spec(st), _lead_spec(ms)]
        out_shape += [jax.ShapeDtypeStruct((n_seq,) + st, F32), jax.ShapeDtypeStruct((n_seq,) + ms, F32)]
    scratch = [pltpu.VMEM((T, SEG), F32), pltpu.VMEM((T, SEG), F32), pltpu.VMEM(st, F32), pltpu.VMEM(ms, F32)]
    return _mixer_call(functools.partial(_mlstm_kernel, T=T, has_cache=has_cache), "mlstm",
                       n_seq, in_specs, out_specs, out_shape, scratch, args)


def _delta_kernel(*refs, T, has_cache):
    if has_cache:
        (q_ref, k_ref, v_ref, z_ref, gt_ref, gp_ref, cw_ref, ng_ref, bd_ref, s0_ref,
         o_ref, of_s, ob_s, st_s, qs, ks, vs) = refs
    else:
        (q_ref, k_ref, v_ref, z_ref, gt_ref, gp_ref, cw_ref, ng_ref, bd_ref,
         o_ref, sf_ref, of_s, ob_s, st_s, qs, ks, vs) = refs
    n_chunks = T // CHUNK
    n_blk = T // ROW_BLOCK
    masks = _tri_masks()
    tri = (masks[0][0].astype(BF16), masks[1][0].astype(BF16))
    eye = _eye(GATE_LANES)
    bd = bd_ref[...]
    bias, a_log = gp_ref[0:1, :], gp_ref[1:2, :]
    row = lax.broadcasted_iota(jnp.int32, (ROW_BLOCK, 1), 0)

    def prologue(i, carry):
        r0 = pl.multiple_of(i * ROW_BLOCK, ROW_BLOCK)
        rp = pl.multiple_of(jnp.maximum(r0 - 8, 0), 8)
        rn = pl.multiple_of(jnp.minimum(r0 + ROW_BLOCK, T - 8), 8)
        for j, (src, dst) in enumerate(((q_ref, qs), (k_ref, ks), (v_ref, vs))):
            cur = src[pl.ds(r0, ROW_BLOCK), :]
            before = jnp.where(i > 0, src[pl.ds(rp, 8), :][7:8, :], 0.0)
            after = jnp.where(i < n_blk - 1, src[pl.ds(rn, 8), :][0:1, :], 0.0)
            down = jnp.where(row == 0, before, pltpu.roll(cur, 1, axis=0))
            up = jnp.where(row == ROW_BLOCK - 1, after, pltpu.roll(cur, ROW_BLOCK - 1, axis=0))
            w = cw_ref[:, j * SEG:(j + 1) * SEG]
            y = _silu(w[0:1, :] * down + w[1:2, :] * cur + w[2:3, :] * up)
            if j < 2:
                y = y * lax.rsqrt(_gsum(y * y, bd) + EPS)
            if j == 0:
                y = y * (HEAD_DIM ** -0.5)
            dst[pl.ds(r0, ROW_BLOCK), :] = y
        return carry

    lax.fori_loop(0, n_blk, prologue, 0)

    for d in range(2):
        for h in range(N_HEADS):
            st_s[d, h] = s0_ref[d, h] if has_cache else jnp.zeros((HEAD_DIM, HEAD_DIM), F32)

    def body(n, carry):
        for d in range(2):
            r0 = _chunk_rows(d, n, n_chunks)
            qc, kc, vc = qs[pl.ds(r0, CHUNK), :], ks[pl.ds(r0, CHUNK), :], vs[pl.ds(r0, CHUNK), :]
            pre = gt_ref[pl.ds(r0, CHUNK), :]
            beta_all = _sigmoid(pre)
            g_all = -jnp.exp(a_log) * _softplus(pre + bias)
            cg = _sel_mm(tri[d], g_all)
            cg_t = _sel_mm(eye, cg, NT)
            outs = []
            for h in range(N_HEADS):
                r = d * N_HEADS + h
                qh, kh, vh = qc[:, _hs(h)], kc[:, _hs(h)], vc[:, _hs(h)]
                beta = beta_all[:, L_BETA + r:L_BETA + r + 1]
                cg_col, cg_row = cg[:, L_ALPHA + r:L_ALPHA + r + 1], cg_t[L_ALPHA + r:L_ALPHA + r + 1, :]
                g_last = cg_row[:, CHUNK - 1:CHUNK] if d == 0 else cg_row[:, 0:1]
                decay = jnp.exp(jnp.where(masks[d][0], cg_col - cg_row, NEG))
                kb = kh * beta
                a = jnp.where(masks[d][1], _mm3(kb, kh, NT) * decay, 0.0)
                x = jnp.concatenate([vh * beta, kb * jnp.exp(cg_col)], axis=1)
                x = x - _mm3(a, x)
                p = a
                for _ in range(5):
                    p = _mm3(p, p)
                    x = x + _mm3(p, x)
                u, w = x[:, :HEAD_DIM], x[:, HEAD_DIM:]
                s = st_s[d, h]
                v_new = u - _mm3(w, s)
                attn = _mm3(qh, kh, NT) * decay
                outs.append(_mm3(qh * jnp.exp(cg_col), s) + _mm3(attn, v_new))
                st_s[d, h] = s * jnp.exp(g_last) + _mm3(kh * jnp.exp(g_last - cg_col), v_new, TN)
            (of_s if d == 0 else ob_s)[pl.ds(r0, CHUNK), :] = jnp.concatenate(outs, axis=1)
        return carry

    lax.fori_loop(0, n_chunks, body, 0)
    if not has_cache:
        sf_ref[...] = st_s[...]
    _epilogue(T, of_s, ob_s, z_ref, _silu, bd_ref, ng_ref, o_ref)


def _deltanet(P, G, T, n_seq, gate_par, conv_w, ng, bd, l, state):
    has_cache = state is not None
    st = (2, N_HEADS, HEAD_DIM, HEAD_DIM)
    in_specs = _seq_specs(T, (S_AQ, S_AK, S_AV, S_AZ)) + [
        _gate_spec(T), _const_spec((8, GATE_LANES)), _const_spec((3, 3 * SEG)),
        _const_spec((1, SEG)), _const_spec((SEG, SEG))]
    args = [P, P, P, P, G, gate_par, conv_w, ng, bd]
    out_specs = [pl.BlockSpec((T, SEG), lambda s: (s, 0))]
    out_shape = [jax.ShapeDtypeStruct((n_seq * T, SEG), F32)]
    if has_cache:
        in_specs.append(_state_in_spec(l, st))
        args.append(state)
    else:
        out_specs.append(_lead_spec(st))
        out_shape.append(jax.ShapeDtypeStruct((n_seq,) + st, F32))
    scratch = [pltpu.VMEM((T, SEG), F32)] * 2 + [pltpu.VMEM(st, F32)] + [pltpu.VMEM((T, SEG), F32)] * 3
    return _mixer_call(functools.partial(_delta_kernel, T=T, has_cache=has_cache), "deltanet",
                       n_seq, in_specs, out_specs, out_shape, scratch, args)


def _diff_kernel(*refs, T, has_cache, lam_init):
    if has_cache:
        (q_ref, k_ref, v_ref, qg_ref, kg_ref, lam_ref, ng_ref, bd32_ref, bd64_ref,
         cos_ref, sin_ref, ck_ref, cv_ref, o_ref, qs, ks) = refs
    else:
        (q_ref, k_ref, v_ref, qg_ref, kg_ref, lam_ref, ng_ref, bd32_ref, bd64_ref,
         o_ref, ko_ref, vo_ref, qs, ks) = refs
    n_blk = T // ROW_BLOCK
    bd32 = bd32_ref[...]
    lane = lax.broadcasted_iota(jnp.int32, (1, SEG), 1)
    first_half = (lane % 16) < 8

    def prologue(i, carry):
        r0 = pl.multiple_of(i * ROW_BLOCK, ROW_BLOCK)
        for src, g_ref, dst in ((q_ref, qg_ref, qs), (k_ref, kg_ref, ks)):
            x = src[pl.ds(r0, ROW_BLOCK), :]
            y = x * lax.rsqrt(_gsum(x * x, bd32) * (1.0 / DQK) + EPS) * g_ref[...]
            if has_cache:
                partner = jnp.where(first_half, pltpu.roll(y, SEG - 8, axis=1), pltpu.roll(y, 8, axis=1))
                y = y * cos_ref[pl.ds(r0, ROW_BLOCK), :] + partner * sin_ref[pl.ds(r0, ROW_BLOCK), :]
            dst[pl.ds(r0, ROW_BLOCK), :] = y
        return carry

    lax.fori_loop(0, n_blk, prologue, 0)

    if not has_cache:
        for h in range(N_HEADS):
            ko_ref[h] = ks[:, _hs(h)]
            vo_ref[h] = v_ref[:, _hs(h)]

    lp = lam_ref[...]
    lam = (jnp.exp(jnp.sum(lp[0:1, :] * lp[1:2, :], axis=1, keepdims=True))
           - jnp.exp(jnp.sum(lp[2:3, :] * lp[3:4, :], axis=1, keepdims=True)) + lam_init)
    scale = DQK ** -0.5
    comp1 = lax.broadcasted_iota(jnp.int32, (1, HEAD_DIM), 1) < DQK
    bd64 = bd64_ref[...]
    ng = ng_ref[...]

    def softmax_parts(scores):
        m = scores[0].max(axis=1, keepdims=True)
        for s in scores[1:]:
            m = jnp.maximum(m, s.max(axis=1, keepdims=True))
        es = [jnp.exp(s - m) for s in scores]
        tot = es[0].sum(axis=1, keepdims=True)
        for e in es[1:]:
            tot = tot + e.sum(axis=1, keepdims=True)
        inv = 1.0 / tot
        return [e * inv for e in es]

    def qblock(i, carry):
        r0 = pl.multiple_of(i * Q_BLOCK, Q_BLOCK)
        qb = qs[pl.ds(r0, Q_BLOCK), :]
        outs = []
        for h in range(N_HEADS):
            qh = qb[:, _hs(h)]
            q1 = jnp.where(comp1, qh, 0.0)
            q2 = qh - q1
            keys = [ks[:, _hs(h)]]
            vals = [v_ref[:, _hs(h)]]
            if has_cache:
                keys.insert(0, ck_ref[h])
                vals.insert(0, cv_ref[h])
            p1 = softmax_parts([_mm1(q1, kk, NT) * scale for kk in keys])
            p2 = softmax_parts([_mm1(q2, kk, NT) * scale for kk in keys])
            o = None
            for a, b, vv in zip(p1, p2, vals):
                t = _mm1(a - lam * b, vv)
                o = t if o is None else o + t
            outs.append(o)
        o = jnp.concatenate(outs, axis=1)
        o_ref[pl.ds(r0, Q_BLOCK), :] = _head_norm_gate(o, bd64, ng, 1.0 - lam_init)
        return carry

    lax.fori_loop(0, T // Q_BLOCK, qblock, 0)


def _diffattn(P, T, n_seq, qg, kg, lam_par, ng, bd32, bd64, l, rope, cache):
    has_cache = cache is not None
    lam_init = 0.8 - 0.6 * math.exp(-0.3 * l)
    kv = (N_HEADS, T, HEAD_DIM)
    in_specs = _seq_specs(T, (S_BQ, S_BK, S_BV)) + [
        _const_spec((1, SEG)), _const_spec((1, SEG)), _const_spec((4, DQK)), _const_spec((1, SEG)),
        _const_spec((SEG, SEG)), _const_spec((SEG, SEG))]
    args = [P, P, P, qg, kg, lam_par, ng, bd32, bd64]
    out_specs = [pl.BlockSpec((T, SEG), lambda s: (s, 0))]
    out_shape = [jax.ShapeDtypeStruct((n_seq * T, SEG), F32)]
    if has_cache:
        ckv = cache[0].shape[2:]
        in_specs += [_const_spec((T, SEG)), _const_spec((T, SEG)), _state_in_spec(l, ckv), _state_in_spec(l, ckv)]
        args += [rope[0], rope[1], cache[0], cache[1]]
    else:
        out_specs += [_lead_spec(kv), _lead_spec(kv)]
        out_shape += [jax.ShapeDtypeStruct((n_seq,) + kv, F32)] * 2
    scratch = [pltpu.VMEM((T, SEG), F32), pltpu.VMEM((T, SEG), F32)]
    return _mixer_call(functools.partial(_diff_kernel, T=T, has_cache=has_cache, lam_init=lam_init),
                       "diff_attn", n_seq, in_specs, out_specs, out_shape, scratch, args)


def _rope_tables(T):
    n_freq = DQK // 4
    t = jnp.arange(T)
    rows = (t // GRID_W).astype(F32)
    cols = (t % GRID_W).astype(F32)
    freqs = ROPE_BASE ** (-jnp.arange(n_freq, dtype=F32) / n_freq)
    ang_r, ang_c = rows[:, None] * freqs, cols[:, None] * freqs

    def comp(fn, sign):
        return jnp.concatenate([fn(ang_r), sign * fn(ang_r), fn(ang_c), sign * fn(ang_c)], axis=1)

    reps = SEG // DQK
    cos = jnp.tile(comp(jnp.cos, 1.0), (1, reps))
    sin = jnp.tile(jnp.concatenate([-jnp.sin(ang_r), jnp.sin(ang_r), -jnp.sin(ang_c), jnp.sin(ang_c)], axis=1),
                   (1, reps))
    return cos, sin


def _block_ones(group):
    i = np.arange(SEG)
    return jnp.asarray(i[:, None] // group == i[None, :] // group, BF16)


def kernel(x_prompt, x_sample, cache_diff_k, cache_diff_v, state_delta, state_mlstm_C, state_mlstm_n, state_mlstm_m, state_ret, c, c_ctx, w_ada, b_ada, norm_g, ffn_w_gate, ffn_w_up, ffn_w_down, w_in, dn_conv_w, dn_a_log, dn_dt_bias, dn_norm_g, da_qn_g, da_kn_g, da_lambda, da_norm_g, ml_i_bias, ml_f_bias, ml_norm_g, ret_decay_logit, ret_norm_g, w_branch, w_out):
    B, T, _ = x_prompt.shape
    Bs, Ts, _ = x_sample.shape
    cond8 = jnp.zeros((8, D_MODEL), F32).at[0].set(c_ctx).at[1:1 + Bs].set(c)
    mod = _ada(cond8, w_ada, b_ada).reshape(DEPTH, 8, N_MOD, D_MODEL)
    bd32, bd64 = _block_ones(DQK), _block_ones(HEAD_DIM)
    rope = _rope_tables(Ts)
    tile_heads = lambda g: jnp.tile(g, SEG // g.shape[0])[None]

    xs = {"ctx": x_prompt.reshape(B * T, D_MODEL), "smp": x_sample.reshape(Bs * Ts, D_MODEL)}
    geo = {"ctx": (T, B, _mod_spec(B * T, 0)), "smp": (Ts, Bs, _mod_spec(Ts, 1))}
    new = {k: [] for k in ("k", "v", "dn", "C", "m", "r")}
    for l in range(DEPTH):
        wg, wu, wd = ffn_w_gate[l].astype(BF16), ffn_w_up[l].astype(BF16), ffn_w_down[l].astype(BF16)
        cols = lambda f: w_in[l][:, _IN_OFFS[f]:_IN_OFFS[f + 1]]
        w_mix = jnp.concatenate([cols(f) for f in _SEG_FIELDS], axis=1).astype(BF16)
        w_gate = jnp.concatenate([cols(f) for f in _GATE_FIELDS]
                                 + [jnp.zeros((D_MODEL, GATE_LANES - 32), F32)], axis=1).astype(BF16)
        w_merge = cols(19).astype(BF16)
        wb, wo = w_branch[l].astype(BF16), w_out[l].astype(BF16)
        gate_par = jnp.zeros((8, GATE_LANES), F32)
        gate_par = gate_par.at[0, L_ALPHA:L_ALPHA + 8].set(dn_dt_bias[l].reshape(8))
        gate_par = gate_par.at[0, L_CI:L_CI + 8].set(ml_i_bias[l].reshape(8))
        gate_par = gate_par.at[0, L_CF:L_CF + 8].set(ml_f_bias[l].reshape(8))
        gate_par = gate_par.at[1, L_ALPHA:L_ALPHA + 8].set(dn_a_log[l].reshape(8))
        caug0 = jnp.concatenate([state_mlstm_C[:, l], state_mlstm_n[:, l][..., None],
                                 jnp.zeros((Bs, 2, N_HEADS, HEAD_DIM, HEAD_DIM - 1), F32)], axis=-1)
        m0 = jnp.broadcast_to(state_mlstm_m[:, l].reshape(Bs, 2 * N_HEADS, 1), (Bs, 2 * N_HEADS, GATE_LANES))
        for path in ("ctx", "smp"):
            Tp, n_seq, mspec = geo[path]
            smp = path == "smp"
            x = _ffn(xs[path], mod[l], mspec, norm_g[l, 0][None], wg[0], wu[0], wd[0], 0)
            P, G = _inproj(x, mod[l], mspec, norm_g[l, 1][None], w_mix, w_gate)
            ra = _deltanet(P, G, Tp, n_seq, gate_par, dn_conv_w[l], tile_heads(dn_norm_g[l]), bd64, l,
                           state_delta if smp else None)
            rb = _diffattn(P, Tp, n_seq, tile_heads(da_qn_g[l]), tile_heads(da_kn_g[l]), da_lambda[l],
                           tile_heads(da_norm_g[l]), bd32, bd64, l, rope if smp else None,
                           (cache_diff_k, cache_diff_v) if smp else None)
            rc = _mlstm(P, G, Tp, n_seq, gate_par, tile_heads(ml_norm_g[l]), bd64, (caug0, m0) if smp else None)
            rd = _retention(P, Tp, n_seq, ret_decay_logit[l], tile_heads(ret_norm_g[l]), bd64, l,
                            state_ret if smp else None)
            if not smp:
                new["dn"].append(ra[1])
                new["k"].append(rb[1])
                new["v"].append(rb[2])
                new["C"].append(rc[1])
                new["m"].append(rc[2])
                new["r"].append(rd[1])
            x = _merge(x, mod[l], mspec, norm_g[l, 1][None], (ra[0], rb[0], rc[0], rd[0]), w_merge, wb, wo)
            xs[path] = _ffn(x, mod[l], mspec, norm_g[l, 2][None], wg[1], wu[1], wd[1], 6)
    st = lambda k: jnp.stack(new[k], axis=1)
    caug = st("C")
    return (xs["ctx"].reshape(B, T, D_MODEL), xs["smp"].reshape(Bs, Ts, D_MODEL),
            st("k"), st("v"), st("dn"),
            caug[..., :HEAD_DIM], caug[..., HEAD_DIM], st("m")[..., 0].reshape(B, DEPTH, 2, N_HEADS),
            st("r"))
```

```python
import functools
import math

import numpy as np
import jax
import jax.numpy as jnp
from jax import lax
from jax.experimental import pallas as pl
from jax.experimental.pallas import tpu as pltpu

F32 = jnp.float32
BF16 = jnp.bfloat16

D_MODEL = 1024
FFN_DIM = 2816
N_MOD = 9
DEPTH = 2
N_HEADS = 4
HEAD_DIM = 64
SEG = N_HEADS * HEAD_DIM
N_SEG = 15
CHUNK = 64
DQK = 32
GRID_W = 64
ROPE_BASE = 10000.0
EPS = 1e-6
N_BRANCH = 4
GATE_LANES = 128
NEG = -1e30

TM = 256
Q_BLOCK = 256
ROW_BLOCK = 128
VMEM_LIMIT = 56 * 1024 * 1024

NN = (((1,), (0,)), ((), ()))
NT = (((1,), (1,)), ((), ()))
TN = (((0,), (0,)), ((), ()))

_IN_SIZES = (256, 256, 256, 256, 8, 8, 256, 256, 256, 256, 256, 256, 256, 8, 8, 256, 256, 256, 256, 4096)
_IN_OFFS = np.concatenate([[0], np.cumsum(_IN_SIZES)]).tolist()
_SEG_FIELDS = (0, 1, 2, 3, 6, 7, 8, 9, 10, 11, 12, 15, 16, 17, 18)
_GATE_FIELDS = (4, 5, 13, 14)
(S_AQ, S_AK, S_AV, S_AZ, S_BQ, S_BK, S_BV, S_CQ, S_CK, S_CV, S_CO, S_DQ, S_DK, S_DV, S_DG) = range(N_SEG)
L_BETA, L_ALPHA, L_CI, L_CF = 0, 8, 16, 24


def _dg(a, b, dims):
    return lax.dot_general(a, b, dims, preferred_element_type=F32)


def _split2(x):
    hi = x.astype(BF16)
    lo = (x - hi.astype(F32)).astype(BF16)
    return hi, lo


def _mm1(a, b, dims=NN):
    return _dg(a.astype(BF16), b.astype(BF16), dims)


def _mm3(a, b, dims=NN):
    ah, al = _split2(a)
    bh, bl = _split2(b)
    return _dg(ah, bh, dims) + (_dg(ah, bl, dims) + _dg(al, bh, dims))


def _sel_mm(sel, x, dims=NN):
    h0 = x.astype(BF16)
    r1 = x - h0.astype(F32)
    h1 = r1.astype(BF16)
    h2 = (r1 - h1.astype(F32)).astype(BF16)
    return _dg(sel, h0, dims) + (_dg(sel, h1, dims) + _dg(sel, h2, dims))


def _gsum(x, bd):
    hi, lo = _split2(x)
    return _dg(hi, bd, NN) + _dg(lo, bd, NN)


def _sigmoid(x):
    return 1.0 / (1.0 + jnp.exp(-x))


def _silu(x):
    return x * _sigmoid(x)


def _softplus(x):
    return jnp.maximum(x, 0.0) + jnp.log1p(jnp.exp(-jnp.abs(x)))


def _log_sigmoid(x):
    return -_softplus(-x)


def _norm_mod(x, g, shift, scale):
    ms = jnp.mean(x * x, axis=-1, keepdims=True)
    return (x * lax.rsqrt(ms + EPS) * g) * (1.0 + scale) + shift


def _tri_masks():
    ii = lax.broadcasted_iota(jnp.int32, (CHUNK, CHUNK), 0)
    jj = lax.broadcasted_iota(jnp.int32, (CHUNK, CHUNK), 1)
    return ((jj <= ii, jj < ii), (jj >= ii, jj > ii))


def _eye(n):
    ii = lax.broadcasted_iota(jnp.int32, (n, n), 0)
    jj = lax.broadcasted_iota(jnp.int32, (n, n), 1)
    return (ii == jj).astype(BF16)


def _cparams(n_grid=1):
    return pltpu.CompilerParams(dimension_semantics=("arbitrary",) * n_grid,
                                vmem_limit_bytes=VMEM_LIMIT)


def _const_spec(shape):
    nd = len(shape)
    return pl.BlockSpec(shape, lambda *_: (0,) * nd)


def _ada_kernel(s_ref, w_ref, b_ref, o_ref):
    s = s_ref[...]
    o_ref[...] = _mm3(_silu(s), w_ref[...]) + b_ref[...]


def _ada(cond8, w_ada, b_ada):
    tn = 1536
    n_t = (N_MOD * D_MODEL) // tn
    return pl.pallas_call(
        _ada_kernel,
        grid=(DEPTH, n_t),
        in_specs=[pl.BlockSpec((8, D_MODEL), lambda l, j: (0, 0)),
                  pl.BlockSpec((None, D_MODEL, tn), lambda l, j: (l, 0, j)),
                  pl.BlockSpec((None, 1, tn), lambda l, j: (l, 0, j))],
        out_specs=pl.BlockSpec((None, 8, tn), lambda l, j: (l, 0, j)),
        out_shape=jax.ShapeDtypeStruct((DEPTH, 8, N_MOD * D_MODEL), F32),
        compiler_params=_cparams(2),
        name="ada_mod",
    )(cond8, w_ada, b_ada.reshape(DEPTH, 1, N_MOD * D_MODEL))


def _ffn_kernel(x_ref, mod_ref, g_ref, wg_ref, wu_ref, wd_ref, o_ref, *, mi):
    x = x_ref[...]
    h = _norm_mod(x, g_ref[...], mod_ref[mi:mi + 1, :], mod_ref[mi + 1:mi + 2, :]).astype(BF16)
    gate = jnp.dot(h, wg_ref[...], preferred_element_type=F32)
    up = jnp.dot(h, wu_ref[...], preferred_element_type=F32)
    act = (_silu(gate) * up).astype(BF16)
    y = jnp.dot(act, wd_ref[...], preferred_element_type=F32)
    o_ref[...] = x + (0.5 * mod_ref[mi + 2:mi + 3, :]) * y


def _mod_spec(rows_per_cond, first_cond):
    per = rows_per_cond // TM
    return pl.BlockSpec((None, N_MOD, D_MODEL), lambda i: (first_cond + i // per, 0, 0))


def _ffn(x, mod, mod_spec, g, wg, wu, wd, mi):
    n = x.shape[0]
    return pl.pallas_call(
        functools.partial(_ffn_kernel, mi=mi),
        grid=(n // TM,),
        in_specs=[pl.BlockSpec((TM, D_MODEL), lambda i: (i, 0)),
                  mod_spec,
                  _const_spec((1, D_MODEL)),
                  _const_spec((D_MODEL, FFN_DIM)),
                  _const_spec((D_MODEL, FFN_DIM)),
                  _const_spec((FFN_DIM, D_MODEL))],
        out_specs=pl.BlockSpec((TM, D_MODEL), lambda i: (i, 0)),
        out_shape=jax.ShapeDtypeStruct((n, D_MODEL), F32),
        compiler_params=_cparams(1),
        name="ffn",
    )(x, mod, g, wg, wu, wd)


def _inproj_kernel(x_ref, mod_ref, g_ref, w_ref, wgt_ref, p_ref, gt_ref):
    h = _norm_mod(x_ref[...], g_ref[...], mod_ref[3:4, :], mod_ref[4:5, :]).astype(BF16)
    step = 3 * SEG
    for j in range(0, N_SEG * SEG, step):
        p_ref[:, j:j + step] = jnp.dot(h, w_ref[:, j:j + step], preferred_element_type=F32)
    gt_ref[...] = jnp.dot(h, wgt_ref[...], preferred_element_type=F32)


def _inproj(x, mod, mod_spec, g, w_mix, w_gate):
    n = x.shape[0]
    return pl.pallas_call(
        _inproj_kernel,
        grid=(n // TM,),
        in_specs=[pl.BlockSpec((TM, D_MODEL), lambda i: (i, 0)),
                  mod_spec,
                  _const_spec((1, D_MODEL)),
                  _const_spec((D_MODEL, N_SEG * SEG)),
                  _const_spec((D_MODEL, GATE_LANES))],
        out_specs=[pl.BlockSpec((TM, N_SEG * SEG), lambda i: (i, 0)),
                   pl.BlockSpec((TM, GATE_LANES), lambda i: (i, 0))],
        out_shape=[jax.ShapeDtypeStruct((n, N_SEG * SEG), F32),
                   jax.ShapeDtypeStruct((n, GATE_LANES), F32)],
        compiler_params=_cparams(1),
        name="in_proj",
    )(x, mod, g, w_mix, w_gate)


def _merge_kernel(x_ref, mod_ref, g_ref, ba_ref, bb_ref, bc_ref, bd_ref, wm_ref, wb_ref, wo_ref, o_ref):
    x = x_ref[...]
    h = _norm_mod(x, g_ref[...], mod_ref[3:4, :], mod_ref[4:5, :]).astype(BF16)
    mixed = None
    for m, b_ref in enumerate((ba_ref, bb_ref, bc_ref, bd_ref)):
        logits = jnp.dot(h, wm_ref[:, m * D_MODEL:(m + 1) * D_MODEL], preferred_element_type=F32)
        pb = jnp.dot(b_ref[...].astype(BF16), wb_ref[m], preferred_element_type=F32)
        term = _sigmoid(logits) * pb
        mixed = term if mixed is None else mixed + term
    y = jnp.dot(mixed.astype(BF16), wo_ref[...], preferred_element_type=F32)
    o_ref[...] = x + mod_ref[5:6, :] * y


def _merge(x, mod, mod_spec, g, branches, w_merge, w_branch, w_out):
    n = x.shape[0]
    row = lambda w: pl.BlockSpec((TM, w), lambda i: (i, 0))
    return pl.pallas_call(
        _merge_kernel,
        grid=(n // TM,),
        in_specs=[row(D_MODEL), mod_spec, _const_spec((1, D_MODEL)),
                  row(SEG), row(SEG), row(SEG), row(SEG),
                  _const_spec((D_MODEL, N_BRANCH * D_MODEL)),
                  _const_spec((N_BRANCH, SEG, D_MODEL)),
                  _const_spec((D_MODEL, D_MODEL))],
        out_specs=row(D_MODEL),
        out_shape=jax.ShapeDtypeStruct((n, D_MODEL), F32),
        compiler_params=_cparams(1),
        name="merge",
    )(x, mod, g, *branches, w_merge, w_branch, w_out)


def _seq_specs(T, segs):
    return [pl.BlockSpec((T, SEG), lambda s, j=j: (s, j)) for j in segs]


def _gate_spec(T):
    return pl.BlockSpec((T, GATE_LANES), lambda s: (s, 0))


def _state_in_spec(l, tail):
    nd = len(tail)
    return pl.BlockSpec((None, None) + tail, lambda s: (s, l) + (0,) * nd)


def _lead_spec(tail):
    nd = len(tail)
    return pl.BlockSpec((None,) + tail, lambda s: (s,) + (0,) * nd)


def _mixer_call(kernel, name, n_seq, in_specs, out_specs, out_shape, scratch, args):
    return pl.pallas_call(
        kernel, grid=(n_seq,), in_specs=in_specs, out_specs=out_specs, out_shape=out_shape,
        scratch_shapes=scratch, compiler_params=_cparams(1), name=name)(*args)


def _head_norm_gate(o, bd, ng, gate):
    ss = _gsum(o * o, bd)
    return o * lax.rsqrt(ss * (1.0 / HEAD_DIM) + EPS) * ng * gate


def _epilogue(T, of_s, ob_s, gate_ref, gate_fn, bd_ref, ng_ref, o_ref):
    rb = 256
    bd = bd_ref[...]
    ng = ng_ref[...]

    def blk(i, carry):
        r0 = pl.multiple_of(i * rb, rb)
        o = of_s[pl.ds(r0, rb), :] + ob_s[pl.ds(r0, rb), :]
        o_ref[pl.ds(r0, rb), :] = _head_norm_gate(o, bd, ng, gate_fn(gate_ref[pl.ds(r0, rb), :]))
        return carry

    lax.fori_loop(0, T // rb, blk, 0)


def _chunk_rows(d, n, n_chunks):
    c = n if d == 0 else n_chunks - 1 - n
    return pl.multiple_of(c * CHUNK, CHUNK)


def _hs(h):
    return slice(h * HEAD_DIM, (h + 1) * HEAD_DIM)


UNITS = tuple((d, h) for d in range(2) for h in range(N_HEADS))


def _ret_kernel(*refs, T, has_cache):
    if has_cache:
        q_ref, k_ref, v_ref, gt_ref, dl_ref, ng_ref, bd_ref, r0_ref, o_ref, of_s, ob_s, st_s = refs
    else:
        q_ref, k_ref, v_ref, gt_ref, dl_ref, ng_ref, bd_ref, o_ref, rf_ref, of_s, ob_s, st_s = refs
    n_chunks = T // CHUNK
    masks = _tri_masks()
    ii = lax.broadcasted_iota(jnp.int32, (CHUNK, CHUNK), 0)
    jj = lax.broadcasted_iota(jnp.int32, (CHUNK, CHUNK), 1)
    rel = (ii - jj).astype(F32)
    pos = lax.broadcasted_iota(jnp.int32, (CHUNK, 1), 0).astype(F32)
    lg_all = _log_sigmoid(dl_ref[...])

    dec, qdec, kdec, cdec = {}, {}, {}, {}
    for d in range(2):
        for h in range(N_HEADS):
            lg = lg_all[d:d + 1, h:h + 1]
            if d == 0:
                e, qd, kd = rel * lg, (pos + 1.0) * lg, (CHUNK - 1.0 - pos) * lg
            else:
                e, qd, kd = -rel * lg, (CHUNK - pos) * lg, pos * lg
            dec[d, h] = jnp.exp(jnp.where(masks[d][0], e, NEG))
            qdec[d, h], kdec[d, h], cdec[d, h] = jnp.exp(qd), jnp.exp(kd), jnp.exp(CHUNK * lg)
            st_s[d, h] = r0_ref[d, h] if has_cache else jnp.zeros((HEAD_DIM, HEAD_DIM), F32)

    def body(n, carry):
        rows = [_chunk_rows(d, n, n_chunks) for d in range(2)]
        qc = [q_ref[pl.ds(r0, CHUNK), :] for r0 in rows]
        kc = [k_ref[pl.ds(r0, CHUNK), :] * (HEAD_DIM ** -0.5) for r0 in rows]
        vc = [v_ref[pl.ds(r0, CHUNK), :] for r0 in rows]
        qh = [qc[d][:, _hs(h)] for d, h in UNITS]
        kh = [kc[d][:, _hs(h)] for d, h in UNITS]
        vh = [vc[d][:, _hs(h)] for d, h in UNITS]
        r = [st_s[d, h] for d, h in UNITS]
        qk = [_mm3(q, k, NT) for q, k in zip(qh, kh)]
        qr = [_mm3(q * qdec[u], x) for q, x, u in zip(qh, r, UNITS)]
        kv = [_mm3(k * kdec[u], v, TN) for k, v, u in zip(kh, vh, UNITS)]
        av = [_mm3(s * dec[u], v) for s, v, u in zip(qk, vh, UNITS)]
        for i, u in enumerate(UNITS):
            st_s[u] = cdec[u] * r[i] + kv[i]
        for d, o_s in enumerate((of_s, ob_s)):
            o_s[pl.ds(rows[d], CHUNK), :] = jnp.concatenate(
                [av[i] + qr[i] for i, u in enumerate(UNITS) if u[0] == d], axis=1)
        return carry

    lax.fori_loop(0, n_chunks, body, 0)
    if not has_cache:
        rf_ref[...] = st_s[...]
    _epilogue(T, of_s, ob_s, gt_ref, _silu, bd_ref, ng_ref, o_ref)


def _retention(P, T, n_seq, decay_logit, ng, bd, l, state):
    has_cache = state is not None
    st = (2, N_HEADS, HEAD_DIM, HEAD_DIM)
    in_specs = _seq_specs(T, (S_DQ, S_DK, S_DV, S_DG)) + [
        _const_spec((2, N_HEADS)), _const_spec((1, SEG)), _const_spec((SEG, SEG))]
    args = [P, P, P, P, decay_logit, ng, bd]
    out_specs = [pl.BlockSpec((T, SEG), lambda s: (s, 0))]
    out_shape = [jax.ShapeDtypeStruct((n_seq * T, SEG), F32)]
    if has_cache:
        in_specs.append(_state_in_spec(l, st))
        args.append(state)
    else:
        out_specs.append(_lead_spec(st))
        out_shape.append(jax.ShapeDtypeStruct((n_seq,) + st, F32))
    scratch = [pltpu.VMEM((T, SEG), F32), pltpu.VMEM((T, SEG), F32), pltpu.VMEM(st, F32)]
    return _mixer_call(functools.partial(_ret_kernel, T=T, has_cache=has_cache), "retention",
                       n_seq, in_specs, out_specs, out_shape, scratch, args)


def _mlstm_kernel(*refs, T, has_cache):
    if has_cache:
        (q_ref, k_ref, v_ref, og_ref, gt_ref, gp_ref, ng_ref, bd_ref, c0_ref, m0_ref,
         o_ref, of_s, ob_s, st_s, m_s) = refs
    else:
        (q_ref, k_ref, v_ref, og_ref, gt_ref, gp_ref, ng_ref, bd_ref,
         o_ref, cf_ref, mf_ref, of_s, ob_s, st_s, m_s) = refs
    n_chunks = T // CHUNK
    masks = _tri_masks()
    tri = (masks[0][0].astype(BF16), masks[1][0].astype(BF16))
    eye = _eye(GATE_LANES)
    ones_col = (lax.broadcasted_iota(jnp.int32, (CHUNK, HEAD_DIM), 1) == 0).astype(F32)
    bias = gp_ref[0:1, :]

    if has_cache:
        st_s[...] = c0_ref[...]
        m_s[...] = m0_ref[...]
    else:
        st_s[...] = jnp.zeros(st_s.shape, F32)
        m_s[...] = jnp.zeros(m_s.shape, F32)

    def body(n, carry):
        rows = [_chunk_rows(d, n, n_chunks) for d in range(2)]
        qc = [q_ref[pl.ds(r0, CHUNK), :] * (HEAD_DIM ** -0.5) for r0 in rows]
        kc = [k_ref[pl.ds(r0, CHUNK), :] for r0 in rows]
        vc = [v_ref[pl.ds(r0, CHUNK), :] for r0 in rows]
        pre = [gt_ref[pl.ds(r0, CHUNK), :] + bias for r0 in rows]
        b = [_sel_mm(tri[d], _log_sigmoid(pre[d])) for d in range(2)]
        b_t = [_sel_mm(eye, x, NT) for x in b]
        ig_t = [_sel_mm(eye, x, NT) for x in pre]

        qh = [qc[d][:, _hs(h)] for d, h in UNITS]
        kh = [kc[d][:, _hs(h)] for d, h in UNITS]
        vaug = [jnp.concatenate([vc[d][:, _hs(h)], ones_col], axis=1) for d, h in UNITS]
        caug = [st_s[u] for u in UNITS]
        qk = [_mm3(q, k, NT) for q, k in zip(qh, kh)]
        qc_ = [_mm3(q, c) for q, c in zip(qh, caug)]

        s, inter_w, m_i, wk, dec, m_new = [], [], [], [], [], []
        for i, (d, h) in enumerate(UNITS):
            r = d * N_HEADS + h
            b_col, b_row = b[d][:, L_CF + r:L_CF + r + 1], b_t[d][L_CF + r:L_CF + r + 1, :]
            ig_col, ig_row = pre[d][:, L_CI + r:L_CI + r + 1], ig_t[d][L_CI + r:L_CI + r + 1, :]
            b_last = b_row[:, CHUNK - 1:CHUNK] if d == 0 else b_row[:, 0:1]
            m_prev = m_s[r:r + 1, 0:1]
            dm = jnp.where(masks[d][0], b_col - b_row + ig_row, NEG)
            inter = b_col + m_prev
            mi = jnp.maximum(inter, jnp.max(dm, axis=1, keepdims=True))
            s.append(qk[i] * jnp.exp(dm - mi))
            inter_w.append(jnp.exp(inter - mi))
            m_i.append(mi)
            mn = jnp.maximum(b_last + m_prev, jnp.max(b_last - b_row + ig_row, axis=1, keepdims=True))
            wk.append(kh[i] * jnp.exp(b_last - b_col + ig_col - mn))
            dec.append(jnp.exp(b_last + m_prev - mn))
            m_new.append(mn)

        sv = [_mm3(x, v) for x, v in zip(s, vaug)]
        kv = [_mm3(k, v, TN) for k, v in zip(wk, vaug)]
        outs = []
        for i, (d, h) in enumerate(UNITS):
            r = d * N_HEADS + h
            acc = inter_w[i] * qc_[i] + sv[i]
            den = acc[:, HEAD_DIM:HEAD_DIM + 1]
            outs.append(acc[:, :HEAD_DIM] / jnp.maximum(jnp.abs(den), jnp.exp(-m_i[i])))
            st_s[d, h] = dec[i] * caug[i] + kv[i]
            m_s[r:r + 1, :] = jnp.broadcast_to(m_new[i], (1, GATE_LANES))
        for d, o_s in enumerate((of_s, ob_s)):
            o_s[pl.ds(rows[d], CHUNK), :] = jnp.concatenate(outs[d * N_HEADS:(d + 1) * N_HEADS], axis=1)
        return carry

    lax.fori_loop(0, n_chunks, body, 0)
    if not has_cache:
        cf_ref[...] = st_s[...]
        mf_ref[...] = m_s[...]
    _epilogue(T, of_s, ob_s, og_ref, _sigmoid, bd_ref, ng_ref, o_ref)


def _mlstm(P, G, T, n_seq, gate_par, ng, bd, state):
    has_cache = state is not None
    st = (2, N_HEADS, HEAD_DIM, 2 * HEAD_DIM)
    ms = (2 * N_HEADS, GATE_LANES)
    in_specs = _seq_specs(T, (S_CQ, S_CK, S_CV, S_CO)) + [
        _gate_spec(T), _const_spec((8, GATE_LANES)), _const_spec((1, SEG)), _const_spec((SEG, SEG))]
    args = [P, P, P, P, G, gate_par, ng, bd]
    out_specs = [pl.BlockSpec((T, SEG), lambda s: (s, 0))]
    out_shape = [jax.ShapeDtypeStruct((n_seq * T, SEG), F32)]
    if has_cache:
        in_specs += [_lead_spec(st), _lead_spec(ms)]
        args += list(state)
    else:
        out_specs += [_lead_spec(st), _lead_spec(ms)]
        out_shape += [jax.ShapeDtypeStruct((n_seq,) + st, F32), jax.ShapeDtypeStruct((n_seq,) + ms, F32)]
    scratch = [pltpu.VMEM((T, SEG), F32), pltpu.VMEM((T, SEG), F32), pltpu.VMEM(st, F32), pltpu.VMEM(ms, F32)]
    return _mixer_call(functools.partial(_mlstm_kernel, T=T, has_cache=has_cache), "mlstm",
                       n_seq, in_specs, out_specs, out_shape, scratch, args)


def _delta_kernel(*refs, T, has_cache):
    if has_cache:
        (q_ref, k_ref, v_ref, z_ref, gt_ref, gp_ref, cw_ref, ng_ref, bd_ref, s0_ref,
         o_ref, of_s, ob_s, st_s, qs, ks, vs) = refs
    else:
        (q_ref, k_ref, v_ref, z_ref, gt_ref, gp_ref, cw_ref, ng_ref, bd_ref,
         o_ref, sf_ref, of_s, ob_s, st_s, qs, ks, vs) = refs
    n_chunks = T // CHUNK
    n_blk = T // ROW_BLOCK
    masks = _tri_masks()
    tri = (masks[0][0].astype(BF16), masks[1][0].astype(BF16))
    eye = _eye(GATE_LANES)
    bd = bd_ref[...]
    bias, a_log = gp_ref[0:1, :], gp_ref[1:2, :]
    row = lax.broadcasted_iota(jnp.int32, (ROW_BLOCK, 1), 0)

    def prologue(i, carry):
        r0 = pl.multiple_of(i * ROW_BLOCK, ROW_BLOCK)
        rp = pl.multiple_of(jnp.maximum(r0 - 8, 0), 8)
        rn = pl.multiple_of(jnp.minimum(r0 + ROW_BLOCK, T - 8), 8)
        for j, (src, dst) in enumerate(((q_ref, qs), (k_ref, ks), (v_ref, vs))):
            cur = src[pl.ds(r0, ROW_BLOCK), :]
            before = jnp.where(i > 0, src[pl.ds(rp, 8), :][7:8, :], 0.0)
            after = jnp.where(i < n_blk - 1, src[pl.ds(rn, 8), :][0:1, :], 0.0)
            down = jnp.where(row == 0, before, pltpu.roll(cur, 1, axis=0))
            up = jnp.where(row == ROW_BLOCK - 1, after, pltpu.roll(cur, ROW_BLOCK - 1, axis=0))
            w = cw_ref[:, j * SEG:(j + 1) * SEG]
            y = _silu(w[0:1, :] * down + w[1:2, :] * cur + w[2:3, :] * up)
            if j < 2:
                y = y * lax.rsqrt(_gsum(y * y, bd) + EPS)
            if j == 0:
                y = y * (HEAD_DIM ** -0.5)
            dst[pl.ds(r0, ROW_BLOCK), :] = y
        return carry

    lax.fori_loop(0, n_blk, prologue, 0)

    for d in range(2):
        for h in range(N_HEADS):
            st_s[d, h] = s0_ref[d, h] if has_cache else jnp.zeros((HEAD_DIM, HEAD_DIM), F32)

    def body(n, carry):
        rows = [_chunk_rows(d, n, n_chunks) for d in range(2)]
        qc = [qs[pl.ds(r0, CHUNK), :] for r0 in rows]
        kc = [ks[pl.ds(r0, CHUNK), :] for r0 in rows]
        vc = [vs[pl.ds(r0, CHUNK), :] for r0 in rows]
        pre = [gt_ref[pl.ds(r0, CHUNK), :] for r0 in rows]
        beta_all = [_sigmoid(x) for x in pre]
        g_all = [-jnp.exp(a_log) * _softplus(x + bias) for x in pre]
        cg = [_sel_mm(tri[d], g_all[d]) for d in range(2)]
        cg_t = [_sel_mm(eye, x, NT) for x in cg]

        qh = [qc[d][:, _hs(h)] for d, h in UNITS]
        kh = [kc[d][:, _hs(h)] for d, h in UNITS]
        kb, x, cg_col, g_last, decay = [], [], [], [], []
        for i, (d, h) in enumerate(UNITS):
            r = d * N_HEADS + h
            beta = beta_all[d][:, L_BETA + r:L_BETA + r + 1]
            col, row_ = cg[d][:, L_ALPHA + r:L_ALPHA + r + 1], cg_t[d][L_ALPHA + r:L_ALPHA + r + 1, :]
            cg_col.append(col)
            g_last.append(row_[:, CHUNK - 1:CHUNK] if d == 0 else row_[:, 0:1])
            decay.append(jnp.exp(jnp.where(masks[d][0], col - row_, NEG)))
            kb.append(kh[i] * beta)
            x.append(jnp.concatenate([vc[d][:, _hs(h)] * beta, kb[i] * jnp.exp(col)], axis=1))

        kk = [_mm3(a, b, NT) for a, b in zip(kb, kh)]
        qk = [_mm3(a, b, NT) for a, b in zip(qh, kh)]
        p = [jnp.where(masks[u[0]][1], g * dc, 0.0) for g, dc, u in zip(kk, decay, UNITS)]
        px = [_mm3(a, b) for a, b in zip(p, x)]
        x = [a - b for a, b in zip(x, px)]
        for _ in range(5):
            p = [_mm3(a, a) for a in p]
            px = [_mm3(a, b) for a, b in zip(p, x)]
            x = [a + b for a, b in zip(x, px)]

        s = [st_s[u] for u in UNITS]
        ws = [_mm3(a[:, HEAD_DIM:], b) for a, b in zip(x, s)]
        qs_ = [_mm3(q * jnp.exp(c), b) for q, c, b in zip(qh, cg_col, s)]
        v_new = [a[:, :HEAD_DIM] - b for a, b in zip(x, ws)]
        av = [_mm3(a * dc, v) for a, dc, v in zip(qk, decay, v_new)]
        kv = [_mm3(k * jnp.exp(gl - c), v, TN) for k, gl, c, v in zip(kh, g_last, cg_col, v_new)]
        for i, u in enumerate(UNITS):
            st_s[u] = s[i] * jnp.exp(g_last[i]) + kv[i]
        for d, o_s in enumerate((of_s, ob_s)):
            o_s[pl.ds(rows[d], CHUNK), :] = jnp.concatenate(
                [qs_[i] + av[i] for i, u in enumerate(UNITS) if u[0] == d], axis=1)
        return carry

    lax.fori_loop(0, n_chunks, body, 0)
    if not has_cache:
        sf_ref[...] = st_s[...]
    _epilogue(T, of_s, ob_s, z_ref, _silu, bd_ref, ng_ref, o_ref)


def _deltanet(P, G, T, n_seq, gate_par, conv_w, ng, bd, l, state):
    has_cache = state is not None
    st = (2, N_HEADS, HEAD_DIM, HEAD_DIM)
    in_specs = _seq_specs(T, (S_AQ, S_AK, S_AV, S_AZ)) + [
        _gate_spec(T), _const_spec((8, GATE_LANES)), _const_spec((3, 3 * SEG)),
        _const_spec((1, SEG)), _const_spec((SEG, SEG))]
    args = [P, P, P, P, G, gate_par, conv_w, ng, bd]
    out_specs = [pl.BlockSpec((T, SEG), lambda s: (s, 0))]
    out_shape = [jax.ShapeDtypeStruct((n_seq * T, SEG), F32)]
    if has_cache:
        in_specs.append(_state_in_spec(l, st))
        args.append(state)
    else:
        out_specs.append(_lead_spec(st))
        out_shape.append(jax.ShapeDtypeStruct((n_seq,) + st, F32))
    scratch = [pltpu.VMEM((T, SEG), F32)] * 2 + [pltpu.VMEM(st, F32)] + [pltpu.VMEM((T, SEG), F32)] * 3
    return _mixer_call(functools.partial(_delta_kernel, T=T, has_cache=has_cache), "deltanet",
                       n_seq, in_specs, out_specs, out_shape, scratch, args)


def _diff_kernel(*refs, T, has_cache, lam_init):
    if has_cache:
        (q_ref, k_ref, v_ref, qg_ref, kg_ref, lam_ref, ng_ref, bd32_ref, bd64_ref,
         cos_ref, sin_ref, ck_ref, cv_ref, o_ref, qs, ks) = refs
    else:
        (q_ref, k_ref, v_ref, qg_ref, kg_ref, lam_ref, ng_ref, bd32_ref, bd64_ref,
         o_ref, ko_ref, vo_ref, qs, ks) = refs
    n_blk = T // ROW_BLOCK
    bd32 = bd32_ref[...]
    lane = lax.broadcasted_iota(jnp.int32, (1, SEG), 1)
    first_half = (lane % 16) < 8

    def prologue(i, carry):
        r0 = pl.multiple_of(i * ROW_BLOCK, ROW_BLOCK)
        for src, g_ref, dst in ((q_ref, qg_ref, qs), (k_ref, kg_ref, ks)):
            x = src[pl.ds(r0, ROW_BLOCK), :]
            y = x * lax.rsqrt(_gsum(x * x, bd32) * (1.0 / DQK) + EPS) * g_ref[...]
            if has_cache:
                partner = jnp.where(first_half, pltpu.roll(y, SEG - 8, axis=1), pltpu.roll(y, 8, axis=1))
                y = y * cos_ref[pl.ds(r0, ROW_BLOCK), :] + partner * sin_ref[pl.ds(r0, ROW_BLOCK), :]
            dst[pl.ds(r0, ROW_BLOCK), :] = y
        return carry

    lax.fori_loop(0, n_blk, prologue, 0)

    if not has_cache:
        for h in range(N_HEADS):
            ko_ref[h] = ks[:, _hs(h)]
            vo_ref[h] = v_ref[:, _hs(h)]

    lp = lam_ref[...]
    lam = (jnp.exp(jnp.sum(lp[0:1, :] * lp[1:2, :], axis=1, keepdims=True))
           - jnp.exp(jnp.sum(lp[2:3, :] * lp[3:4, :], axis=1, keepdims=True)) + lam_init)
    scale = DQK ** -0.5
    comp1 = lax.broadcasted_iota(jnp.int32, (1, HEAD_DIM), 1) < DQK
    bd64 = bd64_ref[...]
    ng = ng_ref[...]

    def softmax_parts(scores):
        m = scores[0].max(axis=1, keepdims=True)
        for s in scores[1:]:
            m = jnp.maximum(m, s.max(axis=1, keepdims=True))
        es = [jnp.exp(s - m) for s in scores]
        tot = es[0].sum(axis=1, keepdims=True)
        for e in es[1:]:
            tot = tot + e.sum(axis=1, keepdims=True)
        inv = 1.0 / tot
        return [e * inv for e in es]

    def qblock(i, carry):
        r0 = pl.multiple_of(i * Q_BLOCK, Q_BLOCK)
        qb = qs[pl.ds(r0, Q_BLOCK), :]
        outs = []
        for h in range(N_HEADS):
            qh = qb[:, _hs(h)]
            q1 = jnp.where(comp1, qh, 0.0)
            q2 = qh - q1
            keys = [ks[:, _hs(h)]]
            vals = [v_ref[:, _hs(h)]]
            if has_cache:
                keys.insert(0, ck_ref[h])
                vals.insert(0, cv_ref[h])
            p1 = softmax_parts([_mm1(q1, kk, NT) * scale for kk in keys])
            p2 = softmax_parts([_mm1(q2, kk, NT) * scale for kk in keys])
            o = None
            for a, b, vv in zip(p1, p2, vals):
                t = _mm1(a - lam * b, vv)
                o = t if o is None else o + t
            outs.append(o)
        o = jnp.concatenate(outs, axis=1)
        o_ref[pl.ds(r0, Q_BLOCK), :] = _head_norm_gate(o, bd64, ng, 1.0 - lam_init)
        return carry

    lax.fori_loop(0, T // Q_BLOCK, qblock, 0)


def _diffattn(P, T, n_seq, qg, kg, lam_par, ng, bd32, bd64, l, rope, cache):
    has_cache = cache is not None
    lam_init = 0.8 - 0.6 * math.exp(-0.3 * l)
    kv = (N_HEADS, T, HEAD_DIM)
    in_specs = _seq_specs(T, (S_BQ, S_BK, S_BV)) + [
        _const_spec((1, SEG)), _const_spec((1, SEG)), _const_spec((4, DQK)), _const_spec((1, SEG)),
        _const_spec((SEG, SEG)), _const_spec((SEG, SEG))]
    args = [P, P, P, qg, kg, lam_par, ng, bd32, bd64]
    out_specs = [pl.BlockSpec((T, SEG), lambda s: (s, 0))]
    out_shape = [jax.ShapeDtypeStruct((n_seq * T, SEG), F32)]
    if has_cache:
        ckv = cache[0].shape[2:]
        in_specs += [_const_spec((T, SEG)), _const_spec((T, SEG)), _state_in_spec(l, ckv), _state_in_spec(l, ckv)]
        args += [rope[0], rope[1], cache[0], cache[1]]
    else:
        out_specs += [_lead_spec(kv), _lead_spec(kv)]
        out_shape += [jax.ShapeDtypeStruct((n_seq,) + kv, F32)] * 2
    scratch = [pltpu.VMEM((T, SEG), F32), pltpu.VMEM((T, SEG), F32)]
    return _mixer_call(functools.partial(_diff_kernel, T=T, has_cache=has_cache, lam_init=lam_init),
                       "diff_attn", n_seq, in_specs, out_specs, out_shape, scratch, args)


def _rope_tables(T):
    n_freq = DQK // 4
    t = jnp.arange(T)
    rows = (t // GRID_W).astype(F32)
    cols = (t % GRID_W).astype(F32)
    freqs = ROPE_BASE ** (-jnp.arange(n_freq, dtype=F32) / n_freq)
    ang_r, ang_c = rows[:, None] * freqs, cols[:, None] * freqs

    def comp(fn, sign):
        return jnp.concatenate([fn(ang_r), sign * fn(ang_r), fn(ang_c), sign * fn(ang_c)], axis=1)

    reps = SEG // DQK
    cos = jnp.tile(comp(jnp.cos, 1.0), (1, reps))
    sin = jnp.tile(jnp.concatenate([-jnp.sin(ang_r), jnp.sin(ang_r), -jnp.sin(ang_c), jnp.sin(ang_c)], axis=1),
                   (1, reps))
    return cos, sin


def _block_ones(group):
    i = np.arange(SEG)
    return jnp.asarray(i[:, None] // group == i[None, :] // group, BF16)


def kernel(x_prompt, x_sample, cache_diff_k, cache_diff_v, state_delta, state_mlstm_C, state_mlstm_n, state_mlstm_m, state_ret, c, c_ctx, w_ada, b_ada, norm_g, ffn_w_gate, ffn_w_up, ffn_w_down, w_in, dn_conv_w, dn_a_log, dn_dt_bias, dn_norm_g, da_qn_g, da_kn_g, da_lambda, da_norm_g, ml_i_bias, ml_f_bias, ml_norm_g, ret_decay_logit, ret_norm_g, w_branch, w_out):
    B, T, _ = x_prompt.shape
    Bs, Ts, _ = x_sample.shape
    cond8 = jnp.zeros((8, D_MODEL), F32).at[0].set(c_ctx).at[1:1 + Bs].set(c)
    mod = _ada(cond8, w_ada, b_ada).reshape(DEPTH, 8, N_MOD, D_MODEL)
    bd32, bd64 = _block_ones(DQK), _block_ones(HEAD_DIM)
    rope = _rope_tables(Ts)
    tile_heads = lambda g: jnp.tile(g, SEG // g.shape[0])[None]

    xs = {"ctx": x_prompt.reshape(B * T, D_MODEL), "smp": x_sample.reshape(Bs * Ts, D_MODEL)}
    geo = {"ctx": (T, B, _mod_spec(B * T, 0)), "smp": (Ts, Bs, _mod_spec(Ts, 1))}
    new = {k: [] for k in ("k", "v", "dn", "C", "m", "r")}
    for l in range(DEPTH):
        wg, wu, wd = ffn_w_gate[l].astype(BF16), ffn_w_up[l].astype(BF16), ffn_w_down[l].astype(BF16)
        cols = lambda f: w_in[l][:, _IN_OFFS[f]:_IN_OFFS[f + 1]]
        w_mix = jnp.concatenate([cols(f) for f in _SEG_FIELDS], axis=1).astype(BF16)
        w_gate = jnp.concatenate([cols(f) for f in _GATE_FIELDS]
                                 + [jnp.zeros((D_MODEL, GATE_LANES - 32), F32)], axis=1).astype(BF16)
        w_merge = cols(19).astype(BF16)
        wb, wo = w_branch[l].astype(BF16), w_out[l].astype(BF16)
        gate_par = jnp.zeros((8, GATE_LANES), F32)
        gate_par = gate_par.at[0, L_ALPHA:L_ALPHA + 8].set(dn_dt_bias[l].reshape(8))
        gate_par = gate_par.at[0, L_CI:L_CI + 8].set(ml_i_bias[l].reshape(8))
        gate_par = gate_par.at[0, L_CF:L_CF + 8].set(ml_f_bias[l].reshape(8))
        gate_par = gate_par.at[1, L_ALPHA:L_ALPHA + 8].set(dn_a_log[l].reshape(8))
        caug0 = jnp.concatenate([state_mlstm_C[:, l], state_mlstm_n[:, l][..., None],
                                 jnp.zeros((Bs, 2, N_HEADS, HEAD_DIM, HEAD_DIM - 1), F32)], axis=-1)
        m0 = jnp.broadcast_to(state_mlstm_m[:, l].reshape(Bs, 2 * N_HEADS, 1), (Bs, 2 * N_HEADS, GATE_LANES))
        for path in ("ctx", "smp"):
            Tp, n_seq, mspec = geo[path]
            smp = path == "smp"
            x = _ffn(xs[path], mod[l], mspec, norm_g[l, 0][None], wg[0], wu[0], wd[0], 0)
            P, G = _inproj(x, mod[l], mspec, norm_g[l, 1][None], w_mix, w_gate)
            ra = _deltanet(P, G, Tp, n_seq, gate_par, dn_conv_w[l], tile_heads(dn_norm_g[l]), bd64, l,
                           state_delta if smp else None)
            rb = _diffattn(P, Tp, n_seq, tile_heads(da_qn_g[l]), tile_heads(da_kn_g[l]), da_lambda[l],
                           tile_heads(da_norm_g[l]), bd32, bd64, l, rope if smp else None,
                           (cache_diff_k, cache_diff_v) if smp else None)
            rc = _mlstm(P, G, Tp, n_seq, gate_par, tile_heads(ml_norm_g[l]), bd64, (caug0, m0) if smp else None)
            rd = _retention(P, Tp, n_seq, ret_decay_logit[l], tile_heads(ret_norm_g[l]), bd64, l,
                            state_ret if smp else None)
            if not smp:
                new["dn"].append(ra[1])
                new["k"].append(rb[1])
                new["v"].append(rb[2])
                new["C"].append(rc[1])
                new["m"].append(rc[2])
                new["r"].append(rd[1])
            x = _merge(x, mod[l], mspec, norm_g[l, 1][None], (ra[0], rb[0], rc[0], rd[0]), w_merge, wb, wo)
            xs[path] = _ffn(x, mod[l], mspec, norm_g[l, 2][None], wg[1], wu[1], wd[1], 6)
    st = lambda k: jnp.stack(new[k], axis=1)
    caug = st("C")
    return (xs["ctx"].reshape(B, T, D_MODEL), xs["smp"].reshape(Bs, Ts, D_MODEL),
            st("k"), st("v"), st("dn"),
            caug[..., :HEAD_DIM], caug[..., HEAD_DIM], st("m")[..., 0].reshape(B, DEPTH, 2, N_HEADS),
            st("r"))
```

```python
import functools
import math

import numpy as np
import jax
import jax.numpy as jnp
from jax import lax
from jax.experimental import pallas as pl
from jax.experimental.pallas import tpu as pltpu

F32 = jnp.float32
BF16 = jnp.bfloat16

D_MODEL = 1024
FFN_DIM = 2816
N_MOD = 9
DEPTH = 2
N_HEADS = 4
HEAD_DIM = 64
SEG = N_HEADS * HEAD_DIM
N_SEG = 15
CHUNK = 64
DQK = 32
GRID_W = 64
ROPE_BASE = 10000.0
EPS = 1e-6
N_BRANCH = 4
GATE_LANES = 128
NEG = -1e30

TM = 256
Q_BLOCK = 256
ROW_BLOCK = 128
VMEM_LIMIT = 56 * 1024 * 1024

NN = (((1,), (0,)), ((), ()))
NT = (((1,), (1,)), ((), ()))
TN = (((0,), (0,)), ((), ()))

_IN_SIZES = (256, 256, 256, 256, 8, 8, 256, 256, 256, 256, 256, 256, 256, 8, 8, 256, 256, 256, 256, 4096)
_IN_OFFS = np.concatenate([[0], np.cumsum(_IN_SIZES)]).tolist()
_SEG_FIELDS = (0, 1, 2, 3, 6, 7, 8, 9, 10, 11, 12, 15, 16, 17, 18)
_GATE_FIELDS = (4, 5, 13, 14)
(S_AQ, S_AK, S_AV, S_AZ, S_BQ, S_BK, S_BV, S_CQ, S_CK, S_CV, S_CO, S_DQ, S_DK, S_DV, S_DG) = range(N_SEG)
L_BETA, L_ALPHA, L_CI, L_CF = 0, 8, 16, 24


def _dg(a, b, dims):
    return lax.dot_general(a, b, dims, preferred_element_type=F32)


def _split2(x):
    hi = x.astype(BF16)
    lo = (x - hi.astype(F32)).astype(BF16)
    return hi, lo


def _mm1(a, b, dims=NN):
    return _dg(a.astype(BF16), b.astype(BF16), dims)


def _mm3(a, b, dims=NN):
    ah, al = _split2(a)
    bh, bl = _split2(b)
    return _dg(ah, bh, dims) + (_dg(ah, bl, dims) + _dg(al, bh, dims))


def _mm2r(a, b, dims=NN):
    ah = a.astype(BF16)
    bh, bl = _split2(b)
    return _dg(ah, bh, dims) + _dg(ah, bl, dims)


def _sel_mm(sel, x, dims=NN):
    h0 = x.astype(BF16)
    r1 = x - h0.astype(F32)
    h1 = r1.astype(BF16)
    h2 = (r1 - h1.astype(F32)).astype(BF16)
    return _dg(sel, h0, dims) + (_dg(sel, h1, dims) + _dg(sel, h2, dims))


def _gsum(x, bd):
    hi, lo = _split2(x)
    return _dg(hi, bd, NN) + _dg(lo, bd, NN)


def _sigmoid(x):
    return 1.0 / (1.0 + jnp.exp(-x))


def _silu(x):
    return x * _sigmoid(x)


def _softplus(x):
    return jnp.maximum(x, 0.0) + jnp.log1p(jnp.exp(-jnp.abs(x)))


def _log_sigmoid(x):
    return -_softplus(-x)


def _norm_mod(x, g, shift, scale):
    ms = jnp.mean(x * x, axis=-1, keepdims=True)
    return (x * lax.rsqrt(ms + EPS) * g) * (1.0 + scale) + shift


def _tri_masks():
    ii = lax.broadcasted_iota(jnp.int32, (CHUNK, CHUNK), 0)
    jj = lax.broadcasted_iota(jnp.int32, (CHUNK, CHUNK), 1)
    return ((jj <= ii, jj < ii), (jj >= ii, jj > ii))


def _eye(n):
    ii = lax.broadcasted_iota(jnp.int32, (n, n), 0)
    jj = lax.broadcasted_iota(jnp.int32, (n, n), 1)
    return (ii == jj).astype(BF16)


def _cparams(n_grid=1):
    return pltpu.CompilerParams(dimension_semantics=("arbitrary",) * n_grid,
                                vmem_limit_bytes=VMEM_LIMIT)


def _const_spec(shape):
    nd = len(shape)
    return pl.BlockSpec(shape, lambda *_: (0,) * nd)


def _ada_kernel(s_ref, w_ref, b_ref, o_ref):
    s = s_ref[...]
    o_ref[...] = _mm3(_silu(s), w_ref[...]) + b_ref[...]


def _ada(cond8, w_ada, b_ada):
    tn = 1536
    n_t = (N_MOD * D_MODEL) // tn
    return pl.pallas_call(
        _ada_kernel,
        grid=(DEPTH, n_t),
        in_specs=[pl.BlockSpec((8, D_MODEL), lambda l, j: (0, 0)),
                  pl.BlockSpec((None, D_MODEL, tn), lambda l, j: (l, 0, j)),
                  pl.BlockSpec((None, 1, tn), lambda l, j: (l, 0, j))],
        out_specs=pl.BlockSpec((None, 8, tn), lambda l, j: (l, 0, j)),
        out_shape=jax.ShapeDtypeStruct((DEPTH, 8, N_MOD * D_MODEL), F32),
        compiler_params=_cparams(2),
        name="ada_mod",
    )(cond8, w_ada, b_ada.reshape(DEPTH, 1, N_MOD * D_MODEL))


def _ffn_kernel(x_ref, mod_ref, g_ref, wg_ref, wu_ref, wd_ref, o_ref, *, mi):
    x = x_ref[...]
    h = _norm_mod(x, g_ref[...], mod_ref[mi:mi + 1, :], mod_ref[mi + 1:mi + 2, :]).astype(BF16)
    gate = jnp.dot(h, wg_ref[...], preferred_element_type=F32)
    up = jnp.dot(h, wu_ref[...], preferred_element_type=F32)
    act = (_silu(gate) * up).astype(BF16)
    y = jnp.dot(act, wd_ref[...], preferred_element_type=F32)
    o_ref[...] = x + (0.5 * mod_ref[mi + 2:mi + 3, :]) * y


def _mod_spec(rows_per_cond, first_cond):
    per = rows_per_cond // TM
    return pl.BlockSpec((None, N_MOD, D_MODEL), lambda i: (first_cond + i // per, 0, 0))


def _ffn(x, mod, mod_spec, g, wg, wu, wd, mi):
    n = x.shape[0]
    return pl.pallas_call(
        functools.partial(_ffn_kernel, mi=mi),
        grid=(n // TM,),
        in_specs=[pl.BlockSpec((TM, D_MODEL), lambda i: (i, 0)),
                  mod_spec,
                  _const_spec((1, D_MODEL)),
                  _const_spec((D_MODEL, FFN_DIM)),
                  _const_spec((D_MODEL, FFN_DIM)),
                  _const_spec((FFN_DIM, D_MODEL))],
        out_specs=pl.BlockSpec((TM, D_MODEL), lambda i: (i, 0)),
        out_shape=jax.ShapeDtypeStruct((n, D_MODEL), F32),
        compiler_params=_cparams(1),
        name="ffn",
    )(x, mod, g, wg, wu, wd)


def _inproj_kernel(x_ref, mod_ref, g_ref, w_ref, wgt_ref, p_ref, gt_ref):
    h = _norm_mod(x_ref[...], g_ref[...], mod_ref[3:4, :], mod_ref[4:5, :]).astype(BF16)
    step = 3 * SEG
    for j in range(0, N_SEG * SEG, step):
        p_ref[:, j:j + step] = jnp.dot(h, w_ref[:, j:j + step], preferred_element_type=F32)
    gt_ref[...] = jnp.dot(h, wgt_ref[...], preferred_element_type=F32)


def _inproj(x, mod, mod_spec, g, w_mix, w_gate):
    n = x.shape[0]
    return pl.pallas_call(
        _inproj_kernel,
        grid=(n // TM,),
        in_specs=[pl.BlockSpec((TM, D_MODEL), lambda i: (i, 0)),
                  mod_spec,
                  _const_spec((1, D_MODEL)),
                  _const_spec((D_MODEL, N_SEG * SEG)),
                  _const_spec((D_MODEL, GATE_LANES))],
        out_specs=[pl.BlockSpec((TM, N_SEG * SEG), lambda i: (i, 0)),
                   pl.BlockSpec((TM, GATE_LANES), lambda i: (i, 0))],
        out_shape=[jax.ShapeDtypeStruct((n, N_SEG * SEG), F32),
                   jax.ShapeDtypeStruct((n, GATE_LANES), F32)],
        compiler_params=_cparams(1),
        name="in_proj",
    )(x, mod, g, w_mix, w_gate)


def _merge_kernel(x_ref, mod_ref, g_ref, ba_ref, bb_ref, bc_ref, bd_ref, wm_ref, wb_ref, wo_ref, o_ref):
    x = x_ref[...]
    h = _norm_mod(x, g_ref[...], mod_ref[3:4, :], mod_ref[4:5, :]).astype(BF16)
    mixed = None
    for m, b_ref in enumerate((ba_ref, bb_ref, bc_ref, bd_ref)):
        logits = jnp.dot(h, wm_ref[:, m * D_MODEL:(m + 1) * D_MODEL], preferred_element_type=F32)
        pb = jnp.dot(b_ref[...].astype(BF16), wb_ref[m], preferred_element_type=F32)
        term = _sigmoid(logits) * pb
        mixed = term if mixed is None else mixed + term
    y = jnp.dot(mixed.astype(BF16), wo_ref[...], preferred_element_type=F32)
    o_ref[...] = x + mod_ref[5:6, :] * y


def _merge(x, mod, mod_spec, g, branches, w_merge, w_branch, w_out):
    n = x.shape[0]
    row = lambda w: pl.BlockSpec((TM, w), lambda i: (i, 0))
    return pl.pallas_call(
        _merge_kernel,
        grid=(n // TM,),
        in_specs=[row(D_MODEL), mod_spec, _const_spec((1, D_MODEL)),
                  row(SEG), row(SEG), row(SEG), row(SEG),
                  _const_spec((D_MODEL, N_BRANCH * D_MODEL)),
                  _const_spec((N_BRANCH, SEG, D_MODEL)),
                  _const_spec((D_MODEL, D_MODEL))],
        out_specs=row(D_MODEL),
        out_shape=jax.ShapeDtypeStruct((n, D_MODEL), F32),
        compiler_params=_cparams(1),
        name="merge",
    )(x, mod, g, *branches, w_merge, w_branch, w_out)


def _seq_specs(T, segs):
    return [pl.BlockSpec((T, SEG), lambda s, j=j: (s, j)) for j in segs]


def _gate_spec(T):
    return pl.BlockSpec((T, GATE_LANES), lambda s: (s, 0))


def _state_in_spec(l, tail):
    nd = len(tail)
    return pl.BlockSpec((None, None) + tail, lambda s: (s, l) + (0,) * nd)


def _lead_spec(tail):
    nd = len(tail)
    return pl.BlockSpec((None,) + tail, lambda s: (s,) + (0,) * nd)


def _mixer_call(kernel, name, n_seq, in_specs, out_specs, out_shape, scratch, args):
    return pl.pallas_call(
        kernel, grid=(n_seq,), in_specs=in_specs, out_specs=out_specs, out_shape=out_shape,
        scratch_shapes=scratch, compiler_params=_cparams(1), name=name)(*args)


def _head_norm_gate(o, bd, ng, gate):
    ss = _gsum(o * o, bd)
    return o * lax.rsqrt(ss * (1.0 / HEAD_DIM) + EPS) * ng * gate


def _epilogue(T, of_s, ob_s, gate_ref, gate_fn, bd_ref, ng_ref, o_ref):
    rb = 256
    bd = bd_ref[...]
    ng = ng_ref[...]

    def blk(i, carry):
        r0 = pl.multiple_of(i * rb, rb)
        o = of_s[pl.ds(r0, rb), :] + ob_s[pl.ds(r0, rb), :]
        o_ref[pl.ds(r0, rb), :] = _head_norm_gate(o, bd, ng, gate_fn(gate_ref[pl.ds(r0, rb), :]))
        return carry

    lax.fori_loop(0, T // rb, blk, 0)


def _chunk_rows(d, n, n_chunks):
    c = n if d == 0 else n_chunks - 1 - n
    return pl.multiple_of(c * CHUNK, CHUNK)


def _hs(h):
    return slice(h * HEAD_DIM, (h + 1) * HEAD_DIM)


def _head_mask():
    ii = lax.broadcasted_iota(jnp.int32, (SEG, SEG), 0) // HEAD_DIM
    jj = lax.broadcasted_iota(jnp.int32, (SEG, SEG), 1) // HEAD_DIM
    return ii == jj


def _blocksum(x):
    c = CHUNK
    return (x[0:c] + x[c:2 * c]) + (x[2 * c:3 * c] + x[3 * c:4 * c])


UNITS = tuple((d, h) for d in range(2) for h in range(N_HEADS))


def _ret_kernel(*refs, T, has_cache):
    if has_cache:
        (q_ref, k_ref, v_ref, gt_ref, dl_ref, ng_ref, bd_ref, r0_ref, o_ref,
         of_s, ob_s, st_s, dec_s, qdec_s, kdec_s, cdec_s) = refs
    else:
        (q_ref, k_ref, v_ref, gt_ref, dl_ref, ng_ref, bd_ref, o_ref, rf_ref,
         of_s, ob_s, st_s, dec_s, qdec_s, kdec_s, cdec_s) = refs
    n_chunks = T // CHUNK
    masks = _tri_masks()
    hm = _head_mask()

    @pl.when(pl.program_id(0) == 0)
    def _():
        ii = lax.broadcasted_iota(jnp.int32, (CHUNK, CHUNK), 0)
        jj = lax.broadcasted_iota(jnp.int32, (CHUNK, CHUNK), 1)
        rel = (ii - jj).astype(F32)
        pos = lax.broadcasted_iota(jnp.int32, (CHUNK, 1), 0).astype(F32)
        lg_all = _log_sigmoid(dl_ref[...])
        for d in range(2):
            dec, qdec, kdec, cdec = [], [], [], []
            for h in range(N_HEADS):
                lg = lg_all[d:d + 1, h:h + 1]
                if d == 0:
                    e, qd, kd = rel * lg, (pos + 1.0) * lg, (CHUNK - 1.0 - pos) * lg
                else:
                    e, qd, kd = -rel * lg, (CHUNK - pos) * lg, pos * lg
                dec.append(jnp.exp(jnp.where(masks[d][0], e, NEG)))
                qdec.append(jnp.broadcast_to(jnp.exp(qd), (CHUNK, HEAD_DIM)))
                kdec.append(jnp.broadcast_to(jnp.exp(kd), (CHUNK, HEAD_DIM)))
                cdec.append(jnp.broadcast_to(jnp.exp(CHUNK * lg), (HEAD_DIM, SEG)))
            dec_s[d] = jnp.concatenate(dec, axis=0)
            qdec_s[d] = jnp.concatenate(qdec, axis=1)
            kdec_s[d] = jnp.concatenate(kdec, axis=1)
            cdec_s[d] = jnp.concatenate(cdec, axis=0)

    for d in range(2):
        st_s[d] = jnp.zeros((SEG, SEG), F32)
        if has_cache:
            for h in range(N_HEADS):
                st_s[d, _hs(h), _hs(h)] = r0_ref[d, h]

    def body(n, carry):
        rows = [_chunk_rows(d, n, n_chunks) for d in range(2)]
        q = [q_ref[pl.ds(r0, CHUNK), :] for r0 in rows]
        k = [k_ref[pl.ds(r0, CHUNK), :] * (HEAD_DIM ** -0.5) for r0 in rows]
        v = [v_ref[pl.ds(r0, CHUNK), :] for r0 in rows]
        q4 = [jnp.where(hm, jnp.concatenate([x] * N_HEADS, axis=0), 0.0) for x in q]
        qk = [_mm1(a, b, NT) for a, b in zip(q4, k)]
        qr = [_mm1(q[d] * qdec_s[d], st_s[d]) for d in range(2)]
        kv = [_mm1(k[d] * kdec_s[d], v[d], TN) for d in range(2)]
        av = [_mm1(qk[d] * dec_s[d], v[d]) for d in range(2)]
        for d, o_s in enumerate((of_s, ob_s)):
            st_s[d] = cdec_s[d] * st_s[d] + jnp.where(hm, kv[d], 0.0)
            o_s[pl.ds(rows[d], CHUNK), :] = qr[d] + _blocksum(jnp.where(hm, av[d], 0.0))
        return carry

    lax.fori_loop(0, n_chunks, body, 0)
    if not has_cache:
        for d in range(2):
            for h in range(N_HEADS):
                rf_ref[d, h] = st_s[d, _hs(h), _hs(h)]
    _epilogue(T, of_s, ob_s, gt_ref, _silu, bd_ref, ng_ref, o_ref)


def _retention(P, T, n_seq, decay_logit, ng, bd, l, state):
    has_cache = state is not None
    st = (2, N_HEADS, HEAD_DIM, HEAD_DIM)
    in_specs = _seq_specs(T, (S_DQ, S_DK, S_DV, S_DG)) + [
        _const_spec((2, N_HEADS)), _const_spec((1, SEG)), _const_spec((SEG, SEG))]
    args = [P, P, P, P, decay_logit, ng, bd]
    out_specs = [pl.BlockSpec((T, SEG), lambda s: (s, 0))]
    out_shape = [jax.ShapeDtypeStruct((n_seq * T, SEG), F32)]
    if has_cache:
        in_specs.append(_state_in_spec(l, st))
        args.append(state)
    else:
        out_specs.append(_lead_spec(st))
        out_shape.append(jax.ShapeDtypeStruct((n_seq,) + st, F32))
    scratch = [pltpu.VMEM((T, SEG), F32), pltpu.VMEM((T, SEG), F32), pltpu.VMEM((2, SEG, SEG), F32),
               pltpu.VMEM((2, SEG, HEAD_DIM), F32), pltpu.VMEM((2, CHUNK, SEG), F32),
               pltpu.VMEM((2, CHUNK, SEG), F32), pltpu.VMEM((2, SEG, SEG), F32)]
    return _mixer_call(functools.partial(_ret_kernel, T=T, has_cache=has_cache), "retention",
                       n_seq, in_specs, out_specs, out_shape, scratch, args)


def _mlstm_kernel(*refs, T, has_cache):
    if has_cache:
        (q_ref, k_ref, v_ref, og_ref, gt_ref, gp_ref, ng_ref, bd_ref, c0_ref, m0_ref,
         o_ref, of_s, ob_s, st_s, m_s) = refs
    else:
        (q_ref, k_ref, v_ref, og_ref, gt_ref, gp_ref, ng_ref, bd_ref,
         o_ref, cf_ref, mf_ref, of_s, ob_s, st_s, m_s) = refs
    n_chunks = T // CHUNK
    masks = _tri_masks()
    tri = (masks[0][0].astype(BF16), masks[1][0].astype(BF16))
    eye = _eye(GATE_LANES)
    ones_col = (lax.broadcasted_iota(jnp.int32, (CHUNK, HEAD_DIM), 1) == 0).astype(F32)
    bias = gp_ref[0:1, :]

    if has_cache:
        st_s[...] = c0_ref[...]
        m_s[...] = m0_ref[...]
    else:
        st_s[...] = jnp.zeros(st_s.shape, F32)
        m_s[...] = jnp.zeros(m_s.shape, F32)

    def body(n, carry):
        rows = [_chunk_rows(d, n, n_chunks) for d in range(2)]
        qc = [q_ref[pl.ds(r0, CHUNK), :] * (HEAD_DIM ** -0.5) for r0 in rows]
        kc = [k_ref[pl.ds(r0, CHUNK), :] for r0 in rows]
        vc = [v_ref[pl.ds(r0, CHUNK), :] for r0 in rows]
        pre = [gt_ref[pl.ds(r0, CHUNK), :] + bias for r0 in rows]
        b = [_sel_mm(tri[d], _log_sigmoid(pre[d])) for d in range(2)]
        b_t = [_sel_mm(eye, x, NT) for x in b]
        ig_t = [_sel_mm(eye, x, NT) for x in pre]

        qh = [qc[d][:, _hs(h)] for d, h in UNITS]
        kh = [kc[d][:, _hs(h)] for d, h in UNITS]
        vaug = [jnp.concatenate([vc[d][:, _hs(h)], ones_col], axis=1) for d, h in UNITS]
        caug = [st_s[u] for u in UNITS]
        qk = [_mm1(q, k, NT) for q, k in zip(qh, kh)]
        qc_ = [_mm1(q, c) for q, c in zip(qh, caug)]

        s, inter_w, m_i, wk, dec, m_new = [], [], [], [], [], []
        for i, (d, h) in enumerate(UNITS):
            r = d * N_HEADS + h
            b_col, b_row = b[d][:, L_CF + r:L_CF + r + 1], b_t[d][L_CF + r:L_CF + r + 1, :]
            ig_col, ig_row = pre[d][:, L_CI + r:L_CI + r + 1], ig_t[d][L_CI + r:L_CI + r + 1, :]
            b_last = b_row[:, CHUNK - 1:CHUNK] if d == 0 else b_row[:, 0:1]
            m_prev = m_s[r:r + 1, 0:1]
            dm = jnp.where(masks[d][0], b_col - b_row + ig_row, NEG)
            inter = b_col + m_prev
            mi = jnp.maximum(inter, jnp.max(dm, axis=1, keepdims=True))
            s.append(qk[i] * jnp.exp(dm - mi))
            inter_w.append(jnp.exp(inter - mi))
            m_i.append(mi)
            mn = jnp.maximum(b_last + m_prev, jnp.max(b_last - b_row + ig_row, axis=1, keepdims=True))
            wk.append(kh[i] * jnp.exp(b_last - b_col + ig_col - mn))
            dec.append(jnp.exp(b_last + m_prev - mn))
            m_new.append(mn)

        sv = [_mm1(x, v) for x, v in zip(s, vaug)]
        kv = [_mm1(k, v, TN) for k, v in zip(wk, vaug)]
        outs = []
        for i, (d, h) in enumerate(UNITS):
            r = d * N_HEADS + h
            acc = inter_w[i] * qc_[i] + sv[i]
            den = acc[:, HEAD_DIM:HEAD_DIM + 1]
            outs.append(acc[:, :HEAD_DIM] / jnp.maximum(jnp.abs(den), jnp.exp(-m_i[i])))
            st_s[d, h] = dec[i] * caug[i] + kv[i]
            m_s[r:r + 1, :] = jnp.broadcast_to(m_new[i], (1, GATE_LANES))
        for d, o_s in enumerate((of_s, ob_s)):
            o_s[pl.ds(rows[d], CHUNK), :] = jnp.concatenate(outs[d * N_HEADS:(d + 1) * N_HEADS], axis=1)
        return carry

    lax.fori_loop(0, n_chunks, body, 0)
    if not has_cache:
        cf_ref[...] = st_s[...]
        mf_ref[...] = m_s[...]
    _epilogue(T, of_s, ob_s, og_ref, _sigmoid, bd_ref, ng_ref, o_ref)


def _mlstm(P, G, T, n_seq, gate_par, ng, bd, state):
    has_cache = state is not None
    st = (2, N_HEADS, HEAD_DIM, 2 * HEAD_DIM)
    ms = (2 * N_HEADS, GATE_LANES)
    in_specs = _seq_specs(T, (S_CQ, S_CK, S_CV, S_CO)) + [
        _gate_spec(T), _const_spec((8, GATE_LANES)), _const_spec((1, SEG)), _const_spec((SEG, SEG))]
    args = [P, P, P, P, G, gate_par, ng, bd]
    out_specs = [pl.BlockSpec((T, SEG), lambda s: (s, 0))]
    out_shape = [jax.ShapeDtypeStruct((n_seq * T, SEG), F32)]
    if has_cache:
        in_specs += [_lead_spec(st), _lead_spec(ms)]
        args += list(state)
    else:
        out_specs += [_lead_spec(st), _lead_spec(ms)]
        out_shape += [jax.ShapeDtypeStruct((n_seq,) + st, F32), jax.ShapeDtypeStruct((n_seq,) + ms, F32)]
    scratch = [pltpu.VMEM((T, SEG), F32), pltpu.VMEM((T, SEG), F32), pltpu.VMEM(st, F32), pltpu.VMEM(ms, F32)]
    return _mixer_call(functools.partial(_mlstm_kernel, T=T, has_cache=has_cache), "mlstm",
                       n_seq, in_specs, out_specs, out_shape, scratch, args)


def _delta_kernel(*refs, T, has_cache):
    if has_cache:
        (q_ref, k_ref, v_ref, z_ref, gt_ref, gp_ref, cw_ref, ng_ref, bd_ref, s0_ref,
         o_ref, of_s, ob_s, st_s, qs, ks, vs) = refs
    else:
        (q_ref, k_ref, v_ref, z_ref, gt_ref, gp_ref, cw_ref, ng_ref, bd_ref,
         o_ref, sf_ref, of_s, ob_s, st_s, qs, ks, vs) = refs
    n_chunks = T // CHUNK
    n_blk = T // ROW_BLOCK
    masks = _tri_masks()
    tri = (masks[0][0].astype(BF16), masks[1][0].astype(BF16))
    eye = _eye(GATE_LANES)
    bd = bd_ref[...]
    bias, a_log = gp_ref[0:1, :], gp_ref[1:2, :]
    row = lax.broadcasted_iota(jnp.int32, (ROW_BLOCK, 1), 0)

    def prologue(i, carry):
        r0 = pl.multiple_of(i * ROW_BLOCK, ROW_BLOCK)
        rp = pl.multiple_of(jnp.maximum(r0 - 8, 0), 8)
        rn = pl.multiple_of(jnp.minimum(r0 + ROW_BLOCK, T - 8), 8)
        for j, (src, dst) in enumerate(((q_ref, qs), (k_ref, ks), (v_ref, vs))):
            cur = src[pl.ds(r0, ROW_BLOCK), :]
            before = jnp.where(i > 0, src[pl.ds(rp, 8), :][7:8, :], 0.0)
            after = jnp.where(i < n_blk - 1, src[pl.ds(rn, 8), :][0:1, :], 0.0)
            down = jnp.where(row == 0, before, pltpu.roll(cur, 1, axis=0))
            up = jnp.where(row == ROW_BLOCK - 1, after, pltpu.roll(cur, ROW_BLOCK - 1, axis=0))
            w = cw_ref[:, j * SEG:(j + 1) * SEG]
            y = _silu(w[0:1, :] * down + w[1:2, :] * cur + w[2:3, :] * up)
            if j < 2:
                y = y * lax.rsqrt(_gsum(y * y, bd) + EPS)
            if j == 0:
                y = y * (HEAD_DIM ** -0.5)
            dst[pl.ds(r0, ROW_BLOCK), :] = y
        return carry

    lax.fori_loop(0, n_blk, prologue, 0)

    for d in range(2):
        for h in range(N_HEADS):
            st_s[d, h] = s0_ref[d, h] if has_cache else jnp.zeros((HEAD_DIM, HEAD_DIM), F32)

    def body(n, carry):
        rows = [_chunk_rows(d, n, n_chunks) for d in range(2)]
        qc = [qs[pl.ds(r0, CHUNK), :] for r0 in rows]
        kc = [ks[pl.ds(r0, CHUNK), :] for r0 in rows]
        vc = [vs[pl.ds(r0, CHUNK), :] for r0 in rows]
        pre = [gt_ref[pl.ds(r0, CHUNK), :] for r0 in rows]
        beta_all = [_sigmoid(x) for x in pre]
        g_all = [-jnp.exp(a_log) * _softplus(x + bias) for x in pre]
        cg = [_sel_mm(tri[d], g_all[d]) for d in range(2)]
        cg_t = [_sel_mm(eye, x, NT) for x in cg]

        qh = [qc[d][:, _hs(h)] for d, h in UNITS]
        kh = [kc[d][:, _hs(h)] for d, h in UNITS]
        kb, x, cg_col, g_last, decay = [], [], [], [], []
        for i, (d, h) in enumerate(UNITS):
            r = d * N_HEADS + h
            beta = beta_all[d][:, L_BETA + r:L_BETA + r + 1]
            col, row_ = cg[d][:, L_ALPHA + r:L_ALPHA + r + 1], cg_t[d][L_ALPHA + r:L_ALPHA + r + 1, :]
            cg_col.append(col)
            g_last.append(row_[:, CHUNK - 1:CHUNK] if d == 0 else row_[:, 0:1])
            decay.append(jnp.exp(jnp.where(masks[d][0], col - row_, NEG)))
            kb.append(kh[i] * beta)
            x.append(jnp.concatenate([vc[d][:, _hs(h)] * beta, kb[i] * jnp.exp(col)], axis=1))

        kk = [_mm1(a, b, NT) for a, b in zip(kb, kh)]
        qk = [_mm1(a, b, NT) for a, b in zip(qh, kh)]
        p = [jnp.where(masks[u[0]][1], g * dc, 0.0) for g, dc, u in zip(kk, decay, UNITS)]
        px = [_mm2r(a, b) for a, b in zip(p, x)]
        x = [a - b for a, b in zip(x, px)]
        for _ in range(5):
            p = [_mm1(a, a) for a in p]
            px = [_mm2r(a, b) for a, b in zip(p, x)]
            x = [a + b for a, b in zip(x, px)]

        s = [st_s[u] for u in UNITS]
        ws = [_mm1(a[:, HEAD_DIM:], b) for a, b in zip(x, s)]
        qs_ = [_mm1(q * jnp.exp(c), b) for q, c, b in zip(qh, cg_col, s)]
        v_new = [a[:, :HEAD_DIM] - b for a, b in zip(x, ws)]
        av = [_mm1(a * dc, v) for a, dc, v in zip(qk, decay, v_new)]
        kv = [_mm1(k * jnp.exp(gl - c), v, TN) for k, gl, c, v in zip(kh, g_last, cg_col, v_new)]
        for i, u in enumerate(UNITS):
            st_s[u] = s[i] * jnp.exp(g_last[i]) + kv[i]
        for d, o_s in enumerate((of_s, ob_s)):
            o_s[pl.ds(rows[d], CHUNK), :] = jnp.concatenate(
                [qs_[i] + av[i] for i, u in enumerate(UNITS) if u[0] == d], axis=1)
        return carry

    lax.fori_loop(0, n_chunks, body, 0)
    if not has_cache:
        sf_ref[...] = st_s[...]
    _epilogue(T, of_s, ob_s, z_ref, _silu, bd_ref, ng_ref, o_ref)


def _deltanet(P, G, T, n_seq, gate_par, conv_w, ng, bd, l, state):
    has_cache = state is not None
    st = (2, N_HEADS, HEAD_DIM, HEAD_DIM)
    in_specs = _seq_specs(T, (S_AQ, S_AK, S_AV, S_AZ)) + [
        _gate_spec(T), _const_spec((8, GATE_LANES)), _const_spec((3, 3 * SEG)),
        _const_spec((1, SEG)), _const_spec((SEG, SEG))]
    args = [P, P, P, P, G, gate_par, conv_w, ng, bd]
    out_specs = [pl.BlockSpec((T, SEG), lambda s: (s, 0))]
    out_shape = [jax.ShapeDtypeStruct((n_seq * T, SEG), F32)]
    if has_cache:
        in_specs.append(_state_in_spec(l, st))
        args.append(state)
    else:
        out_specs.append(_lead_spec(st))
        out_shape.append(jax.ShapeDtypeStruct((n_seq,) + st, F32))
    scratch = [pltpu.VMEM((T, SEG), F32)] * 2 + [pltpu.VMEM(st, F32)] + [pltpu.VMEM((T, SEG), F32)] * 3
    return _mixer_call(functools.partial(_delta_kernel, T=T, has_cache=has_cache), "deltanet",
                       n_seq, in_specs, out_specs, out_shape, scratch, args)


def _diff_kernel(*refs, T, has_cache, lam_init):
    if has_cache:
        (q_ref, k_ref, v_ref, qg_ref, kg_ref, lam_ref, ng_ref, bd32_ref, bd64_ref,
         cos_ref, sin_ref, ck_ref, cv_ref, o_ref, qs, ks) = refs
    else:
        (q_ref, k_ref, v_ref, qg_ref, kg_ref, lam_ref, ng_ref, bd32_ref, bd64_ref,
         o_ref, ko_ref, vo_ref, qs, ks) = refs
    n_blk = T // ROW_BLOCK
    bd32 = bd32_ref[...]
    lane = lax.broadcasted_iota(jnp.int32, (1, SEG), 1)
    first_half = (lane % 16) < 8

    def prologue(i, carry):
        r0 = pl.multiple_of(i * ROW_BLOCK, ROW_BLOCK)
        for src, g_ref, dst in ((q_ref, qg_ref, qs), (k_ref, kg_ref, ks)):
            x = src[pl.ds(r0, ROW_BLOCK), :]
            y = x * lax.rsqrt(_gsum(x * x, bd32) * (1.0 / DQK) + EPS) * g_ref[...]
            if has_cache:
                partner = jnp.where(first_half, pltpu.roll(y, SEG - 8, axis=1), pltpu.roll(y, 8, axis=1))
                y = y * cos_ref[pl.ds(r0, ROW_BLOCK), :] + partner * sin_ref[pl.ds(r0, ROW_BLOCK), :]
            dst[pl.ds(r0, ROW_BLOCK), :] = y
        return carry

    lax.fori_loop(0, n_blk, prologue, 0)

    if not has_cache:
        for h in range(N_HEADS):
            ko_ref[h] = ks[:, _hs(h)]
            vo_ref[h] = v_ref[:, _hs(h)]

    lp = lam_ref[...]
    lam = (jnp.exp(jnp.sum(lp[0:1, :] * lp[1:2, :], axis=1, keepdims=True))
           - jnp.exp(jnp.sum(lp[2:3, :] * lp[3:4, :], axis=1, keepdims=True)) + lam_init)
    scale = DQK ** -0.5
    comp1 = lax.broadcasted_iota(jnp.int32, (1, HEAD_DIM), 1) < DQK
    bd64 = bd64_ref[...]
    ng = ng_ref[...]

    def softmax_parts(scores):
        m = scores[0].max(axis=1, keepdims=True)
        for s in scores[1:]:
            m = jnp.maximum(m, s.max(axis=1, keepdims=True))
        es = [jnp.exp(s - m) for s in scores]
        tot = es[0].sum(axis=1, keepdims=True)
        for e in es[1:]:
            tot = tot + e.sum(axis=1, keepdims=True)
        inv = 1.0 / tot
        return [e * inv for e in es]

    def qblock(i, carry):
        r0 = pl.multiple_of(i * Q_BLOCK, Q_BLOCK)
        qb = qs[pl.ds(r0, Q_BLOCK), :]
        outs = []
        for h in range(N_HEADS):
            qh = qb[:, _hs(h)]
            q1 = jnp.where(comp1, qh, 0.0)
            q2 = qh - q1
            keys = [ks[:, _hs(h)]]
            vals = [v_ref[:, _hs(h)]]
            if has_cache:
                keys.insert(0, ck_ref[h])
                vals.insert(0, cv_ref[h])
            p1 = softmax_parts([_mm1(q1, kk, NT) * scale for kk in keys])
            p2 = softmax_parts([_mm1(q2, kk, NT) * scale for kk in keys])
            o = None
            for a, b, vv in zip(p1, p2, vals):
                t = _mm1(a - lam * b, vv)
                o = t if o is None else o + t
            outs.append(o)
        o = jnp.concatenate(outs, axis=1)
        o_ref[pl.ds(r0, Q_BLOCK), :] = _head_norm_gate(o, bd64, ng, 1.0 - lam_init)
        return carry

    lax.fori_loop(0, T // Q_BLOCK, qblock, 0)


def _diffattn(P, T, n_seq, qg, kg, lam_par, ng, bd32, bd64, l, rope, cache):
    has_cache = cache is not None
    lam_init = 0.8 - 0.6 * math.exp(-0.3 * l)
    kv = (N_HEADS, T, HEAD_DIM)
    in_specs = _seq_specs(T, (S_BQ, S_BK, S_BV)) + [
        _const_spec((1, SEG)), _const_spec((1, SEG)), _const_spec((4, DQK)), _const_spec((1, SEG)),
        _const_spec((SEG, SEG)), _const_spec((SEG, SEG))]
    args = [P, P, P, qg, kg, lam_par, ng, bd32, bd64]
    out_specs = [pl.BlockSpec((T, SEG), lambda s: (s, 0))]
    out_shape = [jax.ShapeDtypeStruct((n_seq * T, SEG), F32)]
    if has_cache:
        ckv = cache[0].shape[2:]
        in_specs += [_const_spec((T, SEG)), _const_spec((T, SEG)), _state_in_spec(l, ckv), _state_in_spec(l, ckv)]
        args += [rope[0], rope[1], cache[0], cache[1]]
    else:
        out_specs += [_lead_spec(kv), _lead_spec(kv)]
        out_shape += [jax.ShapeDtypeStruct((n_seq,) + kv, F32)] * 2
    scratch = [pltpu.VMEM((T, SEG), F32), pltpu.VMEM((T, SEG), F32)]
    return _mixer_call(functools.partial(_diff_kernel, T=T, has_cache=has_cache, lam_init=lam_init),
                       "diff_attn", n_seq, in_specs, out_specs, out_shape, scratch, args)


def _rope_tables(T):
    n_freq = DQK // 4
    t = jnp.arange(T)
    rows = (t // GRID_W).astype(F32)
    cols = (t % GRID_W).astype(F32)
    freqs = ROPE_BASE ** (-jnp.arange(n_freq, dtype=F32) / n_freq)
    ang_r, ang_c = rows[:, None] * freqs, cols[:, None] * freqs

    def comp(fn, sign):
        return jnp.concatenate([fn(ang_r), sign * fn(ang_r), fn(ang_c), sign * fn(ang_c)], axis=1)

    reps = SEG // DQK
    cos = jnp.tile(comp(jnp.cos, 1.0), (1, reps))
    sin = jnp.tile(jnp.concatenate([-jnp.sin(ang_r), jnp.sin(ang_r), -jnp.sin(ang_c), jnp.sin(ang_c)], axis=1),
                   (1, reps))
    return cos, sin


def _block_ones(group):
    i = np.arange(SEG)
    return jnp.asarray(i[:, None] // group == i[None, :] // group, BF16)


def kernel(x_prompt, x_sample, cache_diff_k, cache_diff_v, state_delta, state_mlstm_C, state_mlstm_n, state_mlstm_m, state_ret, c, c_ctx, w_ada, b_ada, norm_g, ffn_w_gate, ffn_w_up, ffn_w_down, w_in, dn_conv_w, dn_a_log, dn_dt_bias, dn_norm_g, da_qn_g, da_kn_g, da_lambda, da_norm_g, ml_i_bias, ml_f_bias, ml_norm_g, ret_decay_logit, ret_norm_g, w_branch, w_out):
    B, T, _ = x_prompt.shape
    Bs, Ts, _ = x_sample.shape
    cond8 = jnp.zeros((8, D_MODEL), F32).at[0].set(c_ctx).at[1:1 + Bs].set(c)
    mod = _ada(cond8, w_ada, b_ada).reshape(DEPTH, 8, N_MOD, D_MODEL)
    bd32, bd64 = _block_ones(DQK), _block_ones(HEAD_DIM)
    rope = _rope_tables(Ts)
    tile_heads = lambda g: jnp.tile(g, SEG // g.shape[0])[None]

    xs = {"ctx": x_prompt.reshape(B * T, D_MODEL), "smp": x_sample.reshape(Bs * Ts, D_MODEL)}
    geo = {"ctx": (T, B, _mod_spec(B * T, 0)), "smp": (Ts, Bs, _mod_spec(Ts, 1))}
    new = {k: [] for k in ("k", "v", "dn", "C", "m", "r")}
    for l in range(DEPTH):
        wg, wu, wd = ffn_w_gate[l].astype(BF16), ffn_w_up[l].astype(BF16), ffn_w_down[l].astype(BF16)
        cols = lambda f: w_in[l][:, _IN_OFFS[f]:_IN_OFFS[f + 1]]
        w_mix = jnp.concatenate([cols(f) for f in _SEG_FIELDS], axis=1).astype(BF16)
        w_gate = jnp.concatenate([cols(f) for f in _GATE_FIELDS]
                                 + [jnp.zeros((D_MODEL, GATE_LANES - 32), F32)], axis=1).astype(BF16)
        w_merge = cols(19).astype(BF16)
        wb, wo = w_branch[l].astype(BF16), w_out[l].astype(BF16)
        gate_par = jnp.zeros((8, GATE_LANES), F32)
        gate_par = gate_par.at[0, L_ALPHA:L_ALPHA + 8].set(dn_dt_bias[l].reshape(8))
        gate_par = gate_par.at[0, L_CI:L_CI + 8].set(ml_i_bias[l].reshape(8))
        gate_par = gate_par.at[0, L_CF:L_CF + 8].set(ml_f_bias[l].reshape(8))
        gate_par = gate_par.at[1, L_ALPHA:L_ALPHA + 8].set(dn_a_log[l].reshape(8))
        caug0 = jnp.concatenate([state_mlstm_C[:, l], state_mlstm_n[:, l][..., None],
                                 jnp.zeros((Bs, 2, N_HEADS, HEAD_DIM, HEAD_DIM - 1), F32)], axis=-1)
        m0 = jnp.broadcast_to(state_mlstm_m[:, l].reshape(Bs, 2 * N_HEADS, 1), (Bs, 2 * N_HEADS, GATE_LANES))
        for path in ("ctx", "smp"):
            Tp, n_seq, mspec = geo[path]
            smp = path == "smp"
            x = _ffn(xs[path], mod[l], mspec, norm_g[l, 0][None], wg[0], wu[0], wd[0], 0)
            P, G = _inproj(x, mod[l], mspec, norm_g[l, 1][None], w_mix, w_gate)
            ra = _deltanet(P, G, Tp, n_seq, gate_par, dn_conv_w[l], tile_heads(dn_norm_g[l]), bd64, l,
                           state_delta if smp else None)
            rb = _diffattn(P, Tp, n_seq, tile_heads(da_qn_g[l]), tile_heads(da_kn_g[l]), da_lambda[l],
                           tile_heads(da_norm_g[l]), bd32, bd64, l, rope if smp else None,
                           (cache_diff_k, cache_diff_v) if smp else None)
            rc = _mlstm(P, G, Tp, n_seq, gate_par, tile_heads(ml_norm_g[l]), bd64, (caug0, m0) if smp else None)
            rd = _retention(P, Tp, n_seq, ret_decay_logit[l], tile_heads(ret_norm_g[l]), bd64, l,
                            state_ret if smp else None)
            if not smp:
                new["dn"].append(ra[1])
                new["k"].append(rb[1])
                new["v"].append(rb[2])
                new["C"].append(rc[1])
                new["m"].append(rc[2])
                new["r"].append(rd[1])
            x = _merge(x, mod[l], mspec, norm_g[l, 1][None], (ra[0], rb[0], rc[0], rd[0]), w_merge, wb, wo)
            xs[path] = _ffn(x, mod[l], mspec, norm_g[l, 2][None], wg[1], wu[1], wd[1], 6)
    st = lambda k: jnp.stack(new[k], axis=1)
    caug = st("C")
    return (xs["ctx"].reshape(B, T, D_MODEL), xs["smp"].reshape(Bs, Ts, D_MODEL),
            st("k"), st("v"), st("dn"),
            caug[..., :HEAD_DIM], caug[..., HEAD_DIM], st("m")[..., 0].reshape(B, DEPTH, 2, N_HEADS),
            st("r"))
```

```python
import functools
import math

import numpy as np
import jax
import jax.numpy as jnp
from jax import lax
from jax.experimental import pallas as pl
from jax.experimental.pallas import tpu as pltpu

F32 = jnp.float32
BF16 = jnp.bfloat16

D_MODEL = 1024
FFN_DIM = 2816
N_MOD = 9
DEPTH = 2
N_HEADS = 4
HEAD_DIM = 64
SEG = N_HEADS * HEAD_DIM
N_SEG = 15
CHUNK = 64
DQK = 32
GRID_W = 64
ROPE_BASE = 10000.0
EPS = 1e-6
N_BRANCH = 4
GATE_LANES = 128
NEG = -1e30

TM = 256
Q_BLOCK = 256
ROW_BLOCK = 128
VMEM_LIMIT = 56 * 1024 * 1024

NN = (((1,), (0,)), ((), ()))
NT = (((1,), (1,)), ((), ()))
TN = (((0,), (0,)), ((), ()))

_IN_SIZES = (256, 256, 256, 256, 8, 8, 256, 256, 256, 256, 256, 256, 256, 8, 8, 256, 256, 256, 256, 4096)
_IN_OFFS = np.concatenate([[0], np.cumsum(_IN_SIZES)]).tolist()
_SEG_FIELDS = (0, 1, 2, 3, 6, 7, 8, 9, 10, 11, 12, 15, 16, 17, 18)
_GATE_FIELDS = (4, 5, 13, 14)
(S_AQ, S_AK, S_AV, S_AZ, S_BQ, S_BK, S_BV, S_CQ, S_CK, S_CV, S_CO, S_DQ, S_DK, S_DV, S_DG) = range(N_SEG)
L_BETA, L_ALPHA, L_CI, L_CF = 0, 8, 16, 24


def _dg(a, b, dims):
    return lax.dot_general(a, b, dims, preferred_element_type=F32)


def _split2(x):
    hi = x.astype(BF16)
    lo = (x - hi.astype(F32)).astype(BF16)
    return hi, lo


def _mm1(a, b, dims=NN):
    return _dg(a.astype(BF16), b.astype(BF16), dims)


def _mm3(a, b, dims=NN):
    ah, al = _split2(a)
    bh, bl = _split2(b)
    return _dg(ah, bh, dims) + (_dg(ah, bl, dims) + _dg(al, bh, dims))


def _mm2r(a, b, dims=NN):
    ah = a.astype(BF16)
    bh, bl = _split2(b)
    return _dg(ah, bh, dims) + _dg(ah, bl, dims)


def _sel_mm(sel, x, dims=NN):
    h0 = x.astype(BF16)
    r1 = x - h0.astype(F32)
    h1 = r1.astype(BF16)
    h2 = (r1 - h1.astype(F32)).astype(BF16)
    return _dg(sel, h0, dims) + (_dg(sel, h1, dims) + _dg(sel, h2, dims))


def _gsum(x, bd):
    hi, lo = _split2(x)
    return _dg(hi, bd, NN) + _dg(lo, bd, NN)


def _sigmoid(x):
    return 1.0 / (1.0 + jnp.exp(-x))


def _silu(x):
    return x * _sigmoid(x)


def _softplus(x):
    return jnp.maximum(x, 0.0) + jnp.log1p(jnp.exp(-jnp.abs(x)))


def _log_sigmoid(x):
    return -_softplus(-x)


def _norm_mod(x, g, shift, scale):
    ms = jnp.mean(x * x, axis=-1, keepdims=True)
    return (x * lax.rsqrt(ms + EPS) * g) * (1.0 + scale) + shift


def _tri_masks():
    ii = lax.broadcasted_iota(jnp.int32, (CHUNK, CHUNK), 0)
    jj = lax.broadcasted_iota(jnp.int32, (CHUNK, CHUNK), 1)
    return ((jj <= ii, jj < ii), (jj >= ii, jj > ii))


def _cparams(n_grid=1):
    return pltpu.CompilerParams(dimension_semantics=("arbitrary",) * n_grid,
                                vmem_limit_bytes=VMEM_LIMIT)


def _const_spec(shape):
    nd = len(shape)
    return pl.BlockSpec(shape, lambda *_: (0,) * nd)


def _ada_kernel(s_ref, w_ref, b_ref, o_ref):
    s = s_ref[...]
    o_ref[...] = _mm3(_silu(s), w_ref[...]) + b_ref[...]


def _ada(cond8, w_ada, b_ada):
    tn = 1536
    n_t = (N_MOD * D_MODEL) // tn
    return pl.pallas_call(
        _ada_kernel,
        grid=(DEPTH, n_t),
        in_specs=[pl.BlockSpec((8, D_MODEL), lambda l, j: (0, 0)),
                  pl.BlockSpec((None, D_MODEL, tn), lambda l, j: (l, 0, j)),
                  pl.BlockSpec((None, 1, tn), lambda l, j: (l, 0, j))],
        out_specs=pl.BlockSpec((None, 8, tn), lambda l, j: (l, 0, j)),
        out_shape=jax.ShapeDtypeStruct((DEPTH, 8, N_MOD * D_MODEL), F32),
        compiler_params=_cparams(2),
        name="ada_mod",
    )(cond8, w_ada, b_ada.reshape(DEPTH, 1, N_MOD * D_MODEL))


def _ffn_kernel(x_ref, mod_ref, g_ref, wg_ref, wu_ref, wd_ref, o_ref, *, mi):
    x = x_ref[...]
    h = _norm_mod(x, g_ref[...], mod_ref[mi:mi + 1, :], mod_ref[mi + 1:mi + 2, :]).astype(BF16)
    gate = jnp.dot(h, wg_ref[...], preferred_element_type=F32)
    up = jnp.dot(h, wu_ref[...], preferred_element_type=F32)
    act = (_silu(gate) * up).astype(BF16)
    y = jnp.dot(act, wd_ref[...], preferred_element_type=F32)
    o_ref[...] = x + (0.5 * mod_ref[mi + 2:mi + 3, :]) * y


def _mod_spec(rows_per_cond, first_cond):
    per = rows_per_cond // TM
    return pl.BlockSpec((None, N_MOD, D_MODEL), lambda i: (first_cond + i // per, 0, 0))


def _ffn(x, mod, mod_spec, g, wg, wu, wd, mi):
    n = x.shape[0]
    return pl.pallas_call(
        functools.partial(_ffn_kernel, mi=mi),
        grid=(n // TM,),
        in_specs=[pl.BlockSpec((TM, D_MODEL), lambda i: (i, 0)),
                  mod_spec,
                  _const_spec((1, D_MODEL)),
                  _const_spec((D_MODEL, FFN_DIM)),
                  _const_spec((D_MODEL, FFN_DIM)),
                  _const_spec((FFN_DIM, D_MODEL))],
        out_specs=pl.BlockSpec((TM, D_MODEL), lambda i: (i, 0)),
        out_shape=jax.ShapeDtypeStruct((n, D_MODEL), F32),
        compiler_params=_cparams(1),
        name="ffn",
    )(x, mod, g, wg, wu, wd)


def _inproj_kernel(x_ref, mod_ref, g_ref, w_ref, wgt_ref, p_ref, gt_ref):
    h = _norm_mod(x_ref[...], g_ref[...], mod_ref[3:4, :], mod_ref[4:5, :]).astype(BF16)
    step = 3 * SEG
    for j in range(0, N_SEG * SEG, step):
        p_ref[:, j:j + step] = jnp.dot(h, w_ref[:, j:j + step], preferred_element_type=F32)
    gt_ref[...] = jnp.dot(h, wgt_ref[...], preferred_element_type=F32)


def _inproj(x, mod, mod_spec, g, w_mix, w_gate):
    n = x.shape[0]
    return pl.pallas_call(
        _inproj_kernel,
        grid=(n // TM,),
        in_specs=[pl.BlockSpec((TM, D_MODEL), lambda i: (i, 0)),
                  mod_spec,
                  _const_spec((1, D_MODEL)),
                  _const_spec((D_MODEL, N_SEG * SEG)),
                  _const_spec((D_MODEL, GATE_LANES))],
        out_specs=[pl.BlockSpec((TM, N_SEG * SEG), lambda i: (i, 0)),
                   pl.BlockSpec((TM, GATE_LANES), lambda i: (i, 0))],
        out_shape=[jax.ShapeDtypeStruct((n, N_SEG * SEG), F32),
                   jax.ShapeDtypeStruct((n, GATE_LANES), F32)],
        compiler_params=_cparams(1),
        name="in_proj",
    )(x, mod, g, w_mix, w_gate)


def _merge_kernel(x_ref, mod_ref, g_ref, ba_ref, bb_ref, bc_ref, bd_ref, wm_ref, wb_ref, wo_ref, o_ref):
    x = x_ref[...]
    h = _norm_mod(x, g_ref[...], mod_ref[3:4, :], mod_ref[4:5, :]).astype(BF16)
    mixed = None
    for m, b_ref in enumerate((ba_ref, bb_ref, bc_ref, bd_ref)):
        logits = jnp.dot(h, wm_ref[:, m * D_MODEL:(m + 1) * D_MODEL], preferred_element_type=F32)
        pb = jnp.dot(b_ref[...].astype(BF16), wb_ref[m], preferred_element_type=F32)
        term = _sigmoid(logits) * pb
        mixed = term if mixed is None else mixed + term
    y = jnp.dot(mixed.astype(BF16), wo_ref[...], preferred_element_type=F32)
    o_ref[...] = x + mod_ref[5:6, :] * y


def _merge(x, mod, mod_spec, g, branches, w_merge, w_branch, w_out):
    n = x.shape[0]
    row = lambda w: pl.BlockSpec((TM, w), lambda i: (i, 0))
    return pl.pallas_call(
        _merge_kernel,
        grid=(n // TM,),
        in_specs=[row(D_MODEL), mod_spec, _const_spec((1, D_MODEL)),
                  row(SEG), row(SEG), row(SEG), row(SEG),
                  _const_spec((D_MODEL, N_BRANCH * D_MODEL)),
                  _const_spec((N_BRANCH, SEG, D_MODEL)),
                  _const_spec((D_MODEL, D_MODEL))],
        out_specs=row(D_MODEL),
        out_shape=jax.ShapeDtypeStruct((n, D_MODEL), F32),
        compiler_params=_cparams(1),
        name="merge",
    )(x, mod, g, *branches, w_merge, w_branch, w_out)


def _seq_specs(T, segs):
    return [pl.BlockSpec((T, SEG), lambda s, j=j: (s, j)) for j in segs]


def _gate_spec(T):
    return pl.BlockSpec((T, GATE_LANES), lambda s: (s, 0))


def _state_in_spec(l, tail):
    nd = len(tail)
    return pl.BlockSpec((None, None) + tail, lambda s: (s, l) + (0,) * nd)


def _lead_spec(tail):
    nd = len(tail)
    return pl.BlockSpec((None,) + tail, lambda s: (s,) + (0,) * nd)


def _mixer_call(kernel, name, n_seq, in_specs, out_specs, out_shape, scratch, args):
    return pl.pallas_call(
        kernel, grid=(n_seq,), in_specs=in_specs, out_specs=out_specs, out_shape=out_shape,
        scratch_shapes=scratch, compiler_params=_cparams(1), name=name)(*args)


def _head_norm_gate(o, bd, ng, gate):
    ss = _gsum(o * o, bd)
    return o * lax.rsqrt(ss * (1.0 / HEAD_DIM) + EPS) * ng * gate


def _epilogue(T, of_s, ob_s, gate_ref, gate_fn, bd_ref, ng_ref, o_ref):
    rb = 256
    bd = bd_ref[...]
    ng = ng_ref[...]

    def blk(i, carry):
        r0 = pl.multiple_of(i * rb, rb)
        o = of_s[pl.ds(r0, rb), :] + ob_s[pl.ds(r0, rb), :]
        o_ref[pl.ds(r0, rb), :] = _head_norm_gate(o, bd, ng, gate_fn(gate_ref[pl.ds(r0, rb), :]))
        return carry

    lax.fori_loop(0, T // rb, blk, 0)


def _chunk_rows(d, n, n_chunks):
    c = n if d == 0 else n_chunks - 1 - n
    return pl.multiple_of(c * CHUNK, CHUNK)


def _hs(h):
    return slice(h * HEAD_DIM, (h + 1) * HEAD_DIM)


def _head_mask():
    ii = lax.broadcasted_iota(jnp.int32, (SEG, SEG), 0) // HEAD_DIM
    jj = lax.broadcasted_iota(jnp.int32, (SEG, SEG), 1) // HEAD_DIM
    return ii == jj


def _blocksum(x):
    c = CHUNK
    return (x[0:c] + x[c:2 * c]) + (x[2 * c:3 * c] + x[3 * c:4 * c])


def _lane_head():
    return lax.broadcasted_iota(jnp.int32, (1, SEG), 1) // HEAD_DIM


def _tiled_masks():
    ii = lax.broadcasted_iota(jnp.int32, (CHUNK, SEG), 0)
    jj = lax.broadcasted_iota(jnp.int32, (CHUNK, SEG), 1) % HEAD_DIM
    return ((jj <= ii, jj < ii), (jj >= ii, jj > ii)), jj == ii


def _block_diag(x, hm):
    return jnp.where(hm, jnp.concatenate([x] * N_HEADS, axis=0), 0.0)


def _col_dense(g, lane0, lane_head):
    out = jnp.broadcast_to(g[:, lane0:lane0 + 1], (CHUNK, SEG))
    for h in range(1, N_HEADS):
        out = jnp.where(lane_head == h, g[:, lane0 + h:lane0 + h + 1], out)
    return out


def _diag_row(col_dense, eye_t):
    return jnp.sum(jnp.where(eye_t, col_dense, 0.0), axis=0, keepdims=True)


def _seg_max(x, lane_head):
    out = None
    for h in range(N_HEADS):
        m = jnp.max(jnp.where(lane_head == h, x, NEG), axis=1, keepdims=True)
        out = m if out is None else jnp.where(lane_head == h, m, out)
    return jnp.broadcast_to(out, x.shape)


UNITS = tuple((d, h) for d in range(2) for h in range(N_HEADS))


def _ret_kernel(*refs, T, has_cache):
    if has_cache:
        (q_ref, k_ref, v_ref, gt_ref, dl_ref, ng_ref, bd_ref, r0_ref, o_ref,
         of_s, ob_s, st_s, dec_s, qdec_s, kdec_s, cdec_s) = refs
    else:
        (q_ref, k_ref, v_ref, gt_ref, dl_ref, ng_ref, bd_ref, o_ref, rf_ref,
         of_s, ob_s, st_s, dec_s, qdec_s, kdec_s, cdec_s) = refs
    n_chunks = T // CHUNK
    masks = _tri_masks()
    hm = _head_mask()

    @pl.when(pl.program_id(0) == 0)
    def _():
        ii = lax.broadcasted_iota(jnp.int32, (CHUNK, CHUNK), 0)
        jj = lax.broadcasted_iota(jnp.int32, (CHUNK, CHUNK), 1)
        rel = (ii - jj).astype(F32)
        pos = lax.broadcasted_iota(jnp.int32, (CHUNK, 1), 0).astype(F32)
        lg_all = _log_sigmoid(dl_ref[...])
        for d in range(2):
            dec, qdec, kdec, cdec = [], [], [], []
            for h in range(N_HEADS):
                lg = lg_all[d:d + 1, h:h + 1]
                if d == 0:
                    e, qd, kd = rel * lg, (pos + 1.0) * lg, (CHUNK - 1.0 - pos) * lg
                else:
                    e, qd, kd = -rel * lg, (CHUNK - pos) * lg, pos * lg
                dec.append(jnp.exp(jnp.where(masks[d][0], e, NEG)))
                qdec.append(jnp.broadcast_to(jnp.exp(qd), (CHUNK, HEAD_DIM)))
                kdec.append(jnp.broadcast_to(jnp.exp(kd), (CHUNK, HEAD_DIM)))
                cdec.append(jnp.broadcast_to(jnp.exp(CHUNK * lg), (HEAD_DIM, SEG)))
            dec_s[d] = jnp.concatenate(dec, axis=0)
            qdec_s[d] = jnp.concatenate(qdec, axis=1)
            kdec_s[d] = jnp.concatenate(kdec, axis=1)
            cdec_s[d] = jnp.concatenate(cdec, axis=0)

    for d in range(2):
        st_s[d] = jnp.zeros((SEG, SEG), F32)
        if has_cache:
            for h in range(N_HEADS):
                st_s[d, _hs(h), _hs(h)] = r0_ref[d, h]

    def body(n, carry):
        rows = [_chunk_rows(d, n, n_chunks) for d in range(2)]
        q = [q_ref[pl.ds(r0, CHUNK), :] for r0 in rows]
        k = [k_ref[pl.ds(r0, CHUNK), :] * (HEAD_DIM ** -0.5) for r0 in rows]
        v = [v_ref[pl.ds(r0, CHUNK), :] for r0 in rows]
        q4 = [jnp.where(hm, jnp.concatenate([x] * N_HEADS, axis=0), 0.0) for x in q]
        qk = [_mm1(a, b, NT) for a, b in zip(q4, k)]
        qr = [_mm1(q[d] * qdec_s[d], st_s[d]) for d in range(2)]
        kv = [_mm1(k[d] * kdec_s[d], v[d], TN) for d in range(2)]
        av = [_mm1(qk[d] * dec_s[d], v[d]) for d in range(2)]
        for d, o_s in enumerate((of_s, ob_s)):
            st_s[d] = cdec_s[d] * st_s[d] + jnp.where(hm, kv[d], 0.0)
            o_s[pl.ds(rows[d], CHUNK), :] = qr[d] + _blocksum(jnp.where(hm, av[d], 0.0))
        return carry

    lax.fori_loop(0, n_chunks, body, 0)
    if not has_cache:
        for d in range(2):
            for h in range(N_HEADS):
                rf_ref[d, h] = st_s[d, _hs(h), _hs(h)]
    _epilogue(T, of_s, ob_s, gt_ref, _silu, bd_ref, ng_ref, o_ref)


def _retention(P, T, n_seq, decay_logit, ng, bd, l, state):
    has_cache = state is not None
    st = (2, N_HEADS, HEAD_DIM, HEAD_DIM)
    in_specs = _seq_specs(T, (S_DQ, S_DK, S_DV, S_DG)) + [
        _const_spec((2, N_HEADS)), _const_spec((1, SEG)), _const_spec((SEG, SEG))]
    args = [P, P, P, P, decay_logit, ng, bd]
    out_specs = [pl.BlockSpec((T, SEG), lambda s: (s, 0))]
    out_shape = [jax.ShapeDtypeStruct((n_seq * T, SEG), F32)]
    if has_cache:
        in_specs.append(_state_in_spec(l, st))
        args.append(state)
    else:
        out_specs.append(_lead_spec(st))
        out_shape.append(jax.ShapeDtypeStruct((n_seq,) + st, F32))
    scratch = [pltpu.VMEM((T, SEG), F32), pltpu.VMEM((T, SEG), F32), pltpu.VMEM((2, SEG, SEG), F32),
               pltpu.VMEM((2, SEG, HEAD_DIM), F32), pltpu.VMEM((2, CHUNK, SEG), F32),
               pltpu.VMEM((2, CHUNK, SEG), F32), pltpu.VMEM((2, SEG, SEG), F32)]
    return _mixer_call(functools.partial(_ret_kernel, T=T, has_cache=has_cache), "retention",
                       n_seq, in_specs, out_specs, out_shape, scratch, args)


def _mlstm_kernel(*refs, T, has_cache):
    if has_cache:
        (q_ref, k_ref, v_ref, og_ref, gt_ref, gp_ref, ng_ref, bd_ref, c0_ref, n0_ref, m0_ref,
         o_ref, of_s, ob_s, c_s, n_s, m_s) = refs
    else:
        (q_ref, k_ref, v_ref, og_ref, gt_ref, gp_ref, ng_ref, bd_ref,
         o_ref, cf_ref, nf_ref, mf_ref, of_s, ob_s, c_s, n_s, m_s) = refs
    n_chunks = T // CHUNK
    masks = _tri_masks()
    tri = (masks[0][0].astype(BF16), masks[1][0].astype(BF16))
    masks_t, eye_t = _tiled_masks()
    tri_t = [masks_t[d][0] for d in range(2)]
    hm = _head_mask()
    lane_head = _lane_head()
    bd = bd_ref[...]
    bias = gp_ref[0:1, :]

    for d in range(2):
        c_s[d] = jnp.zeros((SEG, SEG), F32)
        if has_cache:
            for h in range(N_HEADS):
                c_s[d, _hs(h), _hs(h)] = c0_ref[d, h]
            n_s[d] = n0_ref[d]
            m_s[d] = m0_ref[d]
        else:
            n_s[d] = jnp.zeros((1, SEG), F32)
            m_s[d] = jnp.zeros((1, SEG), F32)

    def body(n, carry):
        rows = [_chunk_rows(d, n, n_chunks) for d in range(2)]
        q = [q_ref[pl.ds(r0, CHUNK), :] * (HEAD_DIM ** -0.5) for r0 in rows]
        k = [k_ref[pl.ds(r0, CHUNK), :] for r0 in rows]
        v = [v_ref[pl.ds(r0, CHUNK), :] for r0 in rows]
        pre = [gt_ref[pl.ds(r0, CHUNK), :] + bias for r0 in rows]
        b = [_sel_mm(tri[d], _log_sigmoid(pre[d])) for d in range(2)]

        k4 = [_block_diag(x, hm) for x in k]
        v4 = [_block_diag(x, hm) for x in v]
        qk = [_mm1(q[d], k4[d], NT) for d in range(2)]
        qc = [_mm1(q[d], c_s[d]) for d in range(2)]
        qn = [_gsum(q[d] * n_s[d], bd) for d in range(2)]

        outs = []
        for d in range(2):
            b_col = _col_dense(b[d], L_CF + d * N_HEADS, lane_head)
            ig_col = _col_dense(pre[d], L_CI + d * N_HEADS, lane_head)
            b_row, ig_row = _diag_row(b_col, eye_t), _diag_row(ig_col, eye_t)
            last = CHUNK - 1 if d == 0 else 0
            b_last = b_col[last:last + 1, :]
            m_prev = m_s[d]
            dm = jnp.where(tri_t[d], b_col - b_row + ig_row, NEG)
            inter = b_col + m_prev
            m_i = jnp.maximum(inter, _seg_max(dm, lane_head))
            s = qk[d] * jnp.exp(dm - m_i)
            sv = _mm1(s, v4[d])
            ssum = _gsum(s, bd)
            w_inter = jnp.exp(inter - m_i)
            num = w_inter * qc[d] + sv
            den = w_inter * qn[d] + ssum
            outs.append(num / jnp.maximum(jnp.abs(den), jnp.exp(-m_i)))

            m_new = jnp.maximum(b_last + m_prev, _seg_max(b_last - b_row + ig_row, lane_head))
            wk = k[d] * jnp.exp(b_last - b_col + ig_col - m_new)
            dec = jnp.exp(b_last + m_prev - m_new)
            kv = _mm1(wk, v[d], TN)
            c_s[d] = dec * c_s[d] + jnp.where(hm, kv, 0.0)
            n_s[d] = dec * n_s[d] + jnp.sum(wk, axis=0, keepdims=True)
            m_s[d] = m_new
        of_s[pl.ds(rows[0], CHUNK), :] = outs[0]
        ob_s[pl.ds(rows[1], CHUNK), :] = outs[1]
        return carry

    lax.fori_loop(0, n_chunks, body, 0)
    if not has_cache:
        for d in range(2):
            for h in range(N_HEADS):
                cf_ref[d, h] = c_s[d, _hs(h), _hs(h)]
        nf_ref[...] = n_s[...]
        mf_ref[...] = m_s[...]
    _epilogue(T, of_s, ob_s, og_ref, _sigmoid, bd_ref, ng_ref, o_ref)


def _mlstm(P, G, T, n_seq, gate_par, ng, bd, state):
    has_cache = state is not None
    st = (2, N_HEADS, HEAD_DIM, HEAD_DIM)
    rw = (2, 1, SEG)
    in_specs = _seq_specs(T, (S_CQ, S_CK, S_CV, S_CO)) + [
        _gate_spec(T), _const_spec((8, GATE_LANES)), _const_spec((1, SEG)), _const_spec((SEG, SEG))]
    args = [P, P, P, P, G, gate_par, ng, bd]
    out_specs = [pl.BlockSpec((T, SEG), lambda s: (s, 0))]
    out_shape = [jax.ShapeDtypeStruct((n_seq * T, SEG), F32)]
    if has_cache:
        in_specs += [_lead_spec(st), _lead_spec(rw), _lead_spec(rw)]
        args += list(state)
    else:
        out_specs += [_lead_spec(st), _lead_spec(rw), _lead_spec(rw)]
        out_shape += [jax.ShapeDtypeStruct((n_seq,) + st, F32)] + [jax.ShapeDtypeStruct((n_seq,) + rw, F32)] * 2
    scratch = [pltpu.VMEM((T, SEG), F32), pltpu.VMEM((T, SEG), F32), pltpu.VMEM((2, SEG, SEG), F32),
               pltpu.VMEM(rw, F32), pltpu.VMEM(rw, F32)]
    return _mixer_call(functools.partial(_mlstm_kernel, T=T, has_cache=has_cache), "mlstm",
                       n_seq, in_specs, out_specs, out_shape, scratch, args)


def _delta_kernel(*refs, T, has_cache):
    if has_cache:
        (q_ref, k_ref, v_ref, z_ref, gt_ref, gp_ref, cw_ref, ng_ref, bd_ref, s0_ref,
         o_ref, of_s, ob_s, st_s, qs, ks, vs) = refs
    else:
        (q_ref, k_ref, v_ref, z_ref, gt_ref, gp_ref, cw_ref, ng_ref, bd_ref,
         o_ref, sf_ref, of_s, ob_s, st_s, qs, ks, vs) = refs
    n_chunks = T // CHUNK
    n_blk = T // ROW_BLOCK
    masks = _tri_masks()
    tri = (masks[0][0].astype(BF16), masks[1][0].astype(BF16))
    masks_t, eye_t = _tiled_masks()
    incl_t = [masks_t[d][0] for d in range(2)]
    strict_t = [masks_t[d][1] for d in range(2)]
    hm = _head_mask()
    lane_head = _lane_head()
    bd = bd_ref[...]
    bias, a_log = gp_ref[0:1, :], gp_ref[1:2, :]
    row = lax.broadcasted_iota(jnp.int32, (ROW_BLOCK, 1), 0)

    def prologue(i, carry):
        r0 = pl.multiple_of(i * ROW_BLOCK, ROW_BLOCK)
        rp = pl.multiple_of(jnp.maximum(r0 - 8, 0), 8)
        rn = pl.multiple_of(jnp.minimum(r0 + ROW_BLOCK, T - 8), 8)
        for j, (src, dst) in enumerate(((q_ref, qs), (k_ref, ks), (v_ref, vs))):
            cur = src[pl.ds(r0, ROW_BLOCK), :]
            before = jnp.where(i > 0, src[pl.ds(rp, 8), :][7:8, :], 0.0)
            after = jnp.where(i < n_blk - 1, src[pl.ds(rn, 8), :][0:1, :], 0.0)
            down = jnp.where(row == 0, before, pltpu.roll(cur, 1, axis=0))
            up = jnp.where(row == ROW_BLOCK - 1, after, pltpu.roll(cur, ROW_BLOCK - 1, axis=0))
            w = cw_ref[:, j * SEG:(j + 1) * SEG]
            y = _silu(w[0:1, :] * down + w[1:2, :] * cur + w[2:3, :] * up)
            if j < 2:
                y = y * lax.rsqrt(_gsum(y * y, bd) + EPS)
            if j == 0:
                y = y * (HEAD_DIM ** -0.5)
            dst[pl.ds(r0, ROW_BLOCK), :] = y
        return carry

    lax.fori_loop(0, n_blk, prologue, 0)

    for d in range(2):
        st_s[d] = jnp.zeros((SEG, SEG), F32)
        if has_cache:
            for h in range(N_HEADS):
                st_s[d, _hs(h), _hs(h)] = s0_ref[d, h]

    def body(n, carry):
        D2 = range(2)
        rows = [_chunk_rows(d, n, n_chunks) for d in D2]
        q = [qs[pl.ds(r0, CHUNK), :] for r0 in rows]
        k = [ks[pl.ds(r0, CHUNK), :] for r0 in rows]
        v = [vs[pl.ds(r0, CHUNK), :] for r0 in rows]
        pre = [gt_ref[pl.ds(r0, CHUNK), :] for r0 in rows]
        g_all = [-jnp.exp(a_log) * _softplus(x + bias) for x in pre]
        cg = [_sel_mm(tri[d], g_all[d]) for d in D2]
        beta = [_col_dense(_sigmoid(pre[d]), L_BETA + d * N_HEADS, lane_head) for d in D2]
        cg_col = [_col_dense(cg[d], L_ALPHA + d * N_HEADS, lane_head) for d in D2]
        cg_row = [_diag_row(x, eye_t) for x in cg_col]
        g_last = [cg_col[0][CHUNK - 1:CHUNK, :], cg_col[1][0:1, :]]
        decay = [jnp.exp(jnp.where(incl_t[d], cg_col[d] - cg_row[d], NEG)) for d in D2]
        kb = [k[d] * beta[d] for d in D2]
        k4 = [_block_diag(x, hm) for x in k]
        kk = [_mm1(kb[d], k4[d], NT) for d in D2]
        qk = [_mm1(q[d], k4[d], NT) for d in D2]
        a = [jnp.where(strict_t[d], kk[d] * decay[d], 0.0) for d in D2]
        xu = [v[d] * beta[d] for d in D2]
        xw = [kb[d] * jnp.exp(cg_col[d]) for d in D2]

        p = [a[d][:, _hs(h)] for d, h in UNITS]
        x = [jnp.concatenate([xu[d][:, _hs(h)], xw[d][:, _hs(h)]], axis=1) for d, h in UNITS]
        px = [_mm2r(a_, b_) for a_, b_ in zip(p, x)]
        x = [a_ - b_ for a_, b_ in zip(x, px)]
        for _ in range(5):
            p = [_mm1(a_, a_) for a_ in p]
            px = [_mm2r(a_, b_) for a_, b_ in zip(p, x)]
            x = [a_ + b_ for a_, b_ in zip(x, px)]
        u = [jnp.concatenate([x[d * N_HEADS + h][:, :HEAD_DIM] for h in range(N_HEADS)], axis=1) for d in D2]
        w = [jnp.concatenate([x[d * N_HEADS + h][:, HEAD_DIM:] for h in range(N_HEADS)], axis=1) for d in D2]

        s = [st_s[d] for d in D2]
        ws = [_mm1(w[d], s[d]) for d in D2]
        qs_ = [_mm1(q[d] * jnp.exp(cg_col[d]), s[d]) for d in D2]
        v_new = [u[d] - ws[d] for d in D2]
        v4 = [_block_diag(x_, hm) for x_ in v_new]
        av = [_mm1(qk[d] * decay[d], v4[d]) for d in D2]
        kv = [_mm1(k[d] * jnp.exp(g_last[d] - cg_col[d]), v_new[d], TN) for d in D2]
        for d, o_s in enumerate((of_s, ob_s)):
            st_s[d] = s[d] * jnp.exp(g_last[d]) + jnp.where(hm, kv[d], 0.0)
            o_s[pl.ds(rows[d], CHUNK), :] = qs_[d] + av[d]
        return carry

    lax.fori_loop(0, n_chunks, body, 0)
    if not has_cache:
        for d in range(2):
            for h in range(N_HEADS):
                sf_ref[d, h] = st_s[d, _hs(h), _hs(h)]
    _epilogue(T, of_s, ob_s, z_ref, _silu, bd_ref, ng_ref, o_ref)


def _deltanet(P, G, T, n_seq, gate_par, conv_w, ng, bd, l, state):
    has_cache = state is not None
    st = (2, N_HEADS, HEAD_DIM, HEAD_DIM)
    in_specs = _seq_specs(T, (S_AQ, S_AK, S_AV, S_AZ)) + [
        _gate_spec(T), _const_spec((8, GATE_LANES)), _const_spec((3, 3 * SEG)),
        _const_spec((1, SEG)), _const_spec((SEG, SEG))]
    args = [P, P, P, P, G, gate_par, conv_w, ng, bd]
    out_specs = [pl.BlockSpec((T, SEG), lambda s: (s, 0))]
    out_shape = [jax.ShapeDtypeStruct((n_seq * T, SEG), F32)]
    if has_cache:
        in_specs.append(_state_in_spec(l, st))
        args.append(state)
    else:
        out_specs.append(_lead_spec(st))
        out_shape.append(jax.ShapeDtypeStruct((n_seq,) + st, F32))
    scratch = ([pltpu.VMEM((T, SEG), F32)] * 2 + [pltpu.VMEM((2, SEG, SEG), F32)]
               + [pltpu.VMEM((T, SEG), F32)] * 3)
    return _mixer_call(functools.partial(_delta_kernel, T=T, has_cache=has_cache), "deltanet",
                       n_seq, in_specs, out_specs, out_shape, scratch, args)


def _diff_kernel(*refs, T, has_cache, lam_init):
    if has_cache:
        (q_ref, k_ref, v_ref, qg_ref, kg_ref, lam_ref, ng_ref, bd32_ref, bd64_ref,
         cos_ref, sin_ref, ck_ref, cv_ref, o_ref, qs, ks) = refs
    else:
        (q_ref, k_ref, v_ref, qg_ref, kg_ref, lam_ref, ng_ref, bd32_ref, bd64_ref,
         o_ref, ko_ref, vo_ref, qs, ks) = refs
    n_blk = T // ROW_BLOCK
    bd32 = bd32_ref[...]
    lane = lax.broadcasted_iota(jnp.int32, (1, SEG), 1)
    first_half = (lane % 16) < 8

    def prologue(i, carry):
        r0 = pl.multiple_of(i * ROW_BLOCK, ROW_BLOCK)
        for src, g_ref, dst in ((q_ref, qg_ref, qs), (k_ref, kg_ref, ks)):
            x = src[pl.ds(r0, ROW_BLOCK), :]
            y = x * lax.rsqrt(_gsum(x * x, bd32) * (1.0 / DQK) + EPS) * g_ref[...]
            if has_cache:
                partner = jnp.where(first_half, pltpu.roll(y, SEG - 8, axis=1), pltpu.roll(y, 8, axis=1))
                y = y * cos_ref[pl.ds(r0, ROW_BLOCK), :] + partner * sin_ref[pl.ds(r0, ROW_BLOCK), :]
            dst[pl.ds(r0, ROW_BLOCK), :] = y
        return carry

    lax.fori_loop(0, n_blk, prologue, 0)

    if not has_cache:
        for h in range(N_HEADS):
            ko_ref[h] = ks[:, _hs(h)]
            vo_ref[h] = v_ref[:, _hs(h)]

    lp = lam_ref[...]
    lam = (jnp.exp(jnp.sum(lp[0:1, :] * lp[1:2, :], axis=1, keepdims=True))
           - jnp.exp(jnp.sum(lp[2:3, :] * lp[3:4, :], axis=1, keepdims=True)) + lam_init)
    scale = DQK ** -0.5
    comp1 = lax.broadcasted_iota(jnp.int32, (1, HEAD_DIM), 1) < DQK
    bd64 = bd64_ref[...]
    ng = ng_ref[...]

    def softmax_parts(scores):
        m = scores[0].max(axis=1, keepdims=True)
        for s in scores[1:]:
            m = jnp.maximum(m, s.max(axis=1, keepdims=True))
        es = [jnp.exp(s - m) for s in scores]
        tot = es[0].sum(axis=1, keepdims=True)
        for e in es[1:]:
            tot = tot + e.sum(axis=1, keepdims=True)
        inv = 1.0 / tot
        return [e * inv for e in es]

    def qblock(i, carry):
        r0 = pl.multiple_of(i * Q_BLOCK, Q_BLOCK)
        qb = qs[pl.ds(r0, Q_BLOCK), :]
        outs = []
        for h in range(N_HEADS):
            qh = qb[:, _hs(h)]
            q1 = jnp.where(comp1, qh, 0.0)
            q2 = qh - q1
            keys = [ks[:, _hs(h)]]
            vals = [v_ref[:, _hs(h)]]
            if has_cache:
                keys.insert(0, ck_ref[h])
                vals.insert(0, cv_ref[h])
            p1 = softmax_parts([_mm1(q1, kk, NT) * scale for kk in keys])
            p2 = softmax_parts([_mm1(q2, kk, NT) * scale for kk in keys])
            o = None
            for a, b, vv in zip(p1, p2, vals):
                t = _mm1(a - lam * b, vv)
                o = t if o is None else o + t
            outs.append(o)
        o = jnp.concatenate(outs, axis=1)
        o_ref[pl.ds(r0, Q_BLOCK), :] = _head_norm_gate(o, bd64, ng, 1.0 - lam_init)
        return carry

    lax.fori_loop(0, T // Q_BLOCK, qblock, 0)


def _diffattn(P, T, n_seq, qg, kg, lam_par, ng, bd32, bd64, l, rope, cache):
    has_cache = cache is not None
    lam_init = 0.8 - 0.6 * math.exp(-0.3 * l)
    kv = (N_HEADS, T, HEAD_DIM)
    in_specs = _seq_specs(T, (S_BQ, S_BK, S_BV)) + [
        _const_spec((1, SEG)), _const_spec((1, SEG)), _const_spec((4, DQK)), _const_spec((1, SEG)),
        _const_spec((SEG, SEG)), _const_spec((SEG, SEG))]
    args = [P, P, P, qg, kg, lam_par, ng, bd32, bd64]
    out_specs = [pl.BlockSpec((T, SEG), lambda s: (s, 0))]
    out_shape = [jax.ShapeDtypeStruct((n_seq * T, SEG), F32)]
    if has_cache:
        ckv = cache[0].shape[2:]
        in_specs += [_const_spec((T, SEG)), _const_spec((T, SEG)), _state_in_spec(l, ckv), _state_in_spec(l, ckv)]
        args += [rope[0], rope[1], cache[0], cache[1]]
    else:
        out_specs += [_lead_spec(kv), _lead_spec(kv)]
        out_shape += [jax.ShapeDtypeStruct((n_seq,) + kv, F32)] * 2
    scratch = [pltpu.VMEM((T, SEG), F32), pltpu.VMEM((T, SEG), F32)]
    return _mixer_call(functools.partial(_diff_kernel, T=T, has_cache=has_cache, lam_init=lam_init),
                       "diff_attn", n_seq, in_specs, out_specs, out_shape, scratch, args)


def _rope_tables(T):
    n_freq = DQK // 4
    t = jnp.arange(T)
    rows = (t // GRID_W).astype(F32)
    cols = (t % GRID_W).astype(F32)
    freqs = ROPE_BASE ** (-jnp.arange(n_freq, dtype=F32) / n_freq)
    ang_r, ang_c = rows[:, None] * freqs, cols[:, None] * freqs

    def comp(fn, sign):
        return jnp.concatenate([fn(ang_r), sign * fn(ang_r), fn(ang_c), sign * fn(ang_c)], axis=1)

    reps = SEG // DQK
    cos = jnp.tile(comp(jnp.cos, 1.0), (1, reps))
    sin = jnp.tile(jnp.concatenate([-jnp.sin(ang_r), jnp.sin(ang_r), -jnp.sin(ang_c), jnp.sin(ang_c)], axis=1),
                   (1, reps))
    return cos, sin


def _block_ones(group):
    i = np.arange(SEG)
    return jnp.asarray(i[:, None] // group == i[None, :] // group, BF16)


def kernel(x_prompt, x_sample, cache_diff_k, cache_diff_v, state_delta, state_mlstm_C, state_mlstm_n, state_mlstm_m, state_ret, c, c_ctx, w_ada, b_ada, norm_g, ffn_w_gate, ffn_w_up, ffn_w_down, w_in, dn_conv_w, dn_a_log, dn_dt_bias, dn_norm_g, da_qn_g, da_kn_g, da_lambda, da_norm_g, ml_i_bias, ml_f_bias, ml_norm_g, ret_decay_logit, ret_norm_g, w_branch, w_out):
    B, T, _ = x_prompt.shape
    Bs, Ts, _ = x_sample.shape
    cond8 = jnp.zeros((8, D_MODEL), F32).at[0].set(c_ctx).at[1:1 + Bs].set(c)
    mod = _ada(cond8, w_ada, b_ada).reshape(DEPTH, 8, N_MOD, D_MODEL)
    bd32, bd64 = _block_ones(DQK), _block_ones(HEAD_DIM)
    rope = _rope_tables(Ts)
    tile_heads = lambda g: jnp.tile(g, SEG // g.shape[0])[None]

    xs = {"ctx": x_prompt.reshape(B * T, D_MODEL), "smp": x_sample.reshape(Bs * Ts, D_MODEL)}
    geo = {"ctx": (T, B, _mod_spec(B * T, 0)), "smp": (Ts, Bs, _mod_spec(Ts, 1))}
    new = {k: [] for k in ("k", "v", "dn", "C", "n", "m", "r")}
    for l in range(DEPTH):
        wg, wu, wd = ffn_w_gate[l].astype(BF16), ffn_w_up[l].astype(BF16), ffn_w_down[l].astype(BF16)
        cols = lambda f: w_in[l][:, _IN_OFFS[f]:_IN_OFFS[f + 1]]
        w_mix = jnp.concatenate([cols(f) for f in _SEG_FIELDS], axis=1).astype(BF16)
        w_gate = jnp.concatenate([cols(f) for f in _GATE_FIELDS]
                                 + [jnp.zeros((D_MODEL, GATE_LANES - 32), F32)], axis=1).astype(BF16)
        w_merge = cols(19).astype(BF16)
        wb, wo = w_branch[l].astype(BF16), w_out[l].astype(BF16)
        gate_par = jnp.zeros((8, GATE_LANES), F32)
        gate_par = gate_par.at[0, L_ALPHA:L_ALPHA + 8].set(dn_dt_bias[l].reshape(8))
        gate_par = gate_par.at[0, L_CI:L_CI + 8].set(ml_i_bias[l].reshape(8))
        gate_par = gate_par.at[0, L_CF:L_CF + 8].set(ml_f_bias[l].reshape(8))
        gate_par = gate_par.at[1, L_ALPHA:L_ALPHA + 8].set(dn_a_log[l].reshape(8))
        ml_state = (state_mlstm_C[:, l], state_mlstm_n[:, l].reshape(Bs, 2, 1, SEG),
                    jnp.repeat(state_mlstm_m[:, l], HEAD_DIM, axis=-1).reshape(Bs, 2, 1, SEG))
        for path in ("ctx", "smp"):
            Tp, n_seq, mspec = geo[path]
            smp = path == "smp"
            x = _ffn(xs[path], mod[l], mspec, norm_g[l, 0][None], wg[0], wu[0], wd[0], 0)
            P, G = _inproj(x, mod[l], mspec, norm_g[l, 1][None], w_mix, w_gate)
            ra = _deltanet(P, G, Tp, n_seq, gate_par, dn_conv_w[l], tile_heads(dn_norm_g[l]), bd64, l,
                           state_delta if smp else None)
            rb = _diffattn(P, Tp, n_seq, tile_heads(da_qn_g[l]), tile_heads(da_kn_g[l]), da_lambda[l],
                           tile_heads(da_norm_g[l]), bd32, bd64, l, rope if smp else None,
                           (cache_diff_k, cache_diff_v) if smp else None)
            rc = _mlstm(P, G, Tp, n_seq, gate_par, tile_heads(ml_norm_g[l]), bd64, ml_state if smp else None)
            rd = _retention(P, Tp, n_seq, ret_decay_logit[l], tile_heads(ret_norm_g[l]), bd64, l,
                            state_ret if smp else None)
            if not smp:
                new["dn"].append(ra[1])
                new["k"].append(rb[1])
                new["v"].append(rb[2])
                new["C"].append(rc[1])
                new["n"].append(rc[2])
                new["m"].append(rc[3])
                new["r"].append(rd[1])
            x = _merge(x, mod[l], mspec, norm_g[l, 1][None], (ra[0], rb[0], rc[0], rd[0]), w_merge, wb, wo)
            xs[path] = _ffn(x, mod[l], mspec, norm_g[l, 2][None], wg[1], wu[1], wd[1], 6)
    st = lambda k: jnp.stack(new[k], axis=1)
    per_head = lambda rows: rows.reshape(B, DEPTH, 2, N_HEADS, HEAD_DIM)
    return (xs["ctx"].reshape(B, T, D_MODEL), xs["smp"].reshape(Bs, Ts, D_MODEL),
            st("k"), st("v"), st("dn"),
            st("C"), per_head(st("n")), per_head(st("m"))[..., 0],
            st("r"))
```

```python
import functools
import math

import numpy as np
import jax
import jax.numpy as jnp
from jax import lax
from jax.experimental import pallas as pl
from jax.experimental.pallas import tpu as pltpu

F32 = jnp.float32
BF16 = jnp.bfloat16

D_MODEL = 1024
FFN_DIM = 2816
N_MOD = 9
DEPTH = 2
N_HEADS = 4
HEAD_DIM = 64
SEG = N_HEADS * HEAD_DIM
N_SEG = 15
CHUNK = 64
DQK = 32
GRID_W = 64
ROPE_BASE = 10000.0
EPS = 1e-6
N_BRANCH = 4
GATE_LANES = 128
NEG = -1e30

TM = 256
Q_BLOCK = 256
ROW_BLOCK = 128
VMEM_LIMIT = 56 * 1024 * 1024

NN = (((1,), (0,)), ((), ()))
NT = (((1,), (1,)), ((), ()))
TN = (((0,), (0,)), ((), ()))

_IN_SIZES = (256, 256, 256, 256, 8, 8, 256, 256, 256, 256, 256, 256, 256, 8, 8, 256, 256, 256, 256, 4096)
_IN_OFFS = np.concatenate([[0], np.cumsum(_IN_SIZES)]).tolist()
_SEG_FIELDS = (0, 1, 2, 3, 6, 7, 8, 9, 10, 11, 12, 15, 16, 17, 18)
_GATE_FIELDS = (4, 5, 13, 14)
(S_AQ, S_AK, S_AV, S_AZ, S_BQ, S_BK, S_BV, S_CQ, S_CK, S_CV, S_CO, S_DQ, S_DK, S_DV, S_DG) = range(N_SEG)
L_BETA, L_ALPHA, L_CI, L_CF = 0, 8, 16, 24


def _dg(a, b, dims):
    return lax.dot_general(a, b, dims, preferred_element_type=F32)


def _split2(x):
    hi = x.astype(BF16)
    lo = (x - hi.astype(F32)).astype(BF16)
    return hi, lo


def _mm1(a, b, dims=NN):
    return _dg(a.astype(BF16), b.astype(BF16), dims)


def _mm3(a, b, dims=NN):
    ah, al = _split2(a)
    bh, bl = _split2(b)
    return _dg(ah, bh, dims) + (_dg(ah, bl, dims) + _dg(al, bh, dims))


def _mm2r(a, b, dims=NN):
    ah = a.astype(BF16)
    bh, bl = _split2(b)
    return _dg(ah, bh, dims) + _dg(ah, bl, dims)


def _sel_mm(sel, x, dims=NN):
    h0 = x.astype(BF16)
    r1 = x - h0.astype(F32)
    h1 = r1.astype(BF16)
    h2 = (r1 - h1.astype(F32)).astype(BF16)
    return _dg(sel, h0, dims) + (_dg(sel, h1, dims) + _dg(sel, h2, dims))


def _gsum(x, bd):
    hi, lo = _split2(x)
    return _dg(hi, bd, NN) + _dg(lo, bd, NN)


def _sigmoid(x):
    return 1.0 / (1.0 + jnp.exp(-x))


def _silu(x):
    return x * _sigmoid(x)


def _softplus(x):
    return jnp.maximum(x, 0.0) + jnp.log1p(jnp.exp(-jnp.abs(x)))


def _log_sigmoid(x):
    return -_softplus(-x)


def _norm_mod(x, g, shift, scale):
    ms = jnp.mean(x * x, axis=-1, keepdims=True)
    return (x * lax.rsqrt(ms + EPS) * g) * (1.0 + scale) + shift


def _tri_masks():
    ii = lax.broadcasted_iota(jnp.int32, (CHUNK, CHUNK), 0)
    jj = lax.broadcasted_iota(jnp.int32, (CHUNK, CHUNK), 1)
    return ((jj <= ii, jj < ii), (jj >= ii, jj > ii))


def _cparams(n_grid=1):
    return pltpu.CompilerParams(dimension_semantics=("arbitrary",) * n_grid,
                                vmem_limit_bytes=VMEM_LIMIT)


def _const_spec(shape):
    nd = len(shape)
    return pl.BlockSpec(shape, lambda *_: (0,) * nd)


def _ada_kernel(s_ref, w_ref, b_ref, o_ref):
    s = s_ref[...]
    o_ref[...] = _mm3(_silu(s), w_ref[...]) + b_ref[...]


def _ada(cond8, w_ada, b_ada):
    tn = 1536
    n_t = (N_MOD * D_MODEL) // tn
    return pl.pallas_call(
        _ada_kernel,
        grid=(DEPTH, n_t),
        in_specs=[pl.BlockSpec((8, D_MODEL), lambda l, j: (0, 0)),
                  pl.BlockSpec((None, D_MODEL, tn), lambda l, j: (l, 0, j)),
                  pl.BlockSpec((None, 1, tn), lambda l, j: (l, 0, j))],
        out_specs=pl.BlockSpec((None, 8, tn), lambda l, j: (l, 0, j)),
        out_shape=jax.ShapeDtypeStruct((DEPTH, 8, N_MOD * D_MODEL), F32),
        compiler_params=_cparams(2),
        name="ada_mod",
    )(cond8, w_ada, b_ada.reshape(DEPTH, 1, N_MOD * D_MODEL))


def _ffn_kernel(x_ref, mod_ref, g_ref, wg_ref, wu_ref, wd_ref, o_ref, *, mi):
    x = x_ref[...]
    h = _norm_mod(x, g_ref[...], mod_ref[mi:mi + 1, :], mod_ref[mi + 1:mi + 2, :]).astype(BF16)
    gate = jnp.dot(h, wg_ref[...], preferred_element_type=F32)
    up = jnp.dot(h, wu_ref[...], preferred_element_type=F32)
    act = (_silu(gate) * up).astype(BF16)
    y = jnp.dot(act, wd_ref[...], preferred_element_type=F32)
    o_ref[...] = x + (0.5 * mod_ref[mi + 2:mi + 3, :]) * y


def _pick_spec(tail, *lead):
    nd = len(tail)
    return pl.BlockSpec((None,) * len(lead) + tuple(tail), lambda *_: tuple(lead) + (0,) * nd)


def _mod_spec(l, rows_per_cond, first_cond):
    per = rows_per_cond // TM
    return pl.BlockSpec((None, None, N_MOD, D_MODEL), lambda i: (l, first_cond + i // per, 0, 0))


def _ffn(x, mod, mod_spec, norm_g, wg, wu, wd, l, j):
    n = x.shape[0]
    return pl.pallas_call(
        functools.partial(_ffn_kernel, mi=6 * j),
        grid=(n // TM,),
        in_specs=[pl.BlockSpec((TM, D_MODEL), lambda i: (i, 0)),
                  mod_spec,
                  _pick_spec((1, D_MODEL), l, 2 * j),
                  _pick_spec((D_MODEL, FFN_DIM), l, j),
                  _pick_spec((D_MODEL, FFN_DIM), l, j),
                  _pick_spec((FFN_DIM, D_MODEL), l, j)],
        out_specs=pl.BlockSpec((TM, D_MODEL), lambda i: (i, 0)),
        out_shape=jax.ShapeDtypeStruct((n, D_MODEL), F32),
        compiler_params=_cparams(1),
        name="ffn",
    )(x, mod, norm_g, wg, wu, wd)


def _inproj_kernel(x_ref, mod_ref, g_ref, w_ref, wgt_ref, p_ref, gt_ref):
    h = _norm_mod(x_ref[...], g_ref[...], mod_ref[3:4, :], mod_ref[4:5, :]).astype(BF16)
    step = 3 * SEG
    for j in range(0, N_SEG * SEG, step):
        p_ref[:, j:j + step] = jnp.dot(h, w_ref[:, j:j + step], preferred_element_type=F32)
    gt_ref[...] = jnp.dot(h, wgt_ref[...], preferred_element_type=F32)


def _inproj(x, mod, mod_spec, norm_g, w_mix, w_gate, l):
    n = x.shape[0]
    return pl.pallas_call(
        _inproj_kernel,
        grid=(n // TM,),
        in_specs=[pl.BlockSpec((TM, D_MODEL), lambda i: (i, 0)),
                  mod_spec,
                  _pick_spec((1, D_MODEL), l, 1),
                  _pick_spec((D_MODEL, N_SEG * SEG), l),
                  _pick_spec((D_MODEL, GATE_LANES), l)],
        out_specs=[pl.BlockSpec((TM, N_SEG * SEG), lambda i: (i, 0)),
                   pl.BlockSpec((TM, GATE_LANES), lambda i: (i, 0))],
        out_shape=[jax.ShapeDtypeStruct((n, N_SEG * SEG), F32),
                   jax.ShapeDtypeStruct((n, GATE_LANES), F32)],
        compiler_params=_cparams(1),
        name="in_proj",
    )(x, mod, norm_g, w_mix, w_gate)


def _merge_kernel(x_ref, mod_ref, g_ref, ba_ref, bb_ref, bc_ref, bd_ref, wm_ref, wb_ref, wo_ref, o_ref):
    x = x_ref[...]
    h = _norm_mod(x, g_ref[...], mod_ref[3:4, :], mod_ref[4:5, :]).astype(BF16)
    mixed = None
    for m, b_ref in enumerate((ba_ref, bb_ref, bc_ref, bd_ref)):
        logits = jnp.dot(h, wm_ref[:, m * D_MODEL:(m + 1) * D_MODEL], preferred_element_type=F32)
        pb = jnp.dot(b_ref[...].astype(BF16), wb_ref[m], preferred_element_type=F32)
        term = _sigmoid(logits) * pb
        mixed = term if mixed is None else mixed + term
    y = jnp.dot(mixed.astype(BF16), wo_ref[...], preferred_element_type=F32)
    o_ref[...] = x + mod_ref[5:6, :] * y


def _merge(x, mod, mod_spec, norm_g, branches, w_merge, w_branch, w_out, l):
    n = x.shape[0]
    row = lambda w: pl.BlockSpec((TM, w), lambda i: (i, 0))
    return pl.pallas_call(
        _merge_kernel,
        grid=(n // TM,),
        in_specs=[row(D_MODEL), mod_spec, _pick_spec((1, D_MODEL), l, 1),
                  row(SEG), row(SEG), row(SEG), row(SEG),
                  _pick_spec((D_MODEL, N_BRANCH * D_MODEL), l),
                  _pick_spec((N_BRANCH, SEG, D_MODEL), l),
                  _pick_spec((D_MODEL, D_MODEL), l)],
        out_specs=row(D_MODEL),
        out_shape=jax.ShapeDtypeStruct((n, D_MODEL), F32),
        compiler_params=_cparams(1),
        name="merge",
    )(x, mod, norm_g, *branches, w_merge, w_branch, w_out)


def _seq_specs(T, segs):
    return [pl.BlockSpec((T, SEG), lambda s, j=j: (s, j)) for j in segs]


def _gate_spec(T):
    return pl.BlockSpec((T, GATE_LANES), lambda s: (s, 0))


def _state_in_spec(l, tail):
    nd = len(tail)
    return pl.BlockSpec((None, None) + tail, lambda s: (s, l) + (0,) * nd)


def _with_state_slabs(kernel, n_in, n_alias, n_state, l, creates, *refs):
    ins, rest = refs[:n_in], refs[n_in + n_alias:]
    main, states, scratch = rest[0], rest[1:1 + n_state], rest[1 + n_state:]
    if creates:
        for r in states:
            for other in range(DEPTH):
                if other != l:
                    r[other] = jnp.zeros(r.shape[1:], F32)
        states = [r.at[l] for r in states]
    return kernel(*ins, main, *states, *scratch)


def _mixer_call(kernel, name, T, n_seq, in_specs, args, scratch, state_tails=(), l=0, prev=None):
    creates = prev is None
    n_in, n_alias = len(args), 0 if creates else len(prev)

    def state_spec(t):
        if creates:
            return pl.BlockSpec((None, DEPTH) + t, lambda s: (s,) + (0,) * (1 + len(t)))
        return _state_in_spec(l, t)

    out_specs = [pl.BlockSpec((T, SEG), lambda s: (s, 0))] + [state_spec(t) for t in state_tails]
    out_shape = ([jax.ShapeDtypeStruct((n_seq * T, SEG), F32)]
                 + [jax.ShapeDtypeStruct((n_seq, DEPTH) + t, F32) for t in state_tails])
    aliases = {}
    if not creates:
        in_specs = in_specs + [pl.BlockSpec(memory_space=pl.ANY)] * n_alias
        args = args + list(prev)
        aliases = {n_in + i: 1 + i for i in range(n_alias)}
    body = functools.partial(_with_state_slabs, kernel, n_in, n_alias, len(state_tails), l, creates)
    return pl.pallas_call(
        body, grid=(n_seq,), in_specs=in_specs, out_specs=out_specs, out_shape=out_shape,
        scratch_shapes=scratch, input_output_aliases=aliases, compiler_params=_cparams(1), name=name)(*args)


def _head_norm_gate(o, bd, ng, gate):
    ss = _gsum(o * o, bd)
    return o * lax.rsqrt(ss * (1.0 / HEAD_DIM) + EPS) * ng * gate


def _epilogue(T, of_s, ob_s, gate_ref, gate_fn, bd_ref, ng_ref, o_ref):
    rb = 256
    bd = bd_ref[...]
    ng = ng_ref[...]

    def blk(i, carry):
        r0 = pl.multiple_of(i * rb, rb)
        o = of_s[pl.ds(r0, rb), :] + ob_s[pl.ds(r0, rb), :]
        o_ref[pl.ds(r0, rb), :] = _head_norm_gate(o, bd, ng, gate_fn(gate_ref[pl.ds(r0, rb), :]))
        return carry

    lax.fori_loop(0, T // rb, blk, 0)


def _chunk_rows(d, n, n_chunks):
    c = n if d == 0 else n_chunks - 1 - n
    return pl.multiple_of(c * CHUNK, CHUNK)


def _hs(h):
    return slice(h * HEAD_DIM, (h + 1) * HEAD_DIM)


def _head_mask():
    ii = lax.broadcasted_iota(jnp.int32, (SEG, SEG), 0) // HEAD_DIM
    jj = lax.broadcasted_iota(jnp.int32, (SEG, SEG), 1) // HEAD_DIM
    return ii == jj


def _blocksum(x):
    c = CHUNK
    return (x[0:c] + x[c:2 * c]) + (x[2 * c:3 * c] + x[3 * c:4 * c])


def _lane_head():
    return lax.broadcasted_iota(jnp.int32, (1, SEG), 1) // HEAD_DIM


def _tiled_masks():
    ii = lax.broadcasted_iota(jnp.int32, (CHUNK, SEG), 0)
    jj = lax.broadcasted_iota(jnp.int32, (CHUNK, SEG), 1) % HEAD_DIM
    return ((jj <= ii, jj < ii), (jj >= ii, jj > ii)), jj == ii


def _block_diag(x, hm):
    return jnp.where(hm, jnp.concatenate([x] * N_HEADS, axis=0), 0.0)


def _col_dense(g, lane0, lane_head):
    out = jnp.broadcast_to(g[:, lane0:lane0 + 1], (CHUNK, SEG))
    for h in range(1, N_HEADS):
        out = jnp.where(lane_head == h, g[:, lane0 + h:lane0 + h + 1], out)
    return out


def _diag_row(col_dense, eye_t):
    return jnp.sum(jnp.where(eye_t, col_dense, 0.0), axis=0, keepdims=True)


def _seg_max(x, lane_head):
    out = None
    for h in range(N_HEADS):
        m = jnp.max(jnp.where(lane_head == h, x, NEG), axis=1, keepdims=True)
        out = m if out is None else jnp.where(lane_head == h, m, out)
    return jnp.broadcast_to(out, x.shape)


UNITS = tuple((d, h) for d in range(2) for h in range(N_HEADS))


def _ret_kernel(*refs, T, has_cache):
    if has_cache:
        (q_ref, k_ref, v_ref, gt_ref, dl_ref, ng_ref, bd_ref, r0_ref, o_ref,
         of_s, ob_s, st_s, dec_s, qdec_s, kdec_s, cdec_s) = refs
    else:
        (q_ref, k_ref, v_ref, gt_ref, dl_ref, ng_ref, bd_ref, o_ref, rf_ref,
         of_s, ob_s, st_s, dec_s, qdec_s, kdec_s, cdec_s) = refs
    n_chunks = T // CHUNK
    masks = _tri_masks()
    hm = _head_mask()

    @pl.when(pl.program_id(0) == 0)
    def _():
        ii = lax.broadcasted_iota(jnp.int32, (CHUNK, CHUNK), 0)
        jj = lax.broadcasted_iota(jnp.int32, (CHUNK, CHUNK), 1)
        rel = (ii - jj).astype(F32)
        pos = lax.broadcasted_iota(jnp.int32, (CHUNK, 1), 0).astype(F32)
        lg_all = _log_sigmoid(dl_ref[...])
        for d in range(2):
            dec, qdec, kdec, cdec = [], [], [], []
            for h in range(N_HEADS):
                lg = lg_all[d:d + 1, h:h + 1]
                if d == 0:
                    e, qd, kd = rel * lg, (pos + 1.0) * lg, (CHUNK - 1.0 - pos) * lg
                else:
                    e, qd, kd = -rel * lg, (CHUNK - pos) * lg, pos * lg
                dec.append(jnp.exp(jnp.where(masks[d][0], e, NEG)))
                qdec.append(jnp.broadcast_to(jnp.exp(qd), (CHUNK, HEAD_DIM)))
                kdec.append(jnp.broadcast_to(jnp.exp(kd), (CHUNK, HEAD_DIM)))
                cdec.append(jnp.broadcast_to(jnp.exp(CHUNK * lg), (HEAD_DIM, SEG)))
            dec_s[d] = jnp.concatenate(dec, axis=0)
            qdec_s[d] = jnp.concatenate(qdec, axis=1)
            kdec_s[d] = jnp.concatenate(kdec, axis=1)
            cdec_s[d] = jnp.concatenate(cdec, axis=0)

    for d in range(2):
        st_s[d] = jnp.zeros((SEG, SEG), F32)
        if has_cache:
            for h in range(N_HEADS):
                st_s[d, _hs(h), _hs(h)] = r0_ref[d, h]

    def body(n, carry):
        rows = [_chunk_rows(d, n, n_chunks) for d in range(2)]
        q = [q_ref[pl.ds(r0, CHUNK), :] for r0 in rows]
        k = [k_ref[pl.ds(r0, CHUNK), :] * (HEAD_DIM ** -0.5) for r0 in rows]
        v = [v_ref[pl.ds(r0, CHUNK), :] for r0 in rows]
        q4 = [jnp.where(hm, jnp.concatenate([x] * N_HEADS, axis=0), 0.0) for x in q]
        qk = [_mm1(a, b, NT) for a, b in zip(q4, k)]
        qr = [_mm1(q[d] * qdec_s[d], st_s[d]) for d in range(2)]
        kv = [_mm1(k[d] * kdec_s[d], v[d], TN) for d in range(2)]
        av = [_mm1(qk[d] * dec_s[d], v[d]) for d in range(2)]
        for d, o_s in enumerate((of_s, ob_s)):
            st_s[d] = cdec_s[d] * st_s[d] + jnp.where(hm, kv[d], 0.0)
            o_s[pl.ds(rows[d], CHUNK), :] = qr[d] + _blocksum(jnp.where(hm, av[d], 0.0))
        return carry

    lax.fori_loop(0, n_chunks, body, 0)
    if not has_cache:
        for d in range(2):
            for h in range(N_HEADS):
                rf_ref[d, h] = st_s[d, _hs(h), _hs(h)]
    _epilogue(T, of_s, ob_s, gt_ref, _silu, bd_ref, ng_ref, o_ref)


def _retention(P, T, n_seq, decay_logit, ng, bd, l, state, prev):
    has_cache = state is not None
    st = (2, N_HEADS, HEAD_DIM, HEAD_DIM)
    in_specs = _seq_specs(T, (S_DQ, S_DK, S_DV, S_DG)) + [
        _const_spec((2, N_HEADS)), _const_spec((1, SEG)), _const_spec((SEG, SEG))]
    args = [P, P, P, P, decay_logit, ng, bd]
    if has_cache:
        in_specs.append(_state_in_spec(l, st))
        args.append(state)
    scratch = [pltpu.VMEM((T, SEG), F32), pltpu.VMEM((T, SEG), F32), pltpu.VMEM((2, SEG, SEG), F32),
               pltpu.VMEM((2, SEG, HEAD_DIM), F32), pltpu.VMEM((2, CHUNK, SEG), F32),
               pltpu.VMEM((2, CHUNK, SEG), F32), pltpu.VMEM((2, SEG, SEG), F32)]
    return _mixer_call(functools.partial(_ret_kernel, T=T, has_cache=has_cache), "retention", T, n_seq,
                       in_specs, args, scratch, () if has_cache else (st,), l, prev)


def _mlstm_kernel(*refs, T, has_cache):
    if has_cache:
        (q_ref, k_ref, v_ref, og_ref, gt_ref, gp_ref, ng_ref, bd_ref, c0_ref, n0_ref, m0_ref,
         o_ref, of_s, ob_s, c_s, n_s, m_s) = refs
    else:
        (q_ref, k_ref, v_ref, og_ref, gt_ref, gp_ref, ng_ref, bd_ref,
         o_ref, cf_ref, nf_ref, mf_ref, of_s, ob_s, c_s, n_s, m_s) = refs
    n_chunks = T // CHUNK
    masks = _tri_masks()
    tri = (masks[0][0].astype(BF16), masks[1][0].astype(BF16))
    masks_t, eye_t = _tiled_masks()
    tri_t = [masks_t[d][0] for d in range(2)]
    hm = _head_mask()
    lane_head = _lane_head()
    bd = bd_ref[...]
    bias = gp_ref[0:1, :]

    for d in range(2):
        c_s[d] = jnp.zeros((SEG, SEG), F32)
        if has_cache:
            for h in range(N_HEADS):
                c_s[d, _hs(h), _hs(h)] = c0_ref[d, h]
            n_s[d] = n0_ref[d]
            m_s[d] = m0_ref[d]
        else:
            n_s[d] = jnp.zeros((1, SEG), F32)
            m_s[d] = jnp.zeros((1, SEG), F32)

    def body(n, carry):
        rows = [_chunk_rows(d, n, n_chunks) for d in range(2)]
        q = [q_ref[pl.ds(r0, CHUNK), :] * (HEAD_DIM ** -0.5) for r0 in rows]
        k = [k_ref[pl.ds(r0, CHUNK), :] for r0 in rows]
        v = [v_ref[pl.ds(r0, CHUNK), :] for r0 in rows]
        pre = [gt_ref[pl.ds(r0, CHUNK), :] + bias for r0 in rows]
        b = [_sel_mm(tri[d], _log_sigmoid(pre[d])) for d in range(2)]

        k4 = [_block_diag(x, hm) for x in k]
        v4 = [_block_diag(x, hm) for x in v]
        qk = [_mm1(q[d], k4[d], NT) for d in range(2)]
        qc = [_mm1(q[d], c_s[d]) for d in range(2)]
        qn = [_gsum(q[d] * n_s[d], bd) for d in range(2)]

        outs = []
        for d in range(2):
            b_col = _col_dense(b[d], L_CF + d * N_HEADS, lane_head)
            ig_col = _col_dense(pre[d], L_CI + d * N_HEADS, lane_head)
            b_row, ig_row = _diag_row(b_col, eye_t), _diag_row(ig_col, eye_t)
            last = CHUNK - 1 if d == 0 else 0
            b_last = b_col[last:last + 1, :]
            m_prev = m_s[d]
            dm = jnp.where(tri_t[d], b_col - b_row + ig_row, NEG)
            inter = b_col + m_prev
            m_i = jnp.maximum(inter, _seg_max(dm, lane_head))
            s = qk[d] * jnp.exp(dm - m_i)
            sv = _mm1(s, v4[d])
            ssum = _gsum(s, bd)
            w_inter = jnp.exp(inter - m_i)
            num = w_inter * qc[d] + sv
            den = w_inter * qn[d] + ssum
            outs.append(num / jnp.maximum(jnp.abs(den), jnp.exp(-m_i)))

            m_new = jnp.maximum(b_last + m_prev, _seg_max(b_last - b_row + ig_row, lane_head))
            wk = k[d] * jnp.exp(b_last - b_col + ig_col - m_new)
            dec = jnp.exp(b_last + m_prev - m_new)
            kv = _mm1(wk, v[d], TN)
            c_s[d] = dec * c_s[d] + jnp.where(hm, kv, 0.0)
            n_s[d] = dec * n_s[d] + jnp.sum(wk, axis=0, keepdims=True)
            m_s[d] = m_new
        of_s[pl.ds(rows[0], CHUNK), :] = outs[0]
        ob_s[pl.ds(rows[1], CHUNK), :] = outs[1]
        return carry

    lax.fori_loop(0, n_chunks, body, 0)
    if not has_cache:
        for d in range(2):
            for h in range(N_HEADS):
                cf_ref[d, h] = c_s[d, _hs(h), _hs(h)]
        nf_ref[...] = n_s[...]
        mf_ref[...] = m_s[...]
    _epilogue(T, of_s, ob_s, og_ref, _sigmoid, bd_ref, ng_ref, o_ref)


def _mlstm(P, G, T, n_seq, gate_par, ng, bd, l, state, prev):
    has_cache = state is not None
    st = (2, N_HEADS, HEAD_DIM, HEAD_DIM)
    rw = (2, 1, SEG)
    in_specs = _seq_specs(T, (S_CQ, S_CK, S_CV, S_CO)) + [
        _gate_spec(T), _pick_spec((8, GATE_LANES), l), _const_spec((1, SEG)), _const_spec((SEG, SEG))]
    args = [P, P, P, P, G, gate_par, ng, bd]
    if has_cache:
        in_specs += [_state_in_spec(l, st), _state_in_spec(l, rw), _state_in_spec(l, rw)]
        args += list(state)
    scratch = [pltpu.VMEM((T, SEG), F32), pltpu.VMEM((T, SEG), F32), pltpu.VMEM((2, SEG, SEG), F32),
               pltpu.VMEM(rw, F32), pltpu.VMEM(rw, F32)]
    return _mixer_call(functools.partial(_mlstm_kernel, T=T, has_cache=has_cache), "mlstm", T, n_seq,
                       in_specs, args, scratch, () if has_cache else (st, rw, rw), l, prev)


INV_BASE = 8
_MM_INV = _mm1
_MM_APPLY = _mm2r


def _inverse_level_masks():
    ii = lax.broadcasted_iota(jnp.int32, (CHUNK, CHUNK), 0)
    jj = lax.broadcasted_iota(jnp.int32, (CHUNK, CHUNK), 1)
    out = []
    for lo, hi in ((jj, ii), (ii, jj)):
        lv = [(lo // INV_BASE == hi // INV_BASE) & (lo < hi)]
        size = 2 * INV_BASE
        while size <= CHUNK:
            lv.append((lo // size == hi // size) & (hi % size >= size // 2) & (lo % size < size // 2))
            size *= 2
        out.append(lv)
    return out


def _delta_kernel(*refs, T, has_cache):
    if has_cache:
        (q_ref, k_ref, v_ref, z_ref, gt_ref, gp_ref, cw_ref, ng_ref, bd_ref, s0_ref,
         o_ref, of_s, ob_s, st_s, qs, ks, vs) = refs
    else:
        (q_ref, k_ref, v_ref, z_ref, gt_ref, gp_ref, cw_ref, ng_ref, bd_ref,
         o_ref, sf_ref, of_s, ob_s, st_s, qs, ks, vs) = refs
    n_chunks = T // CHUNK
    n_blk = T // ROW_BLOCK
    masks = _tri_masks()
    tri = (masks[0][0].astype(BF16), masks[1][0].astype(BF16))
    masks_t, eye_t = _tiled_masks()
    incl_t = [masks_t[d][0] for d in range(2)]
    strict_t = [masks_t[d][1] for d in range(2)]
    hm = _head_mask()
    lane_head = _lane_head()
    lvl_masks = _inverse_level_masks()
    eye64 = (masks[0][0] & masks[1][0]).astype(F32)
    bd = bd_ref[...]
    bias, a_log = gp_ref[0:1, :], gp_ref[1:2, :]
    row = lax.broadcasted_iota(jnp.int32, (ROW_BLOCK, 1), 0)

    def prologue(i, carry):
        r0 = pl.multiple_of(i * ROW_BLOCK, ROW_BLOCK)
        rp = pl.multiple_of(jnp.maximum(r0 - 8, 0), 8)
        rn = pl.multiple_of(jnp.minimum(r0 + ROW_BLOCK, T - 8), 8)
        for j, (src, dst) in enumerate(((q_ref, qs), (k_ref, ks), (v_ref, vs))):
            cur = src[pl.ds(r0, ROW_BLOCK), :]
            before = jnp.where(i > 0, src[pl.ds(rp, 8), :][7:8, :], 0.0)
            after = jnp.where(i < n_blk - 1, src[pl.ds(rn, 8), :][0:1, :], 0.0)
            down = jnp.where(row == 0, before, pltpu.roll(cur, 1, axis=0))
            up = jnp.where(row == ROW_BLOCK - 1, after, pltpu.roll(cur, ROW_BLOCK - 1, axis=0))
            w = cw_ref[:, j * SEG:(j + 1) * SEG]
            y = _silu(w[0:1, :] * down + w[1:2, :] * cur + w[2:3, :] * up)
            if j < 2:
                y = y * lax.rsqrt(_gsum(y * y, bd) + EPS)
            if j == 0:
                y = y * (HEAD_DIM ** -0.5)
            dst[pl.ds(r0, ROW_BLOCK), :] = y
        return carry

    lax.fori_loop(0, n_blk, prologue, 0)

    for d in range(2):
        st_s[d] = jnp.zeros((SEG, SEG), F32)
        if has_cache:
            for h in range(N_HEADS):
                st_s[d, _hs(h), _hs(h)] = s0_ref[d, h]

    def body(n, carry):
        D2 = range(2)
        rows = [_chunk_rows(d, n, n_chunks) for d in D2]
        q = [qs[pl.ds(r0, CHUNK), :] for r0 in rows]
        k = [ks[pl.ds(r0, CHUNK), :] for r0 in rows]
        v = [vs[pl.ds(r0, CHUNK), :] for r0 in rows]
        pre = [gt_ref[pl.ds(r0, CHUNK), :] for r0 in rows]
        g_all = [-jnp.exp(a_log) * _softplus(x + bias) for x in pre]
        cg = [_sel_mm(tri[d], g_all[d]) for d in D2]
        beta = [_col_dense(_sigmoid(pre[d]), L_BETA + d * N_HEADS, lane_head) for d in D2]
        cg_col = [_col_dense(cg[d], L_ALPHA + d * N_HEADS, lane_head) for d in D2]
        cg_row = [_diag_row(x, eye_t) for x in cg_col]
        g_last = [cg_col[0][CHUNK - 1:CHUNK, :], cg_col[1][0:1, :]]
        decay = [jnp.exp(jnp.where(incl_t[d], cg_col[d] - cg_row[d], NEG)) for d in D2]
        kb = [k[d] * beta[d] for d in D2]
        k4 = [_block_diag(x, hm) for x in k]
        kk = [_mm1(kb[d], k4[d], NT) for d in D2]
        qk = [_mm1(q[d], k4[d], NT) for d in D2]
        a = [jnp.where(strict_t[d], kk[d] * decay[d], 0.0) for d in D2]
        xu = [v[d] * beta[d] for d in D2]
        xw = [kb[d] * jnp.exp(cg_col[d]) for d in D2]

        dirs = [d for d, h in UNITS]
        ah = [a[d][:, _hs(h)] for d, h in UNITS]
        x = [jnp.concatenate([xu[d][:, _hs(h)], xw[d][:, _hs(h)]], axis=1) for d, h in UNITS]
        dg = [jnp.where(lvl_masks[d][0], a_, 0.0) for a_, d in zip(ah, dirs)]
        t = [eye64 - d_ for d_ in dg]
        p = [_MM_INV(d_, d_) for d_ in dg]
        pt = [_MM_INV(p_, t_) for p_, t_ in zip(p, t)]
        t = [t_ + u_ for t_, u_ in zip(t, pt)]
        p = [_MM_INV(p_, p_) for p_ in p]
        pt = [_MM_INV(p_, t_) for p_, t_ in zip(p, t)]
        t = [t_ + u_ for t_, u_ in zip(t, pt)]
        for lvl in range(1, len(lvl_masks[0])):
            lo = [jnp.where(lvl_masks[d][lvl], a_, 0.0) for a_, d in zip(ah, dirs)]
            lt = [_MM_INV(l_, t_) for l_, t_ in zip(lo, t)]
            tlt = [_MM_INV(t_, u_) for t_, u_ in zip(t, lt)]
            t = [t_ - u_ for t_, u_ in zip(t, tlt)]
        x = [_MM_APPLY(t_, x_) for t_, x_ in zip(t, x)]
        u = [jnp.concatenate([x[d * N_HEADS + h][:, :HEAD_DIM] for h in range(N_HEADS)], axis=1) for d in D2]
        w = [jnp.concatenate([x[d * N_HEADS + h][:, HEAD_DIM:] for h in range(N_HEADS)], axis=1) for d in D2]

        s = [st_s[d] for d in D2]
        ws = [_mm1(w[d], s[d]) for d in D2]
        qs_ = [_mm1(q[d] * jnp.exp(cg_col[d]), s[d]) for d in D2]
        v_new = [u[d] - ws[d] for d in D2]
        v4 = [_block_diag(x_, hm) for x_ in v_new]
        av = [_mm1(qk[d] * decay[d], v4[d]) for d in D2]
        kv = [_mm1(k[d] * jnp.exp(g_last[d] - cg_col[d]), v_new[d], TN) for d in D2]
        for d, o_s in enumerate((of_s, ob_s)):
            st_s[d] = s[d] * jnp.exp(g_last[d]) + jnp.where(hm, kv[d], 0.0)
            o_s[pl.ds(rows[d], CHUNK), :] = qs_[d] + av[d]
        return carry

    lax.fori_loop(0, n_chunks, body, 0)
    if not has_cache:
        for d in range(2):
            for h in range(N_HEADS):
                sf_ref[d, h] = st_s[d, _hs(h), _hs(h)]
    _epilogue(T, of_s, ob_s, z_ref, _silu, bd_ref, ng_ref, o_ref)


def _deltanet(P, G, T, n_seq, gate_par, conv_w, ng, bd, l, state, prev):
    has_cache = state is not None
    st = (2, N_HEADS, HEAD_DIM, HEAD_DIM)
    in_specs = _seq_specs(T, (S_AQ, S_AK, S_AV, S_AZ)) + [
        _gate_spec(T), _pick_spec((8, GATE_LANES), l), _pick_spec((3, 3 * SEG), l),
        _const_spec((1, SEG)), _const_spec((SEG, SEG))]
    args = [P, P, P, P, G, gate_par, conv_w, ng, bd]
    if has_cache:
        in_specs.append(_state_in_spec(l, st))
        args.append(state)
    scratch = ([pltpu.VMEM((T, SEG), F32)] * 2 + [pltpu.VMEM((2, SEG, SEG), F32)]
               + [pltpu.VMEM((T, SEG), F32)] * 3)
    return _mixer_call(functools.partial(_delta_kernel, T=T, has_cache=has_cache), "deltanet", T, n_seq,
                       in_specs, args, scratch, () if has_cache else (st,), l, prev)


def _diff_kernel(*refs, T, has_cache, lam_init):
    if has_cache:
        (q_ref, k_ref, v_ref, qg_ref, kg_ref, lam_ref, ng_ref, bd32_ref, bd64_ref,
         cos_ref, sin_ref, ck_ref, cv_ref, o_ref, qs, kh, vh) = refs
    else:
        (q_ref, k_ref, v_ref, qg_ref, kg_ref, lam_ref, ng_ref, bd32_ref, bd64_ref,
         o_ref, kh, vh, qs) = refs
    n_blk = T // ROW_BLOCK
    bd32 = bd32_ref[...]
    lane = lax.broadcasted_iota(jnp.int32, (1, SEG), 1)
    first_half = (lane % 16) < 8

    def prologue(i, carry):
        r0 = pl.multiple_of(i * ROW_BLOCK, ROW_BLOCK)
        for src, g_ref in ((q_ref, qg_ref), (k_ref, kg_ref)):
            x = src[pl.ds(r0, ROW_BLOCK), :]
            y = x * lax.rsqrt(_gsum(x * x, bd32) * (1.0 / DQK) + EPS) * g_ref[...]
            if has_cache:
                partner = jnp.where(first_half, pltpu.roll(y, SEG - 8, axis=1), pltpu.roll(y, 8, axis=1))
                y = y * cos_ref[pl.ds(r0, ROW_BLOCK), :] + partner * sin_ref[pl.ds(r0, ROW_BLOCK), :]
            if src is q_ref:
                qs[pl.ds(r0, ROW_BLOCK), :] = y
            else:
                for h in range(N_HEADS):
                    kh[h, pl.ds(r0, ROW_BLOCK), :] = y[:, _hs(h)]
        xv = v_ref[pl.ds(r0, ROW_BLOCK), :]
        for h in range(N_HEADS):
            vh[h, pl.ds(r0, ROW_BLOCK), :] = xv[:, _hs(h)]
        return carry

    lax.fori_loop(0, n_blk, prologue, 0)

    lp = lam_ref[...]
    lam = (jnp.exp(jnp.sum(lp[0:1, :] * lp[1:2, :], axis=1, keepdims=True))
           - jnp.exp(jnp.sum(lp[2:3, :] * lp[3:4, :], axis=1, keepdims=True)) + lam_init)
    scale = DQK ** -0.5
    comp1 = lax.broadcasted_iota(jnp.int32, (1, HEAD_DIM), 1) < DQK
    bd64 = bd64_ref[...]
    ng = ng_ref[...]

    def softmax_times_v(qc, keys, vals):
        scores = [_mm1(qc, kk, NT) for kk in keys]
        m = scores[0].max(axis=1, keepdims=True)
        for s in scores[1:]:
            m = jnp.maximum(m, s.max(axis=1, keepdims=True))
        tot, acc = None, None
        for s, vv in zip(scores, vals):
            e = jnp.exp(s - m)
            t, a = e.sum(axis=1, keepdims=True), _mm1(e, vv)
            tot, acc = (t, a) if tot is None else (tot + t, acc + a)
        return acc * (1.0 / tot)

    def qblock(i, carry):
        r0 = pl.multiple_of(i * Q_BLOCK, Q_BLOCK)
        qb = qs[pl.ds(r0, Q_BLOCK), :] * scale
        outs = []
        for h in range(N_HEADS):
            qh = qb[:, _hs(h)]
            q1 = jnp.where(comp1, qh, 0.0)
            q2 = qh - q1
            keys = [kh[h]]
            vals = [vh[h]]
            if has_cache:
                keys.insert(0, ck_ref[h])
                vals.insert(0, cv_ref[h])
            outs.append(softmax_times_v(q1, keys, vals) - lam * softmax_times_v(q2, keys, vals))
        o = jnp.concatenate(outs, axis=1)
        o_ref[pl.ds(r0, Q_BLOCK), :] = _head_norm_gate(o, bd64, ng, 1.0 - lam_init)
        return carry

    lax.fori_loop(0, T // Q_BLOCK, qblock, 0)


def _diffattn(P, T, n_seq, qg, kg, lam_par, ng, bd32, bd64, l, rope, cache, prev):
    has_cache = cache is not None
    lam_init = 0.8 - 0.6 * math.exp(-0.3 * l)
    kv = (N_HEADS, T, HEAD_DIM)
    in_specs = _seq_specs(T, (S_BQ, S_BK, S_BV)) + [
        _const_spec((1, SEG)), _const_spec((1, SEG)), _pick_spec((4, DQK), l), _const_spec((1, SEG)),
        _const_spec((SEG, SEG)), _const_spec((SEG, SEG))]
    args = [P, P, P, qg, kg, lam_par, ng, bd32, bd64]
    if has_cache:
        ckv = cache[0].shape[2:]
        in_specs += [_const_spec((T, SEG)), _const_spec((T, SEG)), _state_in_spec(l, ckv), _state_in_spec(l, ckv)]
        args += [rope[0], rope[1], cache[0], cache[1]]
    scratch = [pltpu.VMEM((T, SEG), F32)] + ([pltpu.VMEM(kv, F32)] * 2 if has_cache else [])
    return _mixer_call(functools.partial(_diff_kernel, T=T, has_cache=has_cache, lam_init=lam_init),
                       "diff_attn", T, n_seq, in_specs, args, scratch, () if has_cache else (kv, kv), l, prev)


def _rope_tables(T):
    n_freq = DQK // 4
    t = jnp.arange(T)
    rows = (t // GRID_W).astype(F32)
    cols = (t % GRID_W).astype(F32)
    freqs = ROPE_BASE ** (-jnp.arange(n_freq, dtype=F32) / n_freq)
    ang_r, ang_c = rows[:, None] * freqs, cols[:, None] * freqs

    def comp(fn, sign):
        return jnp.concatenate([fn(ang_r), sign * fn(ang_r), fn(ang_c), sign * fn(ang_c)], axis=1)

    reps = SEG // DQK
    cos = jnp.tile(comp(jnp.cos, 1.0), (1, reps))
    sin = jnp.tile(jnp.concatenate([-jnp.sin(ang_r), jnp.sin(ang_r), -jnp.sin(ang_c), jnp.sin(ang_c)], axis=1),
                   (1, reps))
    return cos, sin


def _block_ones(group):
    i = np.arange(SEG)
    return jnp.asarray(i[:, None] // group == i[None, :] // group, BF16)


def kernel(x_prompt, x_sample, cache_diff_k, cache_diff_v, state_delta, state_mlstm_C, state_mlstm_n, state_mlstm_m, state_ret, c, c_ctx, w_ada, b_ada, norm_g, ffn_w_gate, ffn_w_up, ffn_w_down, w_in, dn_conv_w, dn_a_log, dn_dt_bias, dn_norm_g, da_qn_g, da_kn_g, da_lambda, da_norm_g, ml_i_bias, ml_f_bias, ml_norm_g, ret_decay_logit, ret_norm_g, w_branch, w_out):
    B, T, _ = x_prompt.shape
    Bs, Ts, _ = x_sample.shape
    cond8 = jnp.concatenate([c_ctx[None], c, jnp.zeros((8 - 1 - Bs, D_MODEL), F32)], axis=0)
    mod = _ada(cond8, w_ada, b_ada).reshape(DEPTH, 8, N_MOD, D_MODEL)
    bd32, bd64 = _block_ones(DQK), _block_ones(HEAD_DIM)
    rope = _rope_tables(Ts)
    tile_heads = lambda g: jnp.tile(g, SEG // g.shape[0])[None]

    wg, wu, wd = ffn_w_gate.astype(BF16), ffn_w_up.astype(BF16), ffn_w_down.astype(BF16)
    cols = lambda f: w_in[:, :, _IN_OFFS[f]:_IN_OFFS[f + 1]]
    w_mix = jnp.concatenate([cols(f) for f in _SEG_FIELDS], axis=2).astype(BF16)
    w_gate = jnp.concatenate([cols(f) for f in _GATE_FIELDS]
                             + [jnp.zeros((DEPTH, D_MODEL, GATE_LANES - 32), F32)], axis=2).astype(BF16)
    w_merge = cols(19).astype(BF16)
    wb, wo = w_branch.astype(BF16), w_out.astype(BF16)
    norm_g4 = norm_g.reshape(DEPTH, 3, 1, D_MODEL)
    lanes = lambda *parts: jnp.concatenate([p.reshape(DEPTH, 1, -1) for p in parts], axis=2)
    z8 = jnp.zeros((DEPTH, 8), F32)
    gate_par = jnp.concatenate([
        lanes(z8, dn_dt_bias, ml_i_bias, ml_f_bias, jnp.zeros((DEPTH, GATE_LANES - 32), F32)),
        lanes(z8, dn_a_log, jnp.zeros((DEPTH, GATE_LANES - 16), F32)),
        jnp.zeros((DEPTH, 6, GATE_LANES), F32)], axis=1)
    ml_state = (state_mlstm_C, state_mlstm_n.reshape(Bs, DEPTH, 2, 1, SEG),
                jnp.repeat(state_mlstm_m, HEAD_DIM, axis=-1).reshape(Bs, DEPTH, 2, 1, SEG))

    xs = {"ctx": x_prompt.reshape(B * T, D_MODEL), "smp": x_sample.reshape(Bs * Ts, D_MODEL)}
    geo = {"ctx": (T, B, B * T, 0), "smp": (Ts, Bs, Ts, 1)}
    states = {}
    for l in range(DEPTH):
        for path in ("ctx", "smp"):
            Tp, n_seq, rows_per_cond, first_cond = geo[path]
            mspec = _mod_spec(l, rows_per_cond, first_cond)
            smp = path == "smp"
            prev = states.get if not smp else (lambda name: None)
            x = _ffn(xs[path], mod, mspec, norm_g4, wg, wu, wd, l, 0)
            P, G = _inproj(x, mod, mspec, norm_g4, w_mix, w_gate, l)
            ra = _deltanet(P, G, Tp, n_seq, gate_par, dn_conv_w, tile_heads(dn_norm_g[l]), bd64, l,
                           state_delta if smp else None, prev("dn"))
            rb = _diffattn(P, Tp, n_seq, tile_heads(da_qn_g[l]), tile_heads(da_kn_g[l]), da_lambda,
                           tile_heads(da_norm_g[l]), bd32, bd64, l, rope if smp else None,
                           (cache_diff_k, cache_diff_v) if smp else None, prev("kv"))
            rc = _mlstm(P, G, Tp, n_seq, gate_par, tile_heads(ml_norm_g[l]), bd64, l,
                        ml_state if smp else None, prev("ml"))
            rd = _retention(P, Tp, n_seq, ret_decay_logit[l], tile_heads(ret_norm_g[l]), bd64, l,
                            state_ret if smp else None, prev("r"))
            if not smp:
                states = {"dn": ra[1:], "kv": rb[1:], "ml": rc[1:], "r": rd[1:]}
            x = _merge(x, mod, mspec, norm_g4, (ra[0], rb[0], rc[0], rd[0]), w_merge, wb, wo, l)
            xs[path] = _ffn(x, mod, mspec, norm_g4, wg, wu, wd, l, 1)
    (new_dn,), (new_k, new_v), (new_c, new_n, new_m), (new_r,) = (states[k] for k in ("dn", "kv", "ml", "r"))
    per_head = lambda rows: rows.reshape(B, DEPTH, 2, N_HEADS, HEAD_DIM)
    return (xs["ctx"].reshape(B, T, D_MODEL), xs["smp"].reshape(Bs, Ts, D_MODEL),
            new_k, new_v, new_dn, new_c, per_head(new_n), per_head(new_m)[..., 0], new_r)
```

```python
import functools
import math

import numpy as np
import jax
import jax.numpy as jnp
from jax import lax
from jax.experimental import pallas as pl
from jax.experimental.pallas import tpu as pltpu

F32 = jnp.float32
BF16 = jnp.bfloat16

D_MODEL = 1024
FFN_DIM = 2816
N_MOD = 9
DEPTH = 2
N_HEADS = 4
HEAD_DIM = 64
SEG = N_HEADS * HEAD_DIM
N_SEG = 15
CHUNK = 64
DQK = 32
GRID_W = 64
ROPE_BASE = 10000.0
EPS = 1e-6
N_BRANCH = 4
GATE_LANES = 128
NEG = -1e30

TM = 256
Q_BLOCK = 256
ROW_BLOCK = 128
VMEM_LIMIT = 56 * 1024 * 1024

NN = (((1,), (0,)), ((), ()))
NT = (((1,), (1,)), ((), ()))
TN = (((0,), (0,)), ((), ()))

_IN_SIZES = (256, 256, 256, 256, 8, 8, 256, 256, 256, 256, 256, 256, 256, 8, 8, 256, 256, 256, 256, 4096)
_IN_OFFS = np.concatenate([[0], np.cumsum(_IN_SIZES)]).tolist()
_SEG_FIELDS = (0, 1, 2, 3, 6, 7, 8, 9, 10, 11, 12, 15, 16, 17, 18)
_GATE_FIELDS = (4, 5, 13, 14)
(S_AQ, S_AK, S_AV, S_AZ, S_BQ, S_BK, S_BV, S_CQ, S_CK, S_CV, S_CO, S_DQ, S_DK, S_DV, S_DG) = range(N_SEG)
L_BETA, L_ALPHA, L_CI, L_CF = 0, 8, 16, 24


def _dg(a, b, dims):
    return lax.dot_general(a, b, dims, preferred_element_type=F32)


def _split2(x):
    hi = x.astype(BF16)
    lo = (x - hi.astype(F32)).astype(BF16)
    return hi, lo


def _mm1(a, b, dims=NN):
    return _dg(a.astype(BF16), b.astype(BF16), dims)


def _mm3(a, b, dims=NN):
    ah, al = _split2(a)
    bh, bl = _split2(b)
    return _dg(ah, bh, dims) + (_dg(ah, bl, dims) + _dg(al, bh, dims))


def _mm2r(a, b, dims=NN):
    ah = a.astype(BF16)
    bh, bl = _split2(b)
    return _dg(ah, bh, dims) + _dg(ah, bl, dims)


def _sel_mm(sel, x, dims=NN):
    h0 = x.astype(BF16)
    r1 = x - h0.astype(F32)
    h1 = r1.astype(BF16)
    h2 = (r1 - h1.astype(F32)).astype(BF16)
    return _dg(sel, h0, dims) + (_dg(sel, h1, dims) + _dg(sel, h2, dims))


def _gsum(x, bd):
    hi, lo = _split2(x)
    return _dg(hi, bd, NN) + _dg(lo, bd, NN)


def _sigmoid(x):
    return 1.0 / (1.0 + jnp.exp(-x))


def _silu(x):
    return x * _sigmoid(x)


def _softplus(x):
    return jnp.maximum(x, 0.0) + jnp.log1p(jnp.exp(-jnp.abs(x)))


def _log_sigmoid(x):
    return -_softplus(-x)


def _norm_mod(x, g, shift, scale):
    ms = jnp.mean(x * x, axis=-1, keepdims=True)
    return (x * lax.rsqrt(ms + EPS) * g) * (1.0 + scale) + shift


def _tri_masks():
    ii = lax.broadcasted_iota(jnp.int32, (CHUNK, CHUNK), 0)
    jj = lax.broadcasted_iota(jnp.int32, (CHUNK, CHUNK), 1)
    return ((jj <= ii, jj < ii), (jj >= ii, jj > ii))


def _cparams(n_grid=1):
    return pltpu.CompilerParams(dimension_semantics=("arbitrary",) * n_grid,
                                vmem_limit_bytes=VMEM_LIMIT)


def _const_spec(shape):
    nd = len(shape)
    return pl.BlockSpec(shape, lambda *_: (0,) * nd)


def _ada_kernel(s_ref, w_ref, b_ref, o_ref):
    s = s_ref[...]
    o_ref[...] = _mm3(_silu(s), w_ref[...]) + b_ref[...]


def _ada(cond8, w_ada, b_ada):
    tn = 1536
    n_t = (N_MOD * D_MODEL) // tn
    return pl.pallas_call(
        _ada_kernel,
        grid=(DEPTH, n_t),
        in_specs=[pl.BlockSpec((8, D_MODEL), lambda l, j: (0, 0)),
                  pl.BlockSpec((None, D_MODEL, tn), lambda l, j: (l, 0, j)),
                  pl.BlockSpec((None, 1, tn), lambda l, j: (l, 0, j))],
        out_specs=pl.BlockSpec((None, 8, tn), lambda l, j: (l, 0, j)),
        out_shape=jax.ShapeDtypeStruct((DEPTH, 8, N_MOD * D_MODEL), F32),
        compiler_params=_cparams(2),
        name="ada_mod",
    )(cond8, w_ada, b_ada.reshape(DEPTH, 1, N_MOD * D_MODEL))


def _ffn_kernel(x_ref, mod_ref, g_ref, wg_ref, wu_ref, wd_ref, o_ref, *, mi):
    x = x_ref[...]
    h = _norm_mod(x, g_ref[...], mod_ref[mi:mi + 1, :], mod_ref[mi + 1:mi + 2, :]).astype(BF16)
    gate = jnp.dot(h, wg_ref[...], preferred_element_type=F32)
    up = jnp.dot(h, wu_ref[...], preferred_element_type=F32)
    act = (_silu(gate) * up).astype(BF16)
    y = jnp.dot(act, wd_ref[...], preferred_element_type=F32)
    o_ref[...] = x + (0.5 * mod_ref[mi + 2:mi + 3, :]) * y


def _pick_spec(tail, *lead):
    nd = len(tail)
    return pl.BlockSpec((None,) * len(lead) + tuple(tail), lambda *_: tuple(lead) + (0,) * nd)


def _mod_spec(l, rows_per_cond, first_cond):
    per = rows_per_cond // TM
    return pl.BlockSpec((None, None, N_MOD, D_MODEL), lambda i: (l, first_cond + i // per, 0, 0))


def _ffn(x, mod, mod_spec, norm_g, wg, wu, wd, l, j):
    n = x.shape[0]
    return pl.pallas_call(
        functools.partial(_ffn_kernel, mi=6 * j),
        grid=(n // TM,),
        in_specs=[pl.BlockSpec((TM, D_MODEL), lambda i: (i, 0)),
                  mod_spec,
                  _pick_spec((1, D_MODEL), l, 2 * j),
                  _pick_spec((D_MODEL, FFN_DIM), l, j),
                  _pick_spec((D_MODEL, FFN_DIM), l, j),
                  _pick_spec((FFN_DIM, D_MODEL), l, j)],
        out_specs=pl.BlockSpec((TM, D_MODEL), lambda i: (i, 0)),
        out_shape=jax.ShapeDtypeStruct((n, D_MODEL), F32),
        compiler_params=_cparams(1),
        name="ffn",
    )(x, mod, norm_g, wg, wu, wd)


def _inproj_kernel(x_ref, mod_ref, g_ref, w_ref, wgt_ref, p_ref, gt_ref):
    h = _norm_mod(x_ref[...], g_ref[...], mod_ref[3:4, :], mod_ref[4:5, :]).astype(BF16)
    step = 3 * SEG
    for j in range(0, N_SEG * SEG, step):
        p_ref[:, j:j + step] = jnp.dot(h, w_ref[:, j:j + step], preferred_element_type=F32)
    gt_ref[...] = jnp.dot(h, wgt_ref[...], preferred_element_type=F32)


def _inproj(x, mod, mod_spec, norm_g, w_mix, w_gate, l):
    n = x.shape[0]
    return pl.pallas_call(
        _inproj_kernel,
        grid=(n // TM,),
        in_specs=[pl.BlockSpec((TM, D_MODEL), lambda i: (i, 0)),
                  mod_spec,
                  _pick_spec((1, D_MODEL), l, 1),
                  _pick_spec((D_MODEL, N_SEG * SEG), l),
                  _pick_spec((D_MODEL, GATE_LANES), l)],
        out_specs=[pl.BlockSpec((TM, N_SEG * SEG), lambda i: (i, 0)),
                   pl.BlockSpec((TM, GATE_LANES), lambda i: (i, 0))],
        out_shape=[jax.ShapeDtypeStruct((n, N_SEG * SEG), F32),
                   jax.ShapeDtypeStruct((n, GATE_LANES), F32)],
        compiler_params=_cparams(1),
        name="in_proj",
    )(x, mod, norm_g, w_mix, w_gate)


def _merge_kernel(x_ref, mod_ref, g_ref, ba_ref, bb_ref, bc_ref, bd_ref, wm_ref, wb_ref, wo_ref, o_ref):
    x = x_ref[...]
    h = _norm_mod(x, g_ref[...], mod_ref[3:4, :], mod_ref[4:5, :]).astype(BF16)
    mixed = None
    for m, b_ref in enumerate((ba_ref, bb_ref, bc_ref, bd_ref)):
        logits = jnp.dot(h, wm_ref[:, m * D_MODEL:(m + 1) * D_MODEL], preferred_element_type=F32)
        pb = jnp.dot(b_ref[...].astype(BF16), wb_ref[m], preferred_element_type=F32)
        term = _sigmoid(logits) * pb
        mixed = term if mixed is None else mixed + term
    y = jnp.dot(mixed.astype(BF16), wo_ref[...], preferred_element_type=F32)
    o_ref[...] = x + mod_ref[5:6, :] * y


def _merge(x, mod, mod_spec, norm_g, branches, w_merge, w_branch, w_out, l):
    n = x.shape[0]
    row = lambda w: pl.BlockSpec((TM, w), lambda i: (i, 0))
    return pl.pallas_call(
        _merge_kernel,
        grid=(n // TM,),
        in_specs=[row(D_MODEL), mod_spec, _pick_spec((1, D_MODEL), l, 1),
                  row(SEG), row(SEG), row(SEG), row(SEG),
                  _pick_spec((D_MODEL, N_BRANCH * D_MODEL), l),
                  _pick_spec((N_BRANCH, SEG, D_MODEL), l),
                  _pick_spec((D_MODEL, D_MODEL), l)],
        out_specs=row(D_MODEL),
        out_shape=jax.ShapeDtypeStruct((n, D_MODEL), F32),
        compiler_params=_cparams(1),
        name="merge",
    )(x, mod, norm_g, *branches, w_merge, w_branch, w_out)


def _seq_specs(T, segs):
    return [pl.BlockSpec((T, SEG), lambda s, j=j: (s, j)) for j in segs]


def _gate_spec(T):
    return pl.BlockSpec((T, GATE_LANES), lambda s: (s, 0))


def _state_in_spec(l, tail):
    nd = len(tail)
    return pl.BlockSpec((None, None) + tail, lambda s: (s, l) + (0,) * nd)


def _with_state_slabs(kernel, n_in, n_alias, n_state, l, creates, *refs):
    ins, rest = refs[:n_in], refs[n_in + n_alias:]
    main, states, scratch = rest[0], rest[1:1 + n_state], rest[1 + n_state:]
    if creates:
        for r in states:
            for other in range(DEPTH):
                if other != l:
                    r[other] = jnp.zeros(r.shape[1:], F32)
        states = [r.at[l] for r in states]
    return kernel(*ins, main, *states, *scratch)


def _mixer_call(kernel, name, T, n_seq, in_specs, args, scratch, state_tails=(), l=0, prev=None):
    creates = prev is None
    n_in, n_alias = len(args), 0 if creates else len(prev)

    def state_spec(t):
        if creates:
            return pl.BlockSpec((None, DEPTH) + t, lambda s: (s,) + (0,) * (1 + len(t)))
        return _state_in_spec(l, t)

    out_specs = [pl.BlockSpec((T, SEG), lambda s: (s, 0))] + [state_spec(t) for t in state_tails]
    out_shape = ([jax.ShapeDtypeStruct((n_seq * T, SEG), F32)]
                 + [jax.ShapeDtypeStruct((n_seq, DEPTH) + t, F32) for t in state_tails])
    aliases = {}
    if not creates:
        in_specs = in_specs + [pl.BlockSpec(memory_space=pl.ANY)] * n_alias
        args = args + list(prev)
        aliases = {n_in + i: 1 + i for i in range(n_alias)}
    body = functools.partial(_with_state_slabs, kernel, n_in, n_alias, len(state_tails), l, creates)
    return pl.pallas_call(
        body, grid=(n_seq,), in_specs=in_specs, out_specs=out_specs, out_shape=out_shape,
        scratch_shapes=scratch, input_output_aliases=aliases, compiler_params=_cparams(1), name=name)(*args)


def _head_norm_gate(o, bd, ng, gate):
    ss = _gsum(o * o, bd)
    return o * lax.rsqrt(ss * (1.0 / HEAD_DIM) + EPS) * ng * gate


def _epilogue(T, of_s, ob_s, gate_ref, gate_fn, bd_ref, ng_ref, o_ref):
    rb = 256
    bd = bd_ref[...]
    ng = ng_ref[...]

    def blk(i, carry):
        r0 = pl.multiple_of(i * rb, rb)
        o = of_s[pl.ds(r0, rb), :] + ob_s[pl.ds(r0, rb), :]
        o_ref[pl.ds(r0, rb), :] = _head_norm_gate(o, bd, ng, gate_fn(gate_ref[pl.ds(r0, rb), :]))
        return carry

    lax.fori_loop(0, T // rb, blk, 0)


def _chunk_rows(d, n, n_chunks):
    c = n if d == 0 else n_chunks - 1 - n
    return pl.multiple_of(c * CHUNK, CHUNK)


def _hs(h):
    return slice(h * HEAD_DIM, (h + 1) * HEAD_DIM)


def _head_mask():
    ii = lax.broadcasted_iota(jnp.int32, (SEG, SEG), 0) // HEAD_DIM
    jj = lax.broadcasted_iota(jnp.int32, (SEG, SEG), 1) // HEAD_DIM
    return ii == jj


def _blocksum(x):
    c = CHUNK
    return (x[0:c] + x[c:2 * c]) + (x[2 * c:3 * c] + x[3 * c:4 * c])


def _lane_head():
    return lax.broadcasted_iota(jnp.int32, (1, SEG), 1) // HEAD_DIM


def _tiled_masks():
    ii = lax.broadcasted_iota(jnp.int32, (CHUNK, SEG), 0)
    jj = lax.broadcasted_iota(jnp.int32, (CHUNK, SEG), 1) % HEAD_DIM
    return ((jj <= ii, jj < ii), (jj >= ii, jj > ii)), jj == ii


def _block_diag(x, hm):
    return jnp.where(hm, jnp.concatenate([x] * N_HEADS, axis=0), 0.0)


def _col_dense(g, lane0, lane_head):
    out = jnp.broadcast_to(g[:, lane0:lane0 + 1], (CHUNK, SEG))
    for h in range(1, N_HEADS):
        out = jnp.where(lane_head == h, g[:, lane0 + h:lane0 + h + 1], out)
    return out


def _diag_row(col_dense, eye_t):
    return jnp.sum(jnp.where(eye_t, col_dense, 0.0), axis=0, keepdims=True)


def _seg_max(x, lane_head):
    out = None
    for h in range(N_HEADS):
        m = jnp.max(jnp.where(lane_head == h, x, NEG), axis=1, keepdims=True)
        out = m if out is None else jnp.where(lane_head == h, m, out)
    return jnp.broadcast_to(out, x.shape)


UNITS = tuple((d, h) for d in range(2) for h in range(N_HEADS))


def _ret_kernel(*refs, T, has_cache):
    if has_cache:
        (q_ref, k_ref, v_ref, gt_ref, dl_ref, ng_ref, bd_ref, r0_ref, o_ref,
         of_s, ob_s, st_s, dec_s, qdec_s, kdec_s, cdec_s) = refs
    else:
        (q_ref, k_ref, v_ref, gt_ref, dl_ref, ng_ref, bd_ref, o_ref, rf_ref,
         of_s, ob_s, st_s, dec_s, qdec_s, kdec_s, cdec_s) = refs
    n_chunks = T // CHUNK
    masks = _tri_masks()
    hm = _head_mask()

    @pl.when(pl.program_id(0) == 0)
    def _():
        ii = lax.broadcasted_iota(jnp.int32, (CHUNK, CHUNK), 0)
        jj = lax.broadcasted_iota(jnp.int32, (CHUNK, CHUNK), 1)
        rel = (ii - jj).astype(F32)
        pos = lax.broadcasted_iota(jnp.int32, (CHUNK, 1), 0).astype(F32)
        lg_all = _log_sigmoid(dl_ref[...])
        for d in range(2):
            dec, qdec, kdec, cdec = [], [], [], []
            for h in range(N_HEADS):
                lg = lg_all[d:d + 1, h:h + 1]
                if d == 0:
                    e, qd, kd = rel * lg, (pos + 1.0) * lg, (CHUNK - 1.0 - pos) * lg
                else:
                    e, qd, kd = -rel * lg, (CHUNK - pos) * lg, pos * lg
                dec.append(jnp.exp(jnp.where(masks[d][0], e, NEG)))
                qdec.append(jnp.broadcast_to(jnp.exp(qd), (CHUNK, HEAD_DIM)))
                kdec.append(jnp.broadcast_to(jnp.exp(kd), (CHUNK, HEAD_DIM)))
                cdec.append(jnp.broadcast_to(jnp.exp(CHUNK * lg), (HEAD_DIM, SEG)))
            dec_s[d] = jnp.concatenate(dec, axis=0)
            qdec_s[d] = jnp.concatenate(qdec, axis=1)
            kdec_s[d] = jnp.concatenate(kdec, axis=1)
            cdec_s[d] = jnp.concatenate(cdec, axis=0)

    for d in range(2):
        st_s[d] = jnp.zeros((SEG, SEG), F32)
        if has_cache:
            for h in range(N_HEADS):
                st_s[d, _hs(h), _hs(h)] = r0_ref[d, h]

    def body(n, carry):
        rows = [_chunk_rows(d, n, n_chunks) for d in range(2)]
        q = [q_ref[pl.ds(r0, CHUNK), :] for r0 in rows]
        k = [k_ref[pl.ds(r0, CHUNK), :] * (HEAD_DIM ** -0.5) for r0 in rows]
        v = [v_ref[pl.ds(r0, CHUNK), :] for r0 in rows]
        q4 = [jnp.where(hm, jnp.concatenate([x] * N_HEADS, axis=0), 0.0) for x in q]
        qk = [_mm1(a, b, NT) for a, b in zip(q4, k)]
        qr = [_mm1(q[d] * qdec_s[d], st_s[d]) for d in range(2)]
        kv = [_mm1(k[d] * kdec_s[d], v[d], TN) for d in range(2)]
        av = [_mm1(qk[d] * dec_s[d], v[d]) for d in range(2)]
        for d, o_s in enumerate((of_s, ob_s)):
            st_s[d] = cdec_s[d] * st_s[d] + jnp.where(hm, kv[d], 0.0)
            o_s[pl.ds(rows[d], CHUNK), :] = qr[d] + _blocksum(jnp.where(hm, av[d], 0.0))
        return carry

    lax.fori_loop(0, n_chunks, body, 0)
    if not has_cache:
        for d in range(2):
            for h in range(N_HEADS):
                rf_ref[d, h] = st_s[d, _hs(h), _hs(h)]
    _epilogue(T, of_s, ob_s, gt_ref, _silu, bd_ref, ng_ref, o_ref)


def _retention(P, T, n_seq, decay_logit, ng, bd, l, state, prev):
    has_cache = state is not None
    st = (2, N_HEADS, HEAD_DIM, HEAD_DIM)
    in_specs = _seq_specs(T, (S_DQ, S_DK, S_DV, S_DG)) + [
        _const_spec((2, N_HEADS)), _const_spec((1, SEG)), _const_spec((SEG, SEG))]
    args = [P, P, P, P, decay_logit, ng, bd]
    if has_cache:
        in_specs.append(_state_in_spec(l, st))
        args.append(state)
    scratch = [pltpu.VMEM((T, SEG), F32), pltpu.VMEM((T, SEG), F32), pltpu.VMEM((2, SEG, SEG), F32),
               pltpu.VMEM((2, SEG, HEAD_DIM), F32), pltpu.VMEM((2, CHUNK, SEG), F32),
               pltpu.VMEM((2, CHUNK, SEG), F32), pltpu.VMEM((2, SEG, SEG), F32)]
    return _mixer_call(functools.partial(_ret_kernel, T=T, has_cache=has_cache), "retention", T, n_seq,
                       in_specs, args, scratch, () if has_cache else (st,), l, prev)


def _mlstm_kernel(*refs, T, has_cache):
    if has_cache:
        (q_ref, k_ref, v_ref, og_ref, gt_ref, gp_ref, ng_ref, bd_ref, c0_ref, n0_ref, m0_ref,
         o_ref, of_s, ob_s, c_s, n_s, m_s) = refs
    else:
        (q_ref, k_ref, v_ref, og_ref, gt_ref, gp_ref, ng_ref, bd_ref,
         o_ref, cf_ref, nf_ref, mf_ref, of_s, ob_s, c_s, n_s, m_s) = refs
    n_chunks = T // CHUNK
    masks = _tri_masks()
    tri = (masks[0][0].astype(BF16), masks[1][0].astype(BF16))
    masks_t, eye_t = _tiled_masks()
    tri_t = [masks_t[d][0] for d in range(2)]
    hm = _head_mask()
    lane_head = _lane_head()
    bd = bd_ref[...]
    bias = gp_ref[0:1, :]

    for d in range(2):
        c_s[d] = jnp.zeros((SEG, SEG), F32)
        if has_cache:
            for h in range(N_HEADS):
                c_s[d, _hs(h), _hs(h)] = c0_ref[d, h]
            n_s[d] = n0_ref[d]
            m_s[d] = m0_ref[d]
        else:
            n_s[d] = jnp.zeros((1, SEG), F32)
            m_s[d] = jnp.zeros((1, SEG), F32)

    def body(n, carry):
        rows = [_chunk_rows(d, n, n_chunks) for d in range(2)]
        q = [q_ref[pl.ds(r0, CHUNK), :] * (HEAD_DIM ** -0.5) for r0 in rows]
        k = [k_ref[pl.ds(r0, CHUNK), :] for r0 in rows]
        v = [v_ref[pl.ds(r0, CHUNK), :] for r0 in rows]
        pre = [gt_ref[pl.ds(r0, CHUNK), :] + bias for r0 in rows]
        b = [_sel_mm(tri[d], _log_sigmoid(pre[d])) for d in range(2)]

        k4 = [_block_diag(x, hm) for x in k]
        v4 = [_block_diag(x, hm) for x in v]
        qk = [_mm1(q[d], k4[d], NT) for d in range(2)]
        qc = [_mm1(q[d], c_s[d]) for d in range(2)]
        qn = [_gsum(q[d] * n_s[d], bd) for d in range(2)]

        outs = []
        for d in range(2):
            b_col = _col_dense(b[d], L_CF + d * N_HEADS, lane_head)
            ig_col = _col_dense(pre[d], L_CI + d * N_HEADS, lane_head)
            b_row, ig_row = _diag_row(b_col, eye_t), _diag_row(ig_col, eye_t)
            last = CHUNK - 1 if d == 0 else 0
            b_last = b_col[last:last + 1, :]
            m_prev = m_s[d]
            dm = jnp.where(tri_t[d], b_col - b_row + ig_row, NEG)
            inter = b_col + m_prev
            m_i = jnp.maximum(inter, _seg_max(dm, lane_head))
            s = qk[d] * jnp.exp(dm - m_i)
            sv = _mm1(s, v4[d])
            ssum = _gsum(s, bd)
            w_inter = jnp.exp(inter - m_i)
            num = w_inter * qc[d] + sv
            den = w_inter * qn[d] + ssum
            outs.append(num / jnp.maximum(jnp.abs(den), jnp.exp(-m_i)))

            m_new = jnp.maximum(b_last + m_prev, _seg_max(b_last - b_row + ig_row, lane_head))
            wk = k[d] * jnp.exp(b_last - b_col + ig_col - m_new)
            dec = jnp.exp(b_last + m_prev - m_new)
            kv = _mm1(wk, v[d], TN)
            c_s[d] = dec * c_s[d] + jnp.where(hm, kv, 0.0)
            n_s[d] = dec * n_s[d] + jnp.sum(wk, axis=0, keepdims=True)
            m_s[d] = m_new
        of_s[pl.ds(rows[0], CHUNK), :] = outs[0]
        ob_s[pl.ds(rows[1], CHUNK), :] = outs[1]
        return carry

    lax.fori_loop(0, n_chunks, body, 0)
    if not has_cache:
        for d in range(2):
            for h in range(N_HEADS):
                cf_ref[d, h] = c_s[d, _hs(h), _hs(h)]
        nf_ref[...] = n_s[...]
        mf_ref[...] = m_s[...]
    _epilogue(T, of_s, ob_s, og_ref, _sigmoid, bd_ref, ng_ref, o_ref)


def _mlstm(P, G, T, n_seq, gate_par, ng, bd, l, state, prev):
    has_cache = state is not None
    st = (2, N_HEADS, HEAD_DIM, HEAD_DIM)
    rw = (2, 1, SEG)
    in_specs = _seq_specs(T, (S_CQ, S_CK, S_CV, S_CO)) + [
        _gate_spec(T), _pick_spec((8, GATE_LANES), l), _const_spec((1, SEG)), _const_spec((SEG, SEG))]
    args = [P, P, P, P, G, gate_par, ng, bd]
    if has_cache:
        in_specs += [_state_in_spec(l, st), _state_in_spec(l, rw), _state_in_spec(l, rw)]
        args += list(state)
    scratch = [pltpu.VMEM((T, SEG), F32), pltpu.VMEM((T, SEG), F32), pltpu.VMEM((2, SEG, SEG), F32),
               pltpu.VMEM(rw, F32), pltpu.VMEM(rw, F32)]
    return _mixer_call(functools.partial(_mlstm_kernel, T=T, has_cache=has_cache), "mlstm", T, n_seq,
                       in_specs, args, scratch, () if has_cache else (st, rw, rw), l, prev)


INV_BASE = 8
SOLVE_GROUP = 4
_MM_INV = _mm1
_MM_APPLY = _mm2r


def _inverse_level_masks():
    ii = lax.broadcasted_iota(jnp.int32, (CHUNK, CHUNK), 0)
    jj = lax.broadcasted_iota(jnp.int32, (CHUNK, CHUNK), 1)
    out = []
    for lo, hi in ((jj, ii), (ii, jj)):
        lv = [(lo // INV_BASE == hi // INV_BASE) & (lo < hi)]
        size = 2 * INV_BASE
        while size <= CHUNK:
            lv.append((lo // size == hi // size) & (hi % size >= size // 2) & (lo % size < size // 2))
            size *= 2
        out.append(lv)
    return out


def _delta_kernel(*refs, T, has_cache):
    if has_cache:
        (q_ref, k_ref, v_ref, z_ref, gt_ref, gp_ref, cw_ref, ng_ref, bd_ref, s0_ref,
         o_ref, of_s, ob_s, st_s, qs, ks, vs, u_s, w_s, att_s, qg_s, kd_s, gl_s) = refs
    else:
        (q_ref, k_ref, v_ref, z_ref, gt_ref, gp_ref, cw_ref, ng_ref, bd_ref,
         o_ref, sf_ref, of_s, ob_s, st_s, qs, ks, vs, u_s, w_s, att_s, qg_s, kd_s, gl_s) = refs
    n_chunks = T // CHUNK
    group = min(SOLVE_GROUP, n_chunks)
    n_blk = T // ROW_BLOCK
    masks = _tri_masks()
    tri = (masks[0][0].astype(BF16), masks[1][0].astype(BF16))
    masks_t, eye_t = _tiled_masks()
    incl_t = [masks_t[d][0] for d in range(2)]
    strict_t = [masks_t[d][1] for d in range(2)]
    hm = _head_mask()
    lane_head = _lane_head()
    lvl_masks = _inverse_level_masks()
    eye64 = (masks[0][0] & masks[1][0]).astype(F32)
    bd = bd_ref[...]
    bias, a_log = gp_ref[0:1, :], gp_ref[1:2, :]
    row = lax.broadcasted_iota(jnp.int32, (ROW_BLOCK, 1), 0)

    def prologue(i, carry):
        r0 = pl.multiple_of(i * ROW_BLOCK, ROW_BLOCK)
        rp = pl.multiple_of(jnp.maximum(r0 - 8, 0), 8)
        rn = pl.multiple_of(jnp.minimum(r0 + ROW_BLOCK, T - 8), 8)
        for j, (src, dst) in enumerate(((q_ref, qs), (k_ref, ks), (v_ref, vs))):
            cur = src[pl.ds(r0, ROW_BLOCK), :]
            before = jnp.where(i > 0, src[pl.ds(rp, 8), :][7:8, :], 0.0)
            after = jnp.where(i < n_blk - 1, src[pl.ds(rn, 8), :][0:1, :], 0.0)
            down = jnp.where(row == 0, before, pltpu.roll(cur, 1, axis=0))
            up = jnp.where(row == ROW_BLOCK - 1, after, pltpu.roll(cur, ROW_BLOCK - 1, axis=0))
            w = cw_ref[:, j * SEG:(j + 1) * SEG]
            y = _silu(w[0:1, :] * down + w[1:2, :] * cur + w[2:3, :] * up)
            if j < 2:
                y = y * lax.rsqrt(_gsum(y * y, bd) + EPS)
            if j == 0:
                y = y * (HEAD_DIM ** -0.5)
            dst[pl.ds(r0, ROW_BLOCK), :] = y
        return carry

    lax.fori_loop(0, n_blk, prologue, 0)

    for d in range(2):
        st_s[d] = jnp.zeros((SEG, SEG), F32)
        if has_cache:
            for h in range(N_HEADS):
                st_s[d, _hs(h), _hs(h)] = s0_ref[d, h]

    def solve_group(g, carry):
        items = []
        q, k, v, beta, cg_col, g_last, decay, kb, k4 = ([] for _ in range(9))
        for cc in range(group):
            c = g * group + cc
            r0 = pl.multiple_of(c * CHUNK, CHUNK)
            qc, kc, vc = qs[pl.ds(r0, CHUNK), :], ks[pl.ds(r0, CHUNK), :], vs[pl.ds(r0, CHUNK), :]
            pre = gt_ref[pl.ds(r0, CHUNK), :]
            g_all = -jnp.exp(a_log) * _softplus(pre + bias)
            beta_all = _sigmoid(pre)
            kc4 = _block_diag(kc, hm)
            for d in range(2):
                items.append((d, r0, pl.multiple_of(c * 8, 8)))
                cg = _sel_mm(tri[d], g_all)
                col = _col_dense(cg, L_ALPHA + d * N_HEADS, lane_head)
                row_ = _diag_row(col, eye_t)
                bt = _col_dense(beta_all, L_BETA + d * N_HEADS, lane_head)
                q.append(qc), k.append(kc), v.append(vc), k4.append(kc4)
                beta.append(bt), cg_col.append(col), kb.append(kc * bt)
                g_last.append(col[CHUNK - 1:CHUNK, :] if d == 0 else col[0:1, :])
                decay.append(jnp.exp(jnp.where(incl_t[d], col - row_, NEG)))
        n_it = len(items)
        kk = [_mm1(kb[i], k4[i], NT) for i in range(n_it)]
        qk = [_mm1(q[i], k4[i], NT) for i in range(n_it)]
        a = [jnp.where(strict_t[items[i][0]], kk[i] * decay[i], 0.0) for i in range(n_it)]
        xu = [v[i] * beta[i] for i in range(n_it)]
        xw = [kb[i] * jnp.exp(cg_col[i]) for i in range(n_it)]

        units = [(i, h) for i in range(n_it) for h in range(N_HEADS)]
        dirs = [items[i][0] for i, h in units]
        ah = [a[i][:, _hs(h)] for i, h in units]
        x = [jnp.concatenate([xu[i][:, _hs(h)], xw[i][:, _hs(h)]], axis=1) for i, h in units]
        dg = [jnp.where(lvl_masks[d][0], a_, 0.0) for a_, d in zip(ah, dirs)]
        t = [eye64 - d_ for d_ in dg]
        p = [_MM_INV(d_, d_) for d_ in dg]
        pt = [_MM_INV(p_, t_) for p_, t_ in zip(p, t)]
        t = [t_ + u_ for t_, u_ in zip(t, pt)]
        p = [_MM_INV(p_, p_) for p_ in p]
        pt = [_MM_INV(p_, t_) for p_, t_ in zip(p, t)]
        t = [t_ + u_ for t_, u_ in zip(t, pt)]
        for lvl in range(1, len(lvl_masks[0])):
            lo = [jnp.where(lvl_masks[d][lvl], a_, 0.0) for a_, d in zip(ah, dirs)]
            lt = [_MM_INV(l_, t_) for l_, t_ in zip(lo, t)]
            tlt = [_MM_INV(t_, u_) for t_, u_ in zip(t, lt)]
            t = [t_ - u_ for t_, u_ in zip(t, tlt)]
        x = [_MM_APPLY(t_, x_) for t_, x_ in zip(t, x)]
        for i, (d, r0, r8) in enumerate(items):
            xi = x[i * N_HEADS:(i + 1) * N_HEADS]
            u_s[d, pl.ds(r0, CHUNK), :] = jnp.concatenate([x_[:, :HEAD_DIM] for x_ in xi], axis=1)
            w_s[d, pl.ds(r0, CHUNK), :] = jnp.concatenate([x_[:, HEAD_DIM:] for x_ in xi], axis=1).astype(BF16)
            att_s[d, pl.ds(r0, CHUNK), :] = (qk[i] * decay[i]).astype(BF16)
            qg_s[d, pl.ds(r0, CHUNK), :] = (q[i] * jnp.exp(cg_col[i])).astype(BF16)
            kd_s[d, pl.ds(r0, CHUNK), :] = (k[i] * jnp.exp(g_last[i] - cg_col[i])).astype(BF16)
            gl_s[d, pl.ds(r8, 8), :] = jnp.broadcast_to(jnp.exp(g_last[i]), (8, SEG))
        return carry

    lax.fori_loop(0, n_chunks // group, solve_group, 0)

    def scan(n, carry):
        D2 = range(2)
        rows = [_chunk_rows(d, n, n_chunks) for d in D2]
        rows8 = [pl.multiple_of((n if d == 0 else n_chunks - 1 - n) * 8, 8) for d in D2]
        s = [st_s[d] for d in D2]
        ws = [_mm1(w_s[d, pl.ds(rows[d], CHUNK), :], s[d]) for d in D2]
        qs_ = [_mm1(qg_s[d, pl.ds(rows[d], CHUNK), :], s[d]) for d in D2]
        v_new = [u_s[d, pl.ds(rows[d], CHUNK), :] - ws[d] for d in D2]
        v4 = [_block_diag(x_, hm) for x_ in v_new]
        av = [_mm1(att_s[d, pl.ds(rows[d], CHUNK), :], v4[d]) for d in D2]
        kv = [_mm1(kd_s[d, pl.ds(rows[d], CHUNK), :], v_new[d], TN) for d in D2]
        for d, o_s in enumerate((of_s, ob_s)):
            st_s[d] = s[d] * gl_s[d, pl.ds(rows8[d], 1), :] + jnp.where(hm, kv[d], 0.0)
            o_s[pl.ds(rows[d], CHUNK), :] = qs_[d] + av[d]
        return carry

    lax.fori_loop(0, n_chunks, scan, 0)
    if not has_cache:
        for d in range(2):
            for h in range(N_HEADS):
                sf_ref[d, h] = st_s[d, _hs(h), _hs(h)]
    _epilogue(T, of_s, ob_s, z_ref, _silu, bd_ref, ng_ref, o_ref)


def _deltanet(P, G, T, n_seq, gate_par, conv_w, ng, bd, l, state, prev):
    has_cache = state is not None
    st = (2, N_HEADS, HEAD_DIM, HEAD_DIM)
    in_specs = _seq_specs(T, (S_AQ, S_AK, S_AV, S_AZ)) + [
        _gate_spec(T), _pick_spec((8, GATE_LANES), l), _pick_spec((3, 3 * SEG), l),
        _const_spec((1, SEG)), _const_spec((SEG, SEG))]
    args = [P, P, P, P, G, gate_par, conv_w, ng, bd]
    if has_cache:
        in_specs.append(_state_in_spec(l, st))
        args.append(state)
    scratch = ([pltpu.VMEM((T, SEG), F32)] * 2 + [pltpu.VMEM((2, SEG, SEG), F32)]
               + [pltpu.VMEM((T, SEG), F32)] * 3
               + [pltpu.VMEM((2, T, SEG), F32)] + [pltpu.VMEM((2, T, SEG), BF16)] * 4
               + [pltpu.VMEM((2, T // CHUNK * 8, SEG), F32)])
    return _mixer_call(functools.partial(_delta_kernel, T=T, has_cache=has_cache), "deltanet", T, n_seq,
                       in_specs, args, scratch, () if has_cache else (st,), l, prev)


def _diff_kernel(*refs, T, has_cache, lam_init):
    if has_cache:
        (q_ref, k_ref, v_ref, qg_ref, kg_ref, lam_ref, ng_ref, bd32_ref, bd64_ref,
         cos_ref, sin_ref, ck_ref, cv_ref, o_ref, qs, kh, vh) = refs
    else:
        (q_ref, k_ref, v_ref, qg_ref, kg_ref, lam_ref, ng_ref, bd32_ref, bd64_ref,
         o_ref, kh, vh, qs) = refs
    n_blk = T // ROW_BLOCK
    bd32 = bd32_ref[...]
    lane = lax.broadcasted_iota(jnp.int32, (1, SEG), 1)
    first_half = (lane % 16) < 8

    def prologue(i, carry):
        r0 = pl.multiple_of(i * ROW_BLOCK, ROW_BLOCK)
        for src, g_ref in ((q_ref, qg_ref), (k_ref, kg_ref)):
            x = src[pl.ds(r0, ROW_BLOCK), :]
            y = x * lax.rsqrt(_gsum(x * x, bd32) * (1.0 / DQK) + EPS) * g_ref[...]
            if has_cache:
                partner = jnp.where(first_half, pltpu.roll(y, SEG - 8, axis=1), pltpu.roll(y, 8, axis=1))
                y = y * cos_ref[pl.ds(r0, ROW_BLOCK), :] + partner * sin_ref[pl.ds(r0, ROW_BLOCK), :]
            if src is q_ref:
                qs[pl.ds(r0, ROW_BLOCK), :] = y
            else:
                for h in range(N_HEADS):
                    kh[h, pl.ds(r0, ROW_BLOCK), :] = y[:, _hs(h)]
        xv = v_ref[pl.ds(r0, ROW_BLOCK), :]
        for h in range(N_HEADS):
            vh[h, pl.ds(r0, ROW_BLOCK), :] = xv[:, _hs(h)]
        return carry

    lax.fori_loop(0, n_blk, prologue, 0)

    lp = lam_ref[...]
    lam = (jnp.exp(jnp.sum(lp[0:1, :] * lp[1:2, :], axis=1, keepdims=True))
           - jnp.exp(jnp.sum(lp[2:3, :] * lp[3:4, :], axis=1, keepdims=True)) + lam_init)
    scale = DQK ** -0.5
    comp1 = lax.broadcasted_iota(jnp.int32, (1, HEAD_DIM), 1) < DQK
    bd64 = bd64_ref[...]
    ng = ng_ref[...]

    def softmax_times_v(qc, keys, vals):
        scores = [_mm1(qc, kk, NT) for kk in keys]
        m = scores[0].max(axis=1, keepdims=True)
        for s in scores[1:]:
            m = jnp.maximum(m, s.max(axis=1, keepdims=True))
        tot, acc = None, None
        for s, vv in zip(scores, vals):
            e = jnp.exp(s - m)
            t, a = e.sum(axis=1, keepdims=True), _mm1(e, vv)
            tot, acc = (t, a) if tot is None else (tot + t, acc + a)
        return acc * (1.0 / tot)

    def qblock(i, carry):
        r0 = pl.multiple_of(i * Q_BLOCK, Q_BLOCK)
        qb = qs[pl.ds(r0, Q_BLOCK), :] * scale
        outs = []
        for h in range(N_HEADS):
            qh = qb[:, _hs(h)]
            q1 = jnp.where(comp1, qh, 0.0)
            q2 = qh - q1
            keys = [kh[h]]
            vals = [vh[h]]
            if has_cache:
                keys.insert(0, ck_ref[h])
                vals.insert(0, cv_ref[h])
            outs.append(softmax_times_v(q1, keys, vals) - lam * softmax_times_v(q2, keys, vals))
        o = jnp.concatenate(outs, axis=1)
        o_ref[pl.ds(r0, Q_BLOCK), :] = _head_norm_gate(o, bd64, ng, 1.0 - lam_init)
        return carry

    lax.fori_loop(0, T // Q_BLOCK, qblock, 0)


def _diffattn(P, T, n_seq, qg, kg, lam_par, ng, bd32, bd64, l, rope, cache, prev):
    has_cache = cache is not None
    lam_init = 0.8 - 0.6 * math.exp(-0.3 * l)
    kv = (N_HEADS, T, HEAD_DIM)
    in_specs = _seq_specs(T, (S_BQ, S_BK, S_BV)) + [
        _const_spec((1, SEG)), _const_spec((1, SEG)), _pick_spec((4, DQK), l), _const_spec((1, SEG)),
        _const_spec((SEG, SEG)), _const_spec((SEG, SEG))]
    args = [P, P, P, qg, kg, lam_par, ng, bd32, bd64]
    if has_cache:
        ckv = cache[0].shape[2:]
        in_specs += [_const_spec((T, SEG)), _const_spec((T, SEG)), _state_in_spec(l, ckv), _state_in_spec(l, ckv)]
        args += [rope[0], rope[1], cache[0], cache[1]]
    scratch = [pltpu.VMEM((T, SEG), F32)] + ([pltpu.VMEM(kv, F32)] * 2 if has_cache else [])
    return _mixer_call(functools.partial(_diff_kernel, T=T, has_cache=has_cache, lam_init=lam_init),
                       "diff_attn", T, n_seq, in_specs, args, scratch, () if has_cache else (kv, kv), l, prev)


def _rope_tables(T):
    n_freq = DQK // 4
    t = jnp.arange(T)
    rows = (t // GRID_W).astype(F32)
    cols = (t % GRID_W).astype(F32)
    freqs = ROPE_BASE ** (-jnp.arange(n_freq, dtype=F32) / n_freq)
    ang_r, ang_c = rows[:, None] * freqs, cols[:, None] * freqs

    def comp(fn, sign):
        return jnp.concatenate([fn(ang_r), sign * fn(ang_r), fn(ang_c), sign * fn(ang_c)], axis=1)

    reps = SEG // DQK
    cos = jnp.tile(comp(jnp.cos, 1.0), (1, reps))
    sin = jnp.tile(jnp.concatenate([-jnp.sin(ang_r), jnp.sin(ang_r), -jnp.sin(ang_c), jnp.sin(ang_c)], axis=1),
                   (1, reps))
    return cos, sin


def _block_ones(group):
    i = np.arange(SEG)
    return jnp.asarray(i[:, None] // group == i[None, :] // group, BF16)


def kernel(x_prompt, x_sample, cache_diff_k, cache_diff_v, state_delta, state_mlstm_C, state_mlstm_n, state_mlstm_m, state_ret, c, c_ctx, w_ada, b_ada, norm_g, ffn_w_gate, ffn_w_up, ffn_w_down, w_in, dn_conv_w, dn_a_log, dn_dt_bias, dn_norm_g, da_qn_g, da_kn_g, da_lambda, da_norm_g, ml_i_bias, ml_f_bias, ml_norm_g, ret_decay_logit, ret_norm_g, w_branch, w_out):
    B, T, _ = x_prompt.shape
    Bs, Ts, _ = x_sample.shape
    cond8 = jnp.concatenate([c_ctx[None], c, jnp.zeros((8 - 1 - Bs, D_MODEL), F32)], axis=0)
    mod = _ada(cond8, w_ada, b_ada).reshape(DEPTH, 8, N_MOD, D_MODEL)
    bd32, bd64 = _block_ones(DQK), _block_ones(HEAD_DIM)
    rope = _rope_tables(Ts)
    tile_heads = lambda g: jnp.tile(g, SEG // g.shape[0])[None]

    wg, wu, wd = ffn_w_gate.astype(BF16), ffn_w_up.astype(BF16), ffn_w_down.astype(BF16)
    cols = lambda f: w_in[:, :, _IN_OFFS[f]:_IN_OFFS[f + 1]]
    w_mix = jnp.concatenate([cols(f) for f in _SEG_FIELDS], axis=2).astype(BF16)
    w_gate = jnp.concatenate([cols(f) for f in _GATE_FIELDS]
                             + [jnp.zeros((DEPTH, D_MODEL, GATE_LANES - 32), F32)], axis=2).astype(BF16)
    w_merge = cols(19).astype(BF16)
    wb, wo = w_branch.astype(BF16), w_out.astype(BF16)
    norm_g4 = norm_g.reshape(DEPTH, 3, 1, D_MODEL)
    lanes = lambda *parts: jnp.concatenate([p.reshape(DEPTH, 1, -1) for p in parts], axis=2)
    z8 = jnp.zeros((DEPTH, 8), F32)
    gate_par = jnp.concatenate([
        lanes(z8, dn_dt_bias, ml_i_bias, ml_f_bias, jnp.zeros((DEPTH, GATE_LANES - 32), F32)),
        lanes(z8, dn_a_log, jnp.zeros((DEPTH, GATE_LANES - 16), F32)),
        jnp.zeros((DEPTH, 6, GATE_LANES), F32)], axis=1)
    ml_state = (state_mlstm_C, state_mlstm_n.reshape(Bs, DEPTH, 2, 1, SEG),
                jnp.repeat(state_mlstm_m, HEAD_DIM, axis=-1).reshape(Bs, DEPTH, 2, 1, SEG))

    xs = {"ctx": x_prompt.reshape(B * T, D_MODEL), "smp": x_sample.reshape(Bs * Ts, D_MODEL)}
    geo = {"ctx": (T, B, B * T, 0), "smp": (Ts, Bs, Ts, 1)}
    states = {}
    for l in range(DEPTH):
        for path in ("ctx", "smp"):
            Tp, n_seq, rows_per_cond, first_cond = geo[path]
            mspec = _mod_spec(l, rows_per_cond, first_cond)
            smp = path == "smp"
            prev = states.get if not smp else (lambda name: None)
            x = _ffn(xs[path], mod, mspec, norm_g4, wg, wu, wd, l, 0)
            P, G = _inproj(x, mod, mspec, norm_g4, w_mix, w_gate, l)
            ra = _deltanet(P, G, Tp, n_seq, gate_par, dn_conv_w, tile_heads(dn_norm_g[l]), bd64, l,
                           state_delta if smp else None, prev("dn"))
            rb = _diffattn(P, Tp, n_seq, tile_heads(da_qn_g[l]), tile_heads(da_kn_g[l]), da_lambda,
                           tile_heads(da_norm_g[l]), bd32, bd64, l, rope if smp else None,
                           (cache_diff_k, cache_diff_v) if smp else None, prev("kv"))
            rc = _mlstm(P, G, Tp, n_seq, gate_par, tile_heads(ml_norm_g[l]), bd64, l,
                        ml_state if smp else None, prev("ml"))
            rd = _retention(P, Tp, n_seq, ret_decay_logit[l], tile_heads(ret_norm_g[l]), bd64, l,
                            state_ret if smp else None, prev("r"))
            if not smp:
                states = {"dn": ra[1:], "kv": rb[1:], "ml": rc[1:], "r": rd[1:]}
            x = _merge(x, mod, mspec, norm_g4, (ra[0], rb[0], rc[0], rd[0]), w_merge, wb, wo, l)
            xs[path] = _ffn(x, mod, mspec, norm_g4, wg, wu, wd, l, 1)
    (new_dn,), (new_k, new_v), (new_c, new_n, new_m), (new_r,) = (states[k] for k in ("dn", "kv", "ml", "r"))
    per_head = lambda rows: rows.reshape(B, DEPTH, 2, N_HEADS, HEAD_DIM)
    return (xs["ctx"].reshape(B, T, D_MODEL), xs["smp"].reshape(Bs, Ts, D_MODEL),
            new_k, new_v, new_dn, new_c, per_head(new_n), per_head(new_m)[..., 0], new_r)
```

```python
import functools
import math

import numpy as np
import jax
import jax.numpy as jnp
from jax import lax
from jax.experimental import pallas as pl
from jax.experimental.pallas import tpu as pltpu

F32 = jnp.float32
BF16 = jnp.bfloat16

D_MODEL = 1024
FFN_DIM = 2816
N_MOD = 9
DEPTH = 2
N_HEADS = 4
HEAD_DIM = 64
SEG = N_HEADS * HEAD_DIM
N_SEG = 15
CHUNK = 64
DQK = 32
GRID_W = 64
ROPE_BASE = 10000.0
EPS = 1e-6
N_BRANCH = 4
GATE_LANES = 128
NEG = -1e30

TM = 256
Q_BLOCK = 256
ROW_BLOCK = 128
VMEM_LIMIT = 56 * 1024 * 1024

NN = (((1,), (0,)), ((), ()))
NT = (((1,), (1,)), ((), ()))
TN = (((0,), (0,)), ((), ()))

_IN_SIZES = (256, 256, 256, 256, 8, 8, 256, 256, 256, 256, 256, 256, 256, 8, 8, 256, 256, 256, 256, 4096)
_IN_OFFS = np.concatenate([[0], np.cumsum(_IN_SIZES)]).tolist()
_SEG_FIELDS = (0, 1, 2, 3, 6, 7, 8, 9, 10, 11, 12, 15, 16, 17, 18)
_GATE_FIELDS = (4, 5, 13, 14)
(S_AQ, S_AK, S_AV, S_AZ, S_BQ, S_BK, S_BV, S_CQ, S_CK, S_CV, S_CO, S_DQ, S_DK, S_DV, S_DG) = range(N_SEG)
L_BETA, L_ALPHA, L_CI, L_CF = 0, 8, 16, 24


def _dg(a, b, dims):
    return lax.dot_general(a, b, dims, preferred_element_type=F32)


def _split2(x):
    hi = x.astype(BF16)
    lo = (x - hi.astype(F32)).astype(BF16)
    return hi, lo


def _mm1(a, b, dims=NN):
    return _dg(a.astype(BF16), b.astype(BF16), dims)


def _mm3(a, b, dims=NN):
    ah, al = _split2(a)
    bh, bl = _split2(b)
    return _dg(ah, bh, dims) + (_dg(ah, bl, dims) + _dg(al, bh, dims))


def _mm2r(a, b, dims=NN):
    ah = a.astype(BF16)
    bh, bl = _split2(b)
    return _dg(ah, bh, dims) + _dg(ah, bl, dims)


def _sel_mm(sel, x, dims=NN):
    h0 = x.astype(BF16)
    r1 = x - h0.astype(F32)
    h1 = r1.astype(BF16)
    h2 = (r1 - h1.astype(F32)).astype(BF16)
    return _dg(sel, h0, dims) + (_dg(sel, h1, dims) + _dg(sel, h2, dims))


def _gsum(x, bd):
    hi, lo = _split2(x)
    return _dg(hi, bd, NN) + _dg(lo, bd, NN)


def _sigmoid(x):
    return 1.0 / (1.0 + jnp.exp(-x))


def _silu(x):
    return x * _sigmoid(x)


def _softplus(x):
    return jnp.maximum(x, 0.0) + jnp.log1p(jnp.exp(-jnp.abs(x)))


def _log_sigmoid(x):
    return -_softplus(-x)


def _norm_mod(x, g, shift, scale):
    ms = jnp.mean(x * x, axis=-1, keepdims=True)
    return (x * lax.rsqrt(ms + EPS) * g) * (1.0 + scale) + shift


def _tri_masks():
    ii = lax.broadcasted_iota(jnp.int32, (CHUNK, CHUNK), 0)
    jj = lax.broadcasted_iota(jnp.int32, (CHUNK, CHUNK), 1)
    return ((jj <= ii, jj < ii), (jj >= ii, jj > ii))


def _cparams(n_grid=1):
    return pltpu.CompilerParams(dimension_semantics=("arbitrary",) * n_grid,
                                vmem_limit_bytes=VMEM_LIMIT)


def _const_spec(shape):
    nd = len(shape)
    return pl.BlockSpec(shape, lambda *_: (0,) * nd)


def _ada_kernel(s_ref, w_ref, b_ref, o_ref):
    s = s_ref[...]
    o_ref[...] = _mm3(_silu(s), w_ref[...]) + b_ref[...]


def _ada(cond8, w_ada, b_ada):
    tn = 1536
    n_t = (N_MOD * D_MODEL) // tn
    return pl.pallas_call(
        _ada_kernel,
        grid=(DEPTH, n_t),
        in_specs=[pl.BlockSpec((8, D_MODEL), lambda l, j: (0, 0)),
                  pl.BlockSpec((None, D_MODEL, tn), lambda l, j: (l, 0, j)),
                  pl.BlockSpec((None, 1, tn), lambda l, j: (l, 0, j))],
        out_specs=pl.BlockSpec((None, 8, tn), lambda l, j: (l, 0, j)),
        out_shape=jax.ShapeDtypeStruct((DEPTH, 8, N_MOD * D_MODEL), F32),
        compiler_params=_cparams(2),
        name="ada_mod",
    )(cond8, w_ada, b_ada.reshape(DEPTH, 1, N_MOD * D_MODEL))


def _ffn_kernel(x_ref, mod_ref, g_ref, wg_ref, wu_ref, wd_ref, o_ref, *, mi):
    x = x_ref[...]
    h = _norm_mod(x, g_ref[...], mod_ref[mi:mi + 1, :], mod_ref[mi + 1:mi + 2, :]).astype(BF16)
    gate = jnp.dot(h, wg_ref[...], preferred_element_type=F32)
    up = jnp.dot(h, wu_ref[...], preferred_element_type=F32)
    act = (_silu(gate) * up).astype(BF16)
    y = jnp.dot(act, wd_ref[...], preferred_element_type=F32)
    o_ref[...] = x + (0.5 * mod_ref[mi + 2:mi + 3, :]) * y


def _pick_spec(tail, *lead):
    nd = len(tail)
    return pl.BlockSpec((None,) * len(lead) + tuple(tail), lambda *_: tuple(lead) + (0,) * nd)


def _mod_spec(l, rows_per_cond, first_cond):
    per = rows_per_cond // TM
    return pl.BlockSpec((None, None, N_MOD, D_MODEL), lambda i: (l, first_cond + i // per, 0, 0))


def _ffn(x, mod, mod_spec, norm_g, wg, wu, wd, l, j):
    n = x.shape[0]
    return pl.pallas_call(
        functools.partial(_ffn_kernel, mi=6 * j),
        grid=(n // TM,),
        in_specs=[pl.BlockSpec((TM, D_MODEL), lambda i: (i, 0)),
                  mod_spec,
                  _pick_spec((1, D_MODEL), l, 2 * j),
                  _pick_spec((D_MODEL, FFN_DIM), l, j),
                  _pick_spec((D_MODEL, FFN_DIM), l, j),
                  _pick_spec((FFN_DIM, D_MODEL), l, j)],
        out_specs=pl.BlockSpec((TM, D_MODEL), lambda i: (i, 0)),
        out_shape=jax.ShapeDtypeStruct((n, D_MODEL), F32),
        compiler_params=_cparams(1),
        name="ffn",
    )(x, mod, norm_g, wg, wu, wd)


def _inproj_kernel(x_ref, mod_ref, g_ref, w_ref, wgt_ref, p_ref, gt_ref):
    h = _norm_mod(x_ref[...], g_ref[...], mod_ref[3:4, :], mod_ref[4:5, :]).astype(BF16)
    step = 3 * SEG
    for j in range(0, N_SEG * SEG, step):
        p_ref[:, j:j + step] = jnp.dot(h, w_ref[:, j:j + step], preferred_element_type=F32)
    gt_ref[...] = jnp.dot(h, wgt_ref[...], preferred_element_type=F32)


def _inproj(x, mod, mod_spec, norm_g, w_mix, w_gate, l):
    n = x.shape[0]
    return pl.pallas_call(
        _inproj_kernel,
        grid=(n // TM,),
        in_specs=[pl.BlockSpec((TM, D_MODEL), lambda i: (i, 0)),
                  mod_spec,
                  _pick_spec((1, D_MODEL), l, 1),
                  _pick_spec((D_MODEL, N_SEG * SEG), l),
                  _pick_spec((D_MODEL, GATE_LANES), l)],
        out_specs=[pl.BlockSpec((TM, N_SEG * SEG), lambda i: (i, 0)),
                   pl.BlockSpec((TM, GATE_LANES), lambda i: (i, 0))],
        out_shape=[jax.ShapeDtypeStruct((n, N_SEG * SEG), F32),
                   jax.ShapeDtypeStruct((n, GATE_LANES), F32)],
        compiler_params=_cparams(1),
        name="in_proj",
    )(x, mod, norm_g, w_mix, w_gate)


def _merge_kernel(x_ref, mod_ref, g_ref, ba_ref, bb_ref, bc_ref, bd_ref, wm_ref, wb_ref, wo_ref, o_ref):
    x = x_ref[...]
    h = _norm_mod(x, g_ref[...], mod_ref[3:4, :], mod_ref[4:5, :]).astype(BF16)
    mixed = None
    for m, b_ref in enumerate((ba_ref, bb_ref, bc_ref, bd_ref)):
        logits = jnp.dot(h, wm_ref[:, m * D_MODEL:(m + 1) * D_MODEL], preferred_element_type=F32)
        pb = jnp.dot(b_ref[...].astype(BF16), wb_ref[m], preferred_element_type=F32)
        term = _sigmoid(logits) * pb
        mixed = term if mixed is None else mixed + term
    y = jnp.dot(mixed.astype(BF16), wo_ref[...], preferred_element_type=F32)
    o_ref[...] = x + mod_ref[5:6, :] * y


def _merge(x, mod, mod_spec, norm_g, branches, w_merge, w_branch, w_out, l):
    n = x.shape[0]
    row = lambda w: pl.BlockSpec((TM, w), lambda i: (i, 0))
    return pl.pallas_call(
        _merge_kernel,
        grid=(n // TM,),
        in_specs=[row(D_MODEL), mod_spec, _pick_spec((1, D_MODEL), l, 1),
                  row(SEG), row(SEG), row(SEG), row(SEG),
                  _pick_spec((D_MODEL, N_BRANCH * D_MODEL), l),
                  _pick_spec((N_BRANCH, SEG, D_MODEL), l),
                  _pick_spec((D_MODEL, D_MODEL), l)],
        out_specs=row(D_MODEL),
        out_shape=jax.ShapeDtypeStruct((n, D_MODEL), F32),
        compiler_params=_cparams(1),
        name="merge",
    )(x, mod, norm_g, *branches, w_merge, w_branch, w_out)


def _seq_specs(T, segs):
    return [pl.BlockSpec((T, SEG), lambda s, j=j: (s, j)) for j in segs]


def _gate_spec(T):
    return pl.BlockSpec((T, GATE_LANES), lambda s: (s, 0))


def _state_in_spec(l, tail):
    nd = len(tail)
    return pl.BlockSpec((None, None) + tail, lambda s: (s, l) + (0,) * nd)


def _with_state_slabs(kernel, n_in, n_alias, n_state, l, creates, *refs):
    ins, rest = refs[:n_in], refs[n_in + n_alias:]
    main, states, scratch = rest[0], rest[1:1 + n_state], rest[1 + n_state:]
    if creates:
        for r in states:
            for other in range(DEPTH):
                if other != l:
                    r[other] = jnp.zeros(r.shape[1:], F32)
        states = [r.at[l] for r in states]
    return kernel(*ins, main, *states, *scratch)


def _mixer_call(kernel, name, T, n_seq, in_specs, args, scratch, state_tails=(), l=0, prev=None):
    creates = prev is None
    n_in, n_alias = len(args), 0 if creates else len(prev)

    def state_spec(t):
        if creates:
            return pl.BlockSpec((None, DEPTH) + t, lambda s: (s,) + (0,) * (1 + len(t)))
        return _state_in_spec(l, t)

    out_specs = [pl.BlockSpec((T, SEG), lambda s: (s, 0))] + [state_spec(t) for t in state_tails]
    out_shape = ([jax.ShapeDtypeStruct((n_seq * T, SEG), F32)]
                 + [jax.ShapeDtypeStruct((n_seq, DEPTH) + t, F32) for t in state_tails])
    aliases = {}
    if not creates:
        in_specs = in_specs + [pl.BlockSpec(memory_space=pl.ANY)] * n_alias
        args = args + list(prev)
        aliases = {n_in + i: 1 + i for i in range(n_alias)}
    body = functools.partial(_with_state_slabs, kernel, n_in, n_alias, len(state_tails), l, creates)
    return pl.pallas_call(
        body, grid=(n_seq,), in_specs=in_specs, out_specs=out_specs, out_shape=out_shape,
        scratch_shapes=scratch, input_output_aliases=aliases, compiler_params=_cparams(1), name=name)(*args)


def _head_norm_gate(o, bd, ng, gate):
    ss = _gsum(o * o, bd)
    return o * lax.rsqrt(ss * (1.0 / HEAD_DIM) + EPS) * ng * gate


def _epilogue(T, of_s, ob_s, gate_ref, gate_fn, bd_ref, ng_ref, o_ref):
    rb = 256
    bd = bd_ref[...]
    ng = ng_ref[...]

    def blk(i, carry):
        r0 = pl.multiple_of(i * rb, rb)
        o = of_s[pl.ds(r0, rb), :] + ob_s[pl.ds(r0, rb), :]
        o_ref[pl.ds(r0, rb), :] = _head_norm_gate(o, bd, ng, gate_fn(gate_ref[pl.ds(r0, rb), :]))
        return carry

    lax.fori_loop(0, T // rb, blk, 0)


def _chunk_rows(d, n, n_chunks):
    c = n if d == 0 else n_chunks - 1 - n
    return pl.multiple_of(c * CHUNK, CHUNK)


def _hs(h):
    return slice(h * HEAD_DIM, (h + 1) * HEAD_DIM)


def _head_mask():
    ii = lax.broadcasted_iota(jnp.int32, (SEG, SEG), 0) // HEAD_DIM
    jj = lax.broadcasted_iota(jnp.int32, (SEG, SEG), 1) // HEAD_DIM
    return ii == jj


def _blocksum(x):
    c = CHUNK
    return (x[0:c] + x[c:2 * c]) + (x[2 * c:3 * c] + x[3 * c:4 * c])


def _lane_head():
    return lax.broadcasted_iota(jnp.int32, (1, SEG), 1) // HEAD_DIM


def _tiled_masks():
    ii = lax.broadcasted_iota(jnp.int32, (CHUNK, SEG), 0)
    jj = lax.broadcasted_iota(jnp.int32, (CHUNK, SEG), 1) % HEAD_DIM
    return ((jj <= ii, jj < ii), (jj >= ii, jj > ii)), jj == ii


def _block_diag(x, hm):
    return jnp.where(hm, jnp.concatenate([x] * N_HEADS, axis=0), 0.0)


def _col_dense(g, lane0, lane_head):
    out = jnp.broadcast_to(g[:, lane0:lane0 + 1], (CHUNK, SEG))
    for h in range(1, N_HEADS):
        out = jnp.where(lane_head == h, g[:, lane0 + h:lane0 + h + 1], out)
    return out


def _diag_row(col_dense, eye_t):
    return jnp.sum(jnp.where(eye_t, col_dense, 0.0), axis=0, keepdims=True)


def _seg_max(x, lane_head):
    out = None
    for h in range(N_HEADS):
        m = jnp.max(jnp.where(lane_head == h, x, NEG), axis=1, keepdims=True)
        out = m if out is None else jnp.where(lane_head == h, m, out)
    return jnp.broadcast_to(out, x.shape)


UNITS = tuple((d, h) for d in range(2) for h in range(N_HEADS))


def _ret_kernel(*refs, T, has_cache):
    if has_cache:
        (q_ref, k_ref, v_ref, gt_ref, dl_ref, ng_ref, bd_ref, r0_ref, o_ref,
         of_s, ob_s, st_s, dec_s, qdec_s, kdec_s, cdec_s) = refs
    else:
        (q_ref, k_ref, v_ref, gt_ref, dl_ref, ng_ref, bd_ref, o_ref, rf_ref,
         of_s, ob_s, st_s, dec_s, qdec_s, kdec_s, cdec_s) = refs
    n_chunks = T // CHUNK
    masks = _tri_masks()
    hm = _head_mask()

    @pl.when(pl.program_id(0) == 0)
    def _():
        ii = lax.broadcasted_iota(jnp.int32, (CHUNK, CHUNK), 0)
        jj = lax.broadcasted_iota(jnp.int32, (CHUNK, CHUNK), 1)
        rel = (ii - jj).astype(F32)
        pos = lax.broadcasted_iota(jnp.int32, (CHUNK, 1), 0).astype(F32)
        lg_all = _log_sigmoid(dl_ref[...])
        for d in range(2):
            dec, qdec, kdec, cdec = [], [], [], []
            for h in range(N_HEADS):
                lg = lg_all[d:d + 1, h:h + 1]
                if d == 0:
                    e, qd, kd = rel * lg, (pos + 1.0) * lg, (CHUNK - 1.0 - pos) * lg
                else:
                    e, qd, kd = -rel * lg, (CHUNK - pos) * lg, pos * lg
                dec.append(jnp.exp(jnp.where(masks[d][0], e, NEG)))
                qdec.append(jnp.broadcast_to(jnp.exp(qd), (CHUNK, HEAD_DIM)))
                kdec.append(jnp.broadcast_to(jnp.exp(kd), (CHUNK, HEAD_DIM)))
                cdec.append(jnp.broadcast_to(jnp.exp(CHUNK * lg), (HEAD_DIM, SEG)))
            dec_s[d] = jnp.concatenate(dec, axis=0)
            qdec_s[d] = jnp.concatenate(qdec, axis=1)
            kdec_s[d] = jnp.concatenate(kdec, axis=1)
            cdec_s[d] = jnp.concatenate(cdec, axis=0)

    for d in range(2):
        st_s[d] = jnp.zeros((SEG, SEG), F32)
        if has_cache:
            for h in range(N_HEADS):
                st_s[d, _hs(h), _hs(h)] = r0_ref[d, h]

    def body(n, carry):
        rows = [_chunk_rows(d, n, n_chunks) for d in range(2)]
        q = [q_ref[pl.ds(r0, CHUNK), :] for r0 in rows]
        k = [k_ref[pl.ds(r0, CHUNK), :] * (HEAD_DIM ** -0.5) for r0 in rows]
        v = [v_ref[pl.ds(r0, CHUNK), :] for r0 in rows]
        q4 = [jnp.where(hm, jnp.concatenate([x] * N_HEADS, axis=0), 0.0) for x in q]
        qk = [_mm1(a, b, NT) for a, b in zip(q4, k)]
        qr = [_mm1(q[d] * qdec_s[d], st_s[d]) for d in range(2)]
        kv = [_mm1(k[d] * kdec_s[d], v[d], TN) for d in range(2)]
        av = [_mm1(qk[d] * dec_s[d], v[d]) for d in range(2)]
        for d, o_s in enumerate((of_s, ob_s)):
            st_s[d] = cdec_s[d] * st_s[d] + jnp.where(hm, kv[d], 0.0)
            o_s[pl.ds(rows[d], CHUNK), :] = qr[d] + _blocksum(jnp.where(hm, av[d], 0.0))
        return carry

    lax.fori_loop(0, n_chunks, body, 0)
    if not has_cache:
        for d in range(2):
            for h in range(N_HEADS):
                rf_ref[d, h] = st_s[d, _hs(h), _hs(h)]
    _epilogue(T, of_s, ob_s, gt_ref, _silu, bd_ref, ng_ref, o_ref)


def _retention(P, T, n_seq, decay_logit, ng, bd, l, state, prev):
    has_cache = state is not None
    st = (2, N_HEADS, HEAD_DIM, HEAD_DIM)
    in_specs = _seq_specs(T, (S_DQ, S_DK, S_DV, S_DG)) + [
        _const_spec((2, N_HEADS)), _const_spec((1, SEG)), _const_spec((SEG, SEG))]
    args = [P, P, P, P, decay_logit, ng, bd]
    if has_cache:
        in_specs.append(_state_in_spec(l, st))
        args.append(state)
    scratch = [pltpu.VMEM((T, SEG), F32), pltpu.VMEM((T, SEG), F32), pltpu.VMEM((2, SEG, SEG), F32),
               pltpu.VMEM((2, SEG, HEAD_DIM), F32), pltpu.VMEM((2, CHUNK, SEG), F32),
               pltpu.VMEM((2, CHUNK, SEG), F32), pltpu.VMEM((2, SEG, SEG), F32)]
    return _mixer_call(functools.partial(_ret_kernel, T=T, has_cache=has_cache), "retention", T, n_seq,
                       in_specs, args, scratch, () if has_cache else (st,), l, prev)


def _mlstm_kernel(*refs, T, has_cache):
    if has_cache:
        (q_ref, k_ref, v_ref, og_ref, gt_ref, gp_ref, ng_ref, bd_ref, c0_ref, n0_ref, m0_ref,
         o_ref, of_s, ob_s, c_s, n_s, m_s) = refs
    else:
        (q_ref, k_ref, v_ref, og_ref, gt_ref, gp_ref, ng_ref, bd_ref,
         o_ref, cf_ref, nf_ref, mf_ref, of_s, ob_s, c_s, n_s, m_s) = refs
    n_chunks = T // CHUNK
    masks = _tri_masks()
    tri = (masks[0][0].astype(BF16), masks[1][0].astype(BF16))
    masks_t, eye_t = _tiled_masks()
    tri_t = [masks_t[d][0] for d in range(2)]
    hm = _head_mask()
    lane_head = _lane_head()
    bd = bd_ref[...]
    bias = gp_ref[0:1, :]

    for d in range(2):
        c_s[d] = jnp.zeros((SEG, SEG), F32)
        if has_cache:
            for h in range(N_HEADS):
                c_s[d, _hs(h), _hs(h)] = c0_ref[d, h]
            n_s[d] = n0_ref[d]
            m_s[d] = m0_ref[d]
        else:
            n_s[d] = jnp.zeros((1, SEG), F32)
            m_s[d] = jnp.zeros((1, SEG), F32)

    def body(n, carry):
        rows = [_chunk_rows(d, n, n_chunks) for d in range(2)]
        q = [q_ref[pl.ds(r0, CHUNK), :] * (HEAD_DIM ** -0.5) for r0 in rows]
        k = [k_ref[pl.ds(r0, CHUNK), :] for r0 in rows]
        v = [v_ref[pl.ds(r0, CHUNK), :] for r0 in rows]
        pre = [gt_ref[pl.ds(r0, CHUNK), :] + bias for r0 in rows]
        b = [_sel_mm(tri[d], _log_sigmoid(pre[d])) for d in range(2)]

        k4 = [_block_diag(x, hm) for x in k]
        v4 = [_block_diag(x, hm) for x in v]
        qk = [_mm1(q[d], k4[d], NT) for d in range(2)]
        qc = [_mm1(q[d], c_s[d]) for d in range(2)]
        qn = [_gsum(q[d] * n_s[d], bd) for d in range(2)]

        outs = []
        for d in range(2):
            b_col = _col_dense(b[d], L_CF + d * N_HEADS, lane_head)
            ig_col = _col_dense(pre[d], L_CI + d * N_HEADS, lane_head)
            b_row, ig_row = _diag_row(b_col, eye_t), _diag_row(ig_col, eye_t)
            last = CHUNK - 1 if d == 0 else 0
            b_last = b_col[last:last + 1, :]
            m_prev = m_s[d]
            dm = jnp.where(tri_t[d], b_col - b_row + ig_row, NEG)
            inter = b_col + m_prev
            m_i = jnp.maximum(inter, _seg_max(dm, lane_head))
            s = qk[d] * jnp.exp(dm - m_i)
            sv = _mm1(s, v4[d])
            ssum = _gsum(s, bd)
            w_inter = jnp.exp(inter - m_i)
            num = w_inter * qc[d] + sv
            den = w_inter * qn[d] + ssum
            outs.append(num / jnp.maximum(jnp.abs(den), jnp.exp(-m_i)))

            m_new = jnp.maximum(b_last + m_prev, _seg_max(b_last - b_row + ig_row, lane_head))
            wk = k[d] * jnp.exp(b_last - b_col + ig_col - m_new)
            dec = jnp.exp(b_last + m_prev - m_new)
            kv = _mm1(wk, v[d], TN)
            c_s[d] = dec * c_s[d] + jnp.where(hm, kv, 0.0)
            n_s[d] = dec * n_s[d] + jnp.sum(wk, axis=0, keepdims=True)
            m_s[d] = m_new
        of_s[pl.ds(rows[0], CHUNK), :] = outs[0]
        ob_s[pl.ds(rows[1], CHUNK), :] = outs[1]
        return carry

    lax.fori_loop(0, n_chunks, body, 0)
    if not has_cache:
        for d in range(2):
            for h in range(N_HEADS):
                cf_ref[d, h] = c_s[d, _hs(h), _hs(h)]
        nf_ref[...] = n_s[...]
        mf_ref[...] = m_s[...]
    _epilogue(T, of_s, ob_s, og_ref, _sigmoid, bd_ref, ng_ref, o_ref)


def _mlstm(P, G, T, n_seq, gate_par, ng, bd, l, state, prev):
    has_cache = state is not None
    st = (2, N_HEADS, HEAD_DIM, HEAD_DIM)
    rw = (2, 1, SEG)
    in_specs = _seq_specs(T, (S_CQ, S_CK, S_CV, S_CO)) + [
        _gate_spec(T), _pick_spec((8, GATE_LANES), l), _const_spec((1, SEG)), _const_spec((SEG, SEG))]
    args = [P, P, P, P, G, gate_par, ng, bd]
    if has_cache:
        in_specs += [_state_in_spec(l, st), _state_in_spec(l, rw), _state_in_spec(l, rw)]
        args += list(state)
    scratch = [pltpu.VMEM((T, SEG), F32), pltpu.VMEM((T, SEG), F32), pltpu.VMEM((2, SEG, SEG), F32),
               pltpu.VMEM(rw, F32), pltpu.VMEM(rw, F32)]
    return _mixer_call(functools.partial(_mlstm_kernel, T=T, has_cache=has_cache), "mlstm", T, n_seq,
                       in_specs, args, scratch, () if has_cache else (st, rw, rw), l, prev)


INV_BASE = 8
SOLVE_GROUP = 4
_MM_INV = _mm1
_MM_APPLY = _mm2r


def _inverse_level_masks():
    ii = lax.broadcasted_iota(jnp.int32, (CHUNK, CHUNK), 0)
    jj = lax.broadcasted_iota(jnp.int32, (CHUNK, CHUNK), 1)
    out = []
    for lo, hi in ((jj, ii), (ii, jj)):
        lv = [(lo // INV_BASE == hi // INV_BASE) & (lo < hi)]
        size = 2 * INV_BASE
        while size <= CHUNK:
            lv.append((lo // size == hi // size) & (hi % size >= size // 2) & (lo % size < size // 2))
            size *= 2
        out.append(lv)
    return out


def _delta_kernel(*refs, T, has_cache):
    if has_cache:
        (q_ref, k_ref, v_ref, z_ref, gt_ref, gp_ref, cw_ref, ng_ref, bd_ref, s0_ref,
         o_ref, of_s, ob_s, st_s, qs, ks, vs, u_s, w_s, att_s, qg_s, kd_s, gl_s) = refs
    else:
        (q_ref, k_ref, v_ref, z_ref, gt_ref, gp_ref, cw_ref, ng_ref, bd_ref,
         o_ref, sf_ref, of_s, ob_s, st_s, qs, ks, vs, u_s, w_s, att_s, qg_s, kd_s, gl_s) = refs
    n_chunks = T // CHUNK
    group = min(SOLVE_GROUP, n_chunks)
    n_blk = T // ROW_BLOCK
    masks = _tri_masks()
    tri = (masks[0][0].astype(BF16), masks[1][0].astype(BF16))
    masks_t, eye_t = _tiled_masks()
    incl_t = [masks_t[d][0] for d in range(2)]
    strict_t = [masks_t[d][1] for d in range(2)]
    hm = _head_mask()
    lane_head = _lane_head()
    lvl_masks = _inverse_level_masks()
    eye64 = (masks[0][0] & masks[1][0]).astype(F32)
    bd = bd_ref[...]
    bias, a_log = gp_ref[0:1, :], gp_ref[1:2, :]
    row = lax.broadcasted_iota(jnp.int32, (ROW_BLOCK, 1), 0)

    def prologue(i, carry):
        r0 = pl.multiple_of(i * ROW_BLOCK, ROW_BLOCK)
        rp = pl.multiple_of(jnp.maximum(r0 - 8, 0), 8)
        rn = pl.multiple_of(jnp.minimum(r0 + ROW_BLOCK, T - 8), 8)
        for j, (src, dst) in enumerate(((q_ref, qs), (k_ref, ks), (v_ref, vs))):
            cur = src[pl.ds(r0, ROW_BLOCK), :]
            before = jnp.where(i > 0, src[pl.ds(rp, 8), :][7:8, :], 0.0)
            after = jnp.where(i < n_blk - 1, src[pl.ds(rn, 8), :][0:1, :], 0.0)
            down = jnp.where(row == 0, before, pltpu.roll(cur, 1, axis=0))
            up = jnp.where(row == ROW_BLOCK - 1, after, pltpu.roll(cur, ROW_BLOCK - 1, axis=0))
            w = cw_ref[:, j * SEG:(j + 1) * SEG]
            y = _silu(w[0:1, :] * down + w[1:2, :] * cur + w[2:3, :] * up)
            if j < 2:
                y = y * lax.rsqrt(_gsum(y * y, bd) + EPS)
            if j == 0:
                y = y * (HEAD_DIM ** -0.5)
            dst[pl.ds(r0, ROW_BLOCK), :] = y
        return carry

    lax.fori_loop(0, n_blk, prologue, 0)

    for d in range(2):
        st_s[d] = jnp.zeros((SEG, SEG), F32)
        if has_cache:
            for h in range(N_HEADS):
                st_s[d, _hs(h), _hs(h)] = s0_ref[d, h]

    def solve_group(g, carry):
        items = []
        q, k, v, beta, cg_col, g_last, decay, kb, k4 = ([] for _ in range(9))
        for cc in range(group):
            c = g * group + cc
            r0 = pl.multiple_of(c * CHUNK, CHUNK)
            qc, kc, vc = qs[pl.ds(r0, CHUNK), :], ks[pl.ds(r0, CHUNK), :], vs[pl.ds(r0, CHUNK), :]
            pre = gt_ref[pl.ds(r0, CHUNK), :]
            g_all = -jnp.exp(a_log) * _softplus(pre + bias)
            beta_all = _sigmoid(pre)
            kc4 = _block_diag(kc, hm)
            for d in range(2):
                items.append((d, r0, pl.multiple_of(c * 8, 8)))
                cg = _sel_mm(tri[d], g_all)
                col = _col_dense(cg, L_ALPHA + d * N_HEADS, lane_head)
                row_ = _diag_row(col, eye_t)
                bt = _col_dense(beta_all, L_BETA + d * N_HEADS, lane_head)
                q.append(qc), k.append(kc), v.append(vc), k4.append(kc4)
                beta.append(bt), cg_col.append(col), kb.append(kc * bt)
                g_last.append(col[CHUNK - 1:CHUNK, :] if d == 0 else col[0:1, :])
                decay.append(jnp.exp(jnp.where(incl_t[d], col - row_, NEG)))
        n_it = len(items)
        kk = [_mm1(kb[i], k4[i], NT) for i in range(n_it)]
        qk = [_mm1(q[i], k4[i], NT) for i in range(n_it)]
        a = [jnp.where(strict_t[items[i][0]], kk[i] * decay[i], 0.0) for i in range(n_it)]
        xu = [v[i] * beta[i] for i in range(n_it)]
        xw = [kb[i] * jnp.exp(cg_col[i]) for i in range(n_it)]

        units = [(i, h) for i in range(n_it) for h in range(N_HEADS)]
        dirs = [items[i][0] for i, h in units]
        ah = [a[i][:, _hs(h)] for i, h in units]
        x = [jnp.concatenate([xu[i][:, _hs(h)], xw[i][:, _hs(h)]], axis=1) for i, h in units]
        dg = [jnp.where(lvl_masks[d][0], a_, 0.0) for a_, d in zip(ah, dirs)]
        t = [eye64 - d_ for d_ in dg]
        p = [_MM_INV(d_, d_) for d_ in dg]
        pt = [_MM_INV(p_, t_) for p_, t_ in zip(p, t)]
        t = [t_ + u_ for t_, u_ in zip(t, pt)]
        p = [_MM_INV(p_, p_) for p_ in p]
        pt = [_MM_INV(p_, t_) for p_, t_ in zip(p, t)]
        t = [t_ + u_ for t_, u_ in zip(t, pt)]
        for lvl in range(1, len(lvl_masks[0])):
            lo = [jnp.where(lvl_masks[d][lvl], a_, 0.0) for a_, d in zip(ah, dirs)]
            lt = [_MM_INV(l_, t_) for l_, t_ in zip(lo, t)]
            tlt = [_MM_INV(t_, u_) for t_, u_ in zip(t, lt)]
            t = [t_ - u_ for t_, u_ in zip(t, tlt)]
        x = [_MM_APPLY(t_, x_) for t_, x_ in zip(t, x)]
        for i, (d, r0, r8) in enumerate(items):
            xi = x[i * N_HEADS:(i + 1) * N_HEADS]
            u_s[d, pl.ds(r0, CHUNK), :] = jnp.concatenate([x_[:, :HEAD_DIM] for x_ in xi], axis=1)
            w_s[d, pl.ds(r0, CHUNK), :] = jnp.concatenate([x_[:, HEAD_DIM:] for x_ in xi], axis=1).astype(BF16)
            att_s[d, pl.ds(r0, CHUNK), :] = (qk[i] * decay[i]).astype(BF16)
            qg_s[d, pl.ds(r0, CHUNK), :] = (q[i] * jnp.exp(cg_col[i])).astype(BF16)
            kd_s[d, pl.ds(r0, CHUNK), :] = (k[i] * jnp.exp(g_last[i] - cg_col[i])).astype(BF16)
            gl_s[d, pl.ds(r8, 8), :] = jnp.broadcast_to(jnp.exp(g_last[i]), (8, SEG))
        return carry

    lax.fori_loop(0, n_chunks // group, solve_group, 0)

    def scan(n, carry):
        D2 = range(2)
        rows = [_chunk_rows(d, n, n_chunks) for d in D2]
        rows8 = [pl.multiple_of((n if d == 0 else n_chunks - 1 - n) * 8, 8) for d in D2]
        s = [st_s[d] for d in D2]
        ws = [_mm1(w_s[d, pl.ds(rows[d], CHUNK), :], s[d]) for d in D2]
        qs_ = [_mm1(qg_s[d, pl.ds(rows[d], CHUNK), :], s[d]) for d in D2]
        v_new = [u_s[d, pl.ds(rows[d], CHUNK), :] - ws[d] for d in D2]
        v4 = [_block_diag(x_, hm) for x_ in v_new]
        av = [_mm1(att_s[d, pl.ds(rows[d], CHUNK), :], v4[d]) for d in D2]
        kv = [_mm1(kd_s[d, pl.ds(rows[d], CHUNK), :], v_new[d], TN) for d in D2]
        for d, o_s in enumerate((of_s, ob_s)):
            st_s[d] = s[d] * gl_s[d, pl.ds(rows8[d], 1), :] + jnp.where(hm, kv[d], 0.0)
            o_s[pl.ds(rows[d], CHUNK), :] = qs_[d] + av[d]
        return carry

    lax.fori_loop(0, n_chunks, scan, 0)
    if not has_cache:
        for d in range(2):
            for h in range(N_HEADS):
                sf_ref[d, h] = st_s[d, _hs(h), _hs(h)]
    _epilogue(T, of_s, ob_s, z_ref, _silu, bd_ref, ng_ref, o_ref)


def _deltanet(P, G, T, n_seq, gate_par, conv_w, ng, bd, l, state, prev):
    has_cache = state is not None
    st = (2, N_HEADS, HEAD_DIM, HEAD_DIM)
    in_specs = _seq_specs(T, (S_AQ, S_AK, S_AV, S_AZ)) + [
        _gate_spec(T), _pick_spec((8, GATE_LANES), l), _pick_spec((3, 3 * SEG), l),
        _const_spec((1, SEG)), _const_spec((SEG, SEG))]
    args = [P, P, P, P, G, gate_par, conv_w, ng, bd]
    if has_cache:
        in_specs.append(_state_in_spec(l, st))
        args.append(state)
    scratch = ([pltpu.VMEM((T, SEG), F32)] * 2 + [pltpu.VMEM((2, SEG, SEG), F32)]
               + [pltpu.VMEM((T, SEG), F32)] * 3
               + [pltpu.VMEM((2, T, SEG), F32)] + [pltpu.VMEM((2, T, SEG), BF16)] * 4
               + [pltpu.VMEM((2, T // CHUNK * 8, SEG), F32)])
    return _mixer_call(functools.partial(_delta_kernel, T=T, has_cache=has_cache), "deltanet", T, n_seq,
                       in_specs, args, scratch, () if has_cache else (st,), l, prev)


def _diff_kernel(*refs, T, has_cache, lam_init):
    if has_cache:
        (q_ref, k_ref, v_ref, qg_ref, kg_ref, lam_ref, ng_ref, bd32_ref, bd64_ref,
         cos_ref, sin_ref, ck_ref, cv_ref, o_ref, qs, kh, vt) = refs
    else:
        (q_ref, k_ref, v_ref, qg_ref, kg_ref, lam_ref, ng_ref, bd32_ref, bd64_ref,
         o_ref, ko_ref, vo_ref, qs, kh, vt) = refs
    n_blk = T // ROW_BLOCK
    s0 = ck_ref.shape[1] if has_cache else 0
    bd32 = bd32_ref[...]
    lane = lax.broadcasted_iota(jnp.int32, (1, SEG), 1)
    first_half = (lane % 16) < 8

    if has_cache:
        for h in range(N_HEADS):
            kh[h, 0:s0, :] = ck_ref[h]
            vt[h, :, 0:s0] = cv_ref[h].T

    for i in range(n_blk):
        r0 = i * ROW_BLOCK
        for src, g_ref in ((q_ref, qg_ref), (k_ref, kg_ref)):
            x = src[pl.ds(r0, ROW_BLOCK), :]
            y = x * lax.rsqrt(_gsum(x * x, bd32) * (1.0 / DQK) + EPS) * g_ref[...]
            if has_cache:
                partner = jnp.where(first_half, pltpu.roll(y, SEG - 8, axis=1), pltpu.roll(y, 8, axis=1))
                y = y * cos_ref[pl.ds(r0, ROW_BLOCK), :] + partner * sin_ref[pl.ds(r0, ROW_BLOCK), :]
            if src is q_ref:
                qs[pl.ds(r0, ROW_BLOCK), :] = y
            else:
                for h in range(N_HEADS):
                    kh[h, pl.ds(s0 + r0, ROW_BLOCK), :] = y[:, _hs(h)]
                    if not has_cache:
                        ko_ref[h, pl.ds(r0, ROW_BLOCK), :] = y[:, _hs(h)]
        xv = v_ref[pl.ds(r0, ROW_BLOCK), :]
        xvt = xv.T
        for h in range(N_HEADS):
            vt[h, :, pl.ds(s0 + r0, ROW_BLOCK)] = xvt[_hs(h), :]
            if not has_cache:
                vo_ref[h, pl.ds(r0, ROW_BLOCK), :] = xv[:, _hs(h)]

    lp = lam_ref[...]
    lam = (jnp.exp(jnp.sum(lp[0:1, :] * lp[1:2, :], axis=1, keepdims=True))
           - jnp.exp(jnp.sum(lp[2:3, :] * lp[3:4, :], axis=1, keepdims=True)) + lam_init)
    scale = DQK ** -0.5 * math.log2(math.e)
    comp_rows = lax.broadcasted_iota(jnp.int32, (HEAD_DIM, 1), 0) < DQK
    bd64 = bd64_ref[...]
    ng = ng_ref[...]

    def qblock(i, carry):
        r0 = pl.multiple_of(i * Q_BLOCK, Q_BLOCK)
        qt = (qs[pl.ds(r0, Q_BLOCK), :] * scale).T
        qct = []
        for h in range(N_HEADS):
            qh = qt[_hs(h), :]
            q1 = jnp.where(comp_rows, qh, 0.0)
            qct += [q1, qh - q1]
        n_units = len(qct)
        st = _mm1(kh[0], qct[0])
        res = []
        for u in range(n_units):
            nxt = _mm1(kh[(u + 1) // 2], qct[u + 1]) if u + 1 < n_units else None
            e = jnp.exp2(st - st.max(axis=0, keepdims=True))
            res.append(_mm1(vt[u // 2], e) * (1.0 / e.sum(axis=0, keepdims=True)))
            st = nxt
        outs = [res[2 * h] - lam * res[2 * h + 1] for h in range(N_HEADS)]
        o = jnp.concatenate(outs, axis=0).T
        o_ref[pl.ds(r0, Q_BLOCK), :] = _head_norm_gate(o, bd64, ng, 1.0 - lam_init)
        return carry

    lax.fori_loop(0, T // Q_BLOCK, qblock, 0)


def _diffattn(P, T, n_seq, qg, kg, lam_par, ng, bd32, bd64, l, rope, cache, prev):
    has_cache = cache is not None
    lam_init = 0.8 - 0.6 * math.exp(-0.3 * l)
    kv = (N_HEADS, T, HEAD_DIM)
    in_specs = _seq_specs(T, (S_BQ, S_BK, S_BV)) + [
        _const_spec((1, SEG)), _const_spec((1, SEG)), _pick_spec((4, DQK), l), _const_spec((1, SEG)),
        _const_spec((SEG, SEG)), _const_spec((SEG, SEG))]
    args = [P, P, P, qg, kg, lam_par, ng, bd32, bd64]
    if has_cache:
        ckv = cache[0].shape[2:]
        in_specs += [_const_spec((T, SEG)), _const_spec((T, SEG)), _state_in_spec(l, ckv), _state_in_spec(l, ckv)]
        args += [rope[0], rope[1], cache[0], cache[1]]
    n_keys = T + (cache[0].shape[3] if has_cache else 0)
    scratch = [pltpu.VMEM((T, SEG), F32), pltpu.VMEM((N_HEADS, n_keys, HEAD_DIM), F32),
               pltpu.VMEM((N_HEADS, HEAD_DIM, n_keys), F32)]
    return _mixer_call(functools.partial(_diff_kernel, T=T, has_cache=has_cache, lam_init=lam_init),
                       "diff_attn", T, n_seq, in_specs, args, scratch, () if has_cache else (kv, kv), l, prev)


def _rope_tables(T):
    n_freq = DQK // 4
    t = jnp.arange(T)
    rows = (t // GRID_W).astype(F32)
    cols = (t % GRID_W).astype(F32)
    freqs = ROPE_BASE ** (-jnp.arange(n_freq, dtype=F32) / n_freq)
    ang_r, ang_c = rows[:, None] * freqs, cols[:, None] * freqs

    def comp(fn, sign):
        return jnp.concatenate([fn(ang_r), sign * fn(ang_r), fn(ang_c), sign * fn(ang_c)], axis=1)

    reps = SEG // DQK
    cos = jnp.tile(comp(jnp.cos, 1.0), (1, reps))
    sin = jnp.tile(jnp.concatenate([-jnp.sin(ang_r), jnp.sin(ang_r), -jnp.sin(ang_c), jnp.sin(ang_c)], axis=1),
                   (1, reps))
    return cos, sin


def _block_ones(group):
    i = np.arange(SEG)
    return jnp.asarray(i[:, None] // group == i[None, :] // group, BF16)


def kernel(x_prompt, x_sample, cache_diff_k, cache_diff_v, state_delta, state_mlstm_C, state_mlstm_n, state_mlstm_m, state_ret, c, c_ctx, w_ada, b_ada, norm_g, ffn_w_gate, ffn_w_up, ffn_w_down, w_in, dn_conv_w, dn_a_log, dn_dt_bias, dn_norm_g, da_qn_g, da_kn_g, da_lambda, da_norm_g, ml_i_bias, ml_f_bias, ml_norm_g, ret_decay_logit, ret_norm_g, w_branch, w_out):
    B, T, _ = x_prompt.shape
    Bs, Ts, _ = x_sample.shape
    cond8 = jnp.concatenate([c_ctx[None], c, jnp.zeros((8 - 1 - Bs, D_MODEL), F32)], axis=0)
    mod = _ada(cond8, w_ada, b_ada).reshape(DEPTH, 8, N_MOD, D_MODEL)
    bd32, bd64 = _block_ones(DQK), _block_ones(HEAD_DIM)
    rope = _rope_tables(Ts)
    tile_heads = lambda g: jnp.tile(g, SEG // g.shape[0])[None]

    wg, wu, wd = ffn_w_gate.astype(BF16), ffn_w_up.astype(BF16), ffn_w_down.astype(BF16)
    cols = lambda f: w_in[:, :, _IN_OFFS[f]:_IN_OFFS[f + 1]]
    w_mix = jnp.concatenate([cols(f) for f in _SEG_FIELDS], axis=2).astype(BF16)
    w_gate = jnp.concatenate([cols(f) for f in _GATE_FIELDS]
                             + [jnp.zeros((DEPTH, D_MODEL, GATE_LANES - 32), F32)], axis=2).astype(BF16)
    w_merge = cols(19).astype(BF16)
    wb, wo = w_branch.astype(BF16), w_out.astype(BF16)
    norm_g4 = norm_g.reshape(DEPTH, 3, 1, D_MODEL)
    lanes = lambda *parts: jnp.concatenate([p.reshape(DEPTH, 1, -1) for p in parts], axis=2)
    z8 = jnp.zeros((DEPTH, 8), F32)
    gate_par = jnp.concatenate([
        lanes(z8, dn_dt_bias, ml_i_bias, ml_f_bias, jnp.zeros((DEPTH, GATE_LANES - 32), F32)),
        lanes(z8, dn_a_log, jnp.zeros((DEPTH, GATE_LANES - 16), F32)),
        jnp.zeros((DEPTH, 6, GATE_LANES), F32)], axis=1)
    ml_state = (state_mlstm_C, state_mlstm_n.reshape(Bs, DEPTH, 2, 1, SEG),
                jnp.repeat(state_mlstm_m, HEAD_DIM, axis=-1).reshape(Bs, DEPTH, 2, 1, SEG))

    xs = {"ctx": x_prompt.reshape(B * T, D_MODEL), "smp": x_sample.reshape(Bs * Ts, D_MODEL)}
    geo = {"ctx": (T, B, B * T, 0), "smp": (Ts, Bs, Ts, 1)}
    states = {}
    for l in range(DEPTH):
        for path in ("ctx", "smp"):
            Tp, n_seq, rows_per_cond, first_cond = geo[path]
            mspec = _mod_spec(l, rows_per_cond, first_cond)
            smp = path == "smp"
            prev = states.get if not smp else (lambda name: None)
            x = _ffn(xs[path], mod, mspec, norm_g4, wg, wu, wd, l, 0)
            P, G = _inproj(x, mod, mspec, norm_g4, w_mix, w_gate, l)
            ra = _deltanet(P, G, Tp, n_seq, gate_par, dn_conv_w, tile_heads(dn_norm_g[l]), bd64, l,
                           state_delta if smp else None, prev("dn"))
            rb = _diffattn(P, Tp, n_seq, tile_heads(da_qn_g[l]), tile_heads(da_kn_g[l]), da_lambda,
                           tile_heads(da_norm_g[l]), bd32, bd64, l, rope if smp else None,
                           (cache_diff_k, cache_diff_v) if smp else None, prev("kv"))
            rc = _mlstm(P, G, Tp, n_seq, gate_par, tile_heads(ml_norm_g[l]), bd64, l,
                        ml_state if smp else None, prev("ml"))
            rd = _retention(P, Tp, n_seq, ret_decay_logit[l], tile_heads(ret_norm_g[l]), bd64, l,
                            state_ret if smp else None, prev("r"))
            if not smp:
                states = {"dn": ra[1:], "kv": rb[1:], "ml": rc[1:], "r": rd[1:]}
            x = _merge(x, mod, mspec, norm_g4, (ra[0], rb[0], rc[0], rd[0]), w_merge, wb, wo, l)
            xs[path] = _ffn(x, mod, mspec, norm_g4, wg, wu, wd, l, 1)
    (new_dn,), (new_k, new_v), (new_c, new_n, new_m), (new_r,) = (states[k] for k in ("dn", "kv", "ml", "r"))
    per_head = lambda rows: rows.reshape(B, DEPTH, 2, N_HEADS, HEAD_DIM)
    return (xs["ctx"].reshape(B, T, D_MODEL), xs["smp"].reshape(Bs, Ts, D_MODEL),
            new_k, new_v, new_dn, new_c, per_head(new_n), per_head(new_m)[..., 0], new_r)
```

```python
import functools
import math

import numpy as np
import jax
import jax.numpy as jnp
from jax import lax
from jax.experimental import pallas as pl
from jax.experimental.pallas import tpu as pltpu

F32 = jnp.float32
BF16 = jnp.bfloat16

D_MODEL = 1024
FFN_DIM = 2816
N_MOD = 9
DEPTH = 2
N_HEADS = 4
HEAD_DIM = 64
SEG = N_HEADS * HEAD_DIM
N_SEG = 15
CHUNK = 64
DQK = 32
GRID_W = 64
ROPE_BASE = 10000.0
EPS = 1e-6
N_BRANCH = 4
GATE_LANES = 128
NEG = -1e30

TM = 512
Q_BLOCK = 256
ROW_BLOCK = 128
VMEM_LIMIT = 56 * 1024 * 1024

NN = (((1,), (0,)), ((), ()))
NT = (((1,), (1,)), ((), ()))
TN = (((0,), (0,)), ((), ()))

_IN_SIZES = (256, 256, 256, 256, 8, 8, 256, 256, 256, 256, 256, 256, 256, 8, 8, 256, 256, 256, 256, 4096)
_IN_OFFS = np.concatenate([[0], np.cumsum(_IN_SIZES)]).tolist()
_SEG_FIELDS = (0, 1, 2, 3, 6, 7, 8, 9, 10, 11, 12, 15, 16, 17, 18)
_GATE_FIELDS = (4, 5, 13, 14)
(S_AQ, S_AK, S_AV, S_AZ, S_BQ, S_BK, S_BV, S_CQ, S_CK, S_CV, S_CO, S_DQ, S_DK, S_DV, S_DG) = range(N_SEG)
L_BETA, L_ALPHA, L_CI, L_CF = 0, 8, 16, 24


def _dg(a, b, dims):
    return lax.dot_general(a, b, dims, preferred_element_type=F32)


def _split2(x):
    hi = x.astype(BF16)
    lo = (x - hi.astype(F32)).astype(BF16)
    return hi, lo


def _mm1(a, b, dims=NN):
    return _dg(a.astype(BF16), b.astype(BF16), dims)


def _mm3(a, b, dims=NN):
    ah, al = _split2(a)
    bh, bl = _split2(b)
    return _dg(ah, bh, dims) + (_dg(ah, bl, dims) + _dg(al, bh, dims))


def _mm2r(a, b, dims=NN):
    ah = a.astype(BF16)
    bh, bl = _split2(b)
    return _dg(ah, bh, dims) + _dg(ah, bl, dims)


def _sel_mm(sel, x, dims=NN):
    h0 = x.astype(BF16)
    r1 = x - h0.astype(F32)
    h1 = r1.astype(BF16)
    h2 = (r1 - h1.astype(F32)).astype(BF16)
    return _dg(sel, h0, dims) + (_dg(sel, h1, dims) + _dg(sel, h2, dims))


def _gsum(x, bd):
    hi, lo = _split2(x)
    return _dg(hi, bd, NN) + _dg(lo, bd, NN)


def _sigmoid(x):
    return 1.0 / (1.0 + jnp.exp(-x))


def _silu(x):
    return x * _sigmoid(x)


def _softplus(x):
    return jnp.maximum(x, 0.0) + jnp.log1p(jnp.exp(-jnp.abs(x)))


def _log_sigmoid(x):
    return -_softplus(-x)


def _norm_mod(x, g, shift, scale):
    ms = jnp.mean(x * x, axis=-1, keepdims=True)
    return (x * lax.rsqrt(ms + EPS) * g) * (1.0 + scale) + shift


def _tri_masks():
    ii = lax.broadcasted_iota(jnp.int32, (CHUNK, CHUNK), 0)
    jj = lax.broadcasted_iota(jnp.int32, (CHUNK, CHUNK), 1)
    return ((jj <= ii, jj < ii), (jj >= ii, jj > ii))


def _cparams(n_grid=1):
    return pltpu.CompilerParams(dimension_semantics=("arbitrary",) * n_grid,
                                vmem_limit_bytes=VMEM_LIMIT)


def _const_spec(shape):
    nd = len(shape)
    return pl.BlockSpec(shape, lambda *_: (0,) * nd)


def _ada_kernel(s_ref, w_ref, b_ref, o_ref):
    s = s_ref[...]
    o_ref[...] = _mm3(_silu(s), w_ref[...]) + b_ref[...]


def _ada(cond8, w_ada, b_ada):
    tn = 1536
    n_t = (N_MOD * D_MODEL) // tn
    return pl.pallas_call(
        _ada_kernel,
        grid=(DEPTH, n_t),
        in_specs=[pl.BlockSpec((8, D_MODEL), lambda l, j: (0, 0)),
                  pl.BlockSpec((None, D_MODEL, tn), lambda l, j: (l, 0, j)),
                  pl.BlockSpec((None, 1, tn), lambda l, j: (l, 0, j))],
        out_specs=pl.BlockSpec((None, 8, tn), lambda l, j: (l, 0, j)),
        out_shape=jax.ShapeDtypeStruct((DEPTH, 8, N_MOD * D_MODEL), F32),
        compiler_params=_cparams(2),
        name="ada_mod",
    )(cond8, w_ada, b_ada.reshape(DEPTH, 1, N_MOD * D_MODEL))


def _ffn_kernel(x_ref, mod_ref, g_ref, wg_ref, wu_ref, wd_ref, o_ref, *, mi):
    x = x_ref[...]
    h = _norm_mod(x, g_ref[...], mod_ref[mi:mi + 1, :], mod_ref[mi + 1:mi + 2, :]).astype(BF16)
    gate = jnp.dot(h, wg_ref[...], preferred_element_type=F32)
    up = jnp.dot(h, wu_ref[...], preferred_element_type=F32)
    act = (_silu(gate) * up).astype(BF16)
    y = jnp.dot(act, wd_ref[...], preferred_element_type=F32)
    o_ref[...] = x + (0.5 * mod_ref[mi + 2:mi + 3, :]) * y


def _pick_spec(tail, *lead):
    nd = len(tail)
    return pl.BlockSpec((None,) * len(lead) + tuple(tail), lambda *_: tuple(lead) + (0,) * nd,
                        pipeline_mode=pl.Buffered(1))


def _mod_spec(l, rows_per_cond, first_cond):
    per = rows_per_cond // TM
    return pl.BlockSpec((None, None, N_MOD, D_MODEL), lambda i: (l, first_cond + i // per, 0, 0))


def _ffn(x, mod, mod_spec, norm_g, wg, wu, wd, l, j):
    n = x.shape[0]
    return pl.pallas_call(
        functools.partial(_ffn_kernel, mi=6 * j),
        grid=(n // TM,),
        in_specs=[pl.BlockSpec((TM, D_MODEL), lambda i: (i, 0)),
                  mod_spec,
                  _pick_spec((1, D_MODEL), l, 2 * j),
                  _pick_spec((D_MODEL, FFN_DIM), l, j),
                  _pick_spec((D_MODEL, FFN_DIM), l, j),
                  _pick_spec((FFN_DIM, D_MODEL), l, j)],
        out_specs=pl.BlockSpec((TM, D_MODEL), lambda i: (i, 0)),
        out_shape=jax.ShapeDtypeStruct((n, D_MODEL), F32),
        compiler_params=_cparams(1),
        name="ffn",
    )(x, mod, norm_g, wg, wu, wd)


def _inproj_kernel(x_ref, mod_ref, g_ref, w_ref, wgt_ref, p_ref, gt_ref):
    h = _norm_mod(x_ref[...], g_ref[...], mod_ref[3:4, :], mod_ref[4:5, :]).astype(BF16)
    step = 3 * SEG
    for j in range(0, N_SEG * SEG, step):
        p_ref[:, j:j + step] = jnp.dot(h, w_ref[:, j:j + step], preferred_element_type=F32)
    gt_ref[...] = jnp.dot(h, wgt_ref[...], preferred_element_type=F32)


def _inproj(x, mod, mod_spec, norm_g, w_mix, w_gate, l):
    n = x.shape[0]
    return pl.pallas_call(
        _inproj_kernel,
        grid=(n // TM,),
        in_specs=[pl.BlockSpec((TM, D_MODEL), lambda i: (i, 0)),
                  mod_spec,
                  _pick_spec((1, D_MODEL), l, 1),
                  _pick_spec((D_MODEL, N_SEG * SEG), l),
                  _pick_spec((D_MODEL, GATE_LANES), l)],
        out_specs=[pl.BlockSpec((TM, N_SEG * SEG), lambda i: (i, 0)),
                   pl.BlockSpec((TM, GATE_LANES), lambda i: (i, 0))],
        out_shape=[jax.ShapeDtypeStruct((n, N_SEG * SEG), F32),
                   jax.ShapeDtypeStruct((n, GATE_LANES), F32)],
        compiler_params=_cparams(1),
        name="in_proj",
    )(x, mod, norm_g, w_mix, w_gate)


def _merge_kernel(x_ref, mod_ref, g_ref, ba_ref, bb_ref, bc_ref, bd_ref, wm_ref, wb_ref, wo_ref, o_ref):
    x = x_ref[...]
    h = _norm_mod(x, g_ref[...], mod_ref[3:4, :], mod_ref[4:5, :]).astype(BF16)
    mixed = None
    for m, b_ref in enumerate((ba_ref, bb_ref, bc_ref, bd_ref)):
        logits = jnp.dot(h, wm_ref[:, m * D_MODEL:(m + 1) * D_MODEL], preferred_element_type=F32)
        pb = jnp.dot(b_ref[...].astype(BF16), wb_ref[m], preferred_element_type=F32)
        term = _sigmoid(logits) * pb
        mixed = term if mixed is None else mixed + term
    y = jnp.dot(mixed.astype(BF16), wo_ref[...], preferred_element_type=F32)
    o_ref[...] = x + mod_ref[5:6, :] * y


def _merge(x, mod, mod_spec, norm_g, branches, w_merge, w_branch, w_out, l):
    n = x.shape[0]
    row = lambda w: pl.BlockSpec((TM, w), lambda i: (i, 0))
    return pl.pallas_call(
        _merge_kernel,
        grid=(n // TM,),
        in_specs=[row(D_MODEL), mod_spec, _pick_spec((1, D_MODEL), l, 1),
                  row(SEG), row(SEG), row(SEG), row(SEG),
                  _pick_spec((D_MODEL, N_BRANCH * D_MODEL), l),
                  _pick_spec((N_BRANCH, SEG, D_MODEL), l),
                  _pick_spec((D_MODEL, D_MODEL), l)],
        out_specs=row(D_MODEL),
        out_shape=jax.ShapeDtypeStruct((n, D_MODEL), F32),
        compiler_params=_cparams(1),
        name="merge",
    )(x, mod, norm_g, *branches, w_merge, w_branch, w_out)


def _seq_specs(T, segs):
    return [pl.BlockSpec((T, SEG), lambda s, j=j: (s, j)) for j in segs]


def _gate_spec(T):
    return pl.BlockSpec((T, GATE_LANES), lambda s: (s, 0))


def _state_in_spec(l, tail):
    nd = len(tail)
    return pl.BlockSpec((None, None) + tail, lambda s: (s, l) + (0,) * nd)


def _with_state_slabs(kernel, n_in, n_alias, n_state, l, creates, *refs):
    ins, rest = refs[:n_in], refs[n_in + n_alias:]
    main, states, scratch = rest[0], rest[1:1 + n_state], rest[1 + n_state:]
    if creates:
        for r in states:
            for other in range(DEPTH):
                if other != l:
                    r[other] = jnp.zeros(r.shape[1:], F32)
        states = [r.at[l] for r in states]
    return kernel(*ins, main, *states, *scratch)


def _mixer_call(kernel, name, T, n_seq, in_specs, args, scratch, state_tails=(), l=0, prev=None):
    creates = prev is None
    n_in, n_alias = len(args), 0 if creates else len(prev)

    def state_spec(t):
        if creates:
            return pl.BlockSpec((None, DEPTH) + t, lambda s: (s,) + (0,) * (1 + len(t)))
        return _state_in_spec(l, t)

    out_specs = [pl.BlockSpec((T, SEG), lambda s: (s, 0))] + [state_spec(t) for t in state_tails]
    out_shape = ([jax.ShapeDtypeStruct((n_seq * T, SEG), F32)]
                 + [jax.ShapeDtypeStruct((n_seq, DEPTH) + t, F32) for t in state_tails])
    aliases = {}
    if not creates:
        in_specs = in_specs + [pl.BlockSpec(memory_space=pl.ANY)] * n_alias
        args = args + list(prev)
        aliases = {n_in + i: 1 + i for i in range(n_alias)}
    body = functools.partial(_with_state_slabs, kernel, n_in, n_alias, len(state_tails), l, creates)
    return pl.pallas_call(
        body, grid=(n_seq,), in_specs=in_specs, out_specs=out_specs, out_shape=out_shape,
        scratch_shapes=scratch, input_output_aliases=aliases, compiler_params=_cparams(1), name=name)(*args)


def _head_norm_gate(o, bd, ng, gate):
    ss = _gsum(o * o, bd)
    return o * lax.rsqrt(ss * (1.0 / HEAD_DIM) + EPS) * ng * gate


def _epilogue(T, of_s, ob_s, gate_ref, gate_fn, bd_ref, ng_ref, o_ref):
    rb = 256
    bd = bd_ref[...]
    ng = ng_ref[...]

    def blk(i, carry):
        r0 = pl.multiple_of(i * rb, rb)
        o = of_s[pl.ds(r0, rb), :] + ob_s[pl.ds(r0, rb), :]
        o_ref[pl.ds(r0, rb), :] = _head_norm_gate(o, bd, ng, gate_fn(gate_ref[pl.ds(r0, rb), :]))
        return carry

    lax.fori_loop(0, T // rb, blk, 0)


def _chunk_rows(d, n, n_chunks):
    c = n if d == 0 else n_chunks - 1 - n
    return pl.multiple_of(c * CHUNK, CHUNK)


def _hs(h):
    return slice(h * HEAD_DIM, (h + 1) * HEAD_DIM)


def _head_mask():
    ii = lax.broadcasted_iota(jnp.int32, (SEG, SEG), 0) // HEAD_DIM
    jj = lax.broadcasted_iota(jnp.int32, (SEG, SEG), 1) // HEAD_DIM
    return ii == jj


def _blocksum(x):
    c = CHUNK
    return (x[0:c] + x[c:2 * c]) + (x[2 * c:3 * c] + x[3 * c:4 * c])


def _lane_head():
    return lax.broadcasted_iota(jnp.int32, (1, SEG), 1) // HEAD_DIM


def _tiled_masks():
    ii = lax.broadcasted_iota(jnp.int32, (CHUNK, SEG), 0)
    jj = lax.broadcasted_iota(jnp.int32, (CHUNK, SEG), 1) % HEAD_DIM
    return ((jj <= ii, jj < ii), (jj >= ii, jj > ii)), jj == ii


def _block_diag(x, hm):
    return jnp.where(hm, jnp.concatenate([x] * N_HEADS, axis=0), 0.0)


def _col_dense(g, lane0, lane_head):
    out = jnp.broadcast_to(g[:, lane0:lane0 + 1], (CHUNK, SEG))
    for h in range(1, N_HEADS):
        out = jnp.where(lane_head == h, g[:, lane0 + h:lane0 + h + 1], out)
    return out


def _diag_row(col_dense, eye_t):
    return jnp.sum(jnp.where(eye_t, col_dense, 0.0), axis=0, keepdims=True)


def _seg_max(x, lane_head):
    out = None
    for h in range(N_HEADS):
        m = jnp.max(jnp.where(lane_head == h, x, NEG), axis=1, keepdims=True)
        out = m if out is None else jnp.where(lane_head == h, m, out)
    return jnp.broadcast_to(out, x.shape)


UNITS = tuple((d, h) for d in range(2) for h in range(N_HEADS))


def _ret_kernel(*refs, T, has_cache):
    if has_cache:
        (q_ref, k_ref, v_ref, gt_ref, dl_ref, ng_ref, bd_ref, r0_ref, o_ref,
         of_s, ob_s, st_s, dec_s, qdec_s, kdec_s, cdec_s) = refs
    else:
        (q_ref, k_ref, v_ref, gt_ref, dl_ref, ng_ref, bd_ref, o_ref, rf_ref,
         of_s, ob_s, st_s, dec_s, qdec_s, kdec_s, cdec_s) = refs
    n_chunks = T // CHUNK
    masks = _tri_masks()
    hm = _head_mask()

    @pl.when(pl.program_id(0) == 0)
    def _():
        ii = lax.broadcasted_iota(jnp.int32, (CHUNK, CHUNK), 0)
        jj = lax.broadcasted_iota(jnp.int32, (CHUNK, CHUNK), 1)
        rel = (ii - jj).astype(F32)
        pos = lax.broadcasted_iota(jnp.int32, (CHUNK, 1), 0).astype(F32)
        lg_all = _log_sigmoid(dl_ref[...])
        for d in range(2):
            dec, qdec, kdec, cdec = [], [], [], []
            for h in range(N_HEADS):
                lg = lg_all[d:d + 1, h:h + 1]
                if d == 0:
                    e, qd, kd = rel * lg, (pos + 1.0) * lg, (CHUNK - 1.0 - pos) * lg
                else:
                    e, qd, kd = -rel * lg, (CHUNK - pos) * lg, pos * lg
                dec.append(jnp.exp(jnp.where(masks[d][0], e, NEG)))
                qdec.append(jnp.broadcast_to(jnp.exp(qd), (CHUNK, HEAD_DIM)))
                kdec.append(jnp.broadcast_to(jnp.exp(kd), (CHUNK, HEAD_DIM)))
                cdec.append(jnp.broadcast_to(jnp.exp(CHUNK * lg), (HEAD_DIM, SEG)))
            dec_s[d] = jnp.concatenate(dec, axis=0)
            qdec_s[d] = jnp.concatenate(qdec, axis=1)
            kdec_s[d] = jnp.concatenate(kdec, axis=1)
            cdec_s[d] = jnp.concatenate(cdec, axis=0)

    for d in range(2):
        st_s[d] = jnp.zeros((SEG, SEG), F32)
        if has_cache:
            for h in range(N_HEADS):
                st_s[d, _hs(h), _hs(h)] = r0_ref[d, h]

    def body(n, carry):
        rows = [_chunk_rows(d, n, n_chunks) for d in range(2)]
        q = [q_ref[pl.ds(r0, CHUNK), :] for r0 in rows]
        k = [k_ref[pl.ds(r0, CHUNK), :] * (HEAD_DIM ** -0.5) for r0 in rows]
        v = [v_ref[pl.ds(r0, CHUNK), :] for r0 in rows]
        q4 = [jnp.where(hm, jnp.concatenate([x] * N_HEADS, axis=0), 0.0) for x in q]
        qk = [_mm1(a, b, NT) for a, b in zip(q4, k)]
        qr = [_mm1(q[d] * qdec_s[d], st_s[d]) for d in range(2)]
        kv = [_mm1(k[d] * kdec_s[d], v[d], TN) for d in range(2)]
        av = [_mm1(qk[d] * dec_s[d], v[d]) for d in range(2)]
        for d, o_s in enumerate((of_s, ob_s)):
            st_s[d] = cdec_s[d] * st_s[d] + jnp.where(hm, kv[d], 0.0)
            o_s[pl.ds(rows[d], CHUNK), :] = qr[d] + _blocksum(jnp.where(hm, av[d], 0.0))
        return carry

    lax.fori_loop(0, n_chunks, body, 0)
    if not has_cache:
        for d in range(2):
            for h in range(N_HEADS):
                rf_ref[d, h] = st_s[d, _hs(h), _hs(h)]
    _epilogue(T, of_s, ob_s, gt_ref, _silu, bd_ref, ng_ref, o_ref)


def _retention(P, T, n_seq, decay_logit, ng, bd, l, state, prev):
    has_cache = state is not None
    st = (2, N_HEADS, HEAD_DIM, HEAD_DIM)
    in_specs = _seq_specs(T, (S_DQ, S_DK, S_DV, S_DG)) + [
        _const_spec((2, N_HEADS)), _const_spec((1, SEG)), _const_spec((SEG, SEG))]
    args = [P, P, P, P, decay_logit, ng, bd]
    if has_cache:
        in_specs.append(_state_in_spec(l, st))
        args.append(state)
    scratch = [pltpu.VMEM((T, SEG), F32), pltpu.VMEM((T, SEG), F32), pltpu.VMEM((2, SEG, SEG), F32),
               pltpu.VMEM((2, SEG, HEAD_DIM), F32), pltpu.VMEM((2, CHUNK, SEG), F32),
               pltpu.VMEM((2, CHUNK, SEG), F32), pltpu.VMEM((2, SEG, SEG), F32)]
    return _mixer_call(functools.partial(_ret_kernel, T=T, has_cache=has_cache), "retention", T, n_seq,
                       in_specs, args, scratch, () if has_cache else (st,), l, prev)


def _mlstm_kernel(*refs, T, has_cache):
    if has_cache:
        (q_ref, k_ref, v_ref, og_ref, gt_ref, gp_ref, ng_ref, bd_ref, c0_ref, n0_ref, m0_ref,
         o_ref, of_s, ob_s, c_s, n_s, m_s) = refs
    else:
        (q_ref, k_ref, v_ref, og_ref, gt_ref, gp_ref, ng_ref, bd_ref,
         o_ref, cf_ref, nf_ref, mf_ref, of_s, ob_s, c_s, n_s, m_s) = refs
    n_chunks = T // CHUNK
    masks = _tri_masks()
    tri = (masks[0][0].astype(BF16), masks[1][0].astype(BF16))
    masks_t, eye_t = _tiled_masks()
    tri_t = [masks_t[d][0] for d in range(2)]
    hm = _head_mask()
    lane_head = _lane_head()
    bd = bd_ref[...]
    bias = gp_ref[0:1, :]

    for d in range(2):
        c_s[d] = jnp.zeros((SEG, SEG), F32)
        if has_cache:
            for h in range(N_HEADS):
                c_s[d, _hs(h), _hs(h)] = c0_ref[d, h]
            n_s[d] = n0_ref[d]
            m_s[d] = m0_ref[d]
        else:
            n_s[d] = jnp.zeros((1, SEG), F32)
            m_s[d] = jnp.zeros((1, SEG), F32)

    def body(n, carry):
        rows = [_chunk_rows(d, n, n_chunks) for d in range(2)]
        q = [q_ref[pl.ds(r0, CHUNK), :] * (HEAD_DIM ** -0.5) for r0 in rows]
        k = [k_ref[pl.ds(r0, CHUNK), :] for r0 in rows]
        v = [v_ref[pl.ds(r0, CHUNK), :] for r0 in rows]
        pre = [gt_ref[pl.ds(r0, CHUNK), :] + bias for r0 in rows]
        b = [_sel_mm(tri[d], _log_sigmoid(pre[d])) for d in range(2)]

        k4 = [_block_diag(x, hm) for x in k]
        v4 = [_block_diag(x, hm) for x in v]
        qk = [_mm1(q[d], k4[d], NT) for d in range(2)]
        qc = [_mm1(q[d], c_s[d]) for d in range(2)]
        qn = [_gsum(q[d] * n_s[d], bd) for d in range(2)]

        outs = []
        for d in range(2):
            b_col = _col_dense(b[d], L_CF + d * N_HEADS, lane_head)
            ig_col = _col_dense(pre[d], L_CI + d * N_HEADS, lane_head)
            b_row, ig_row = _diag_row(b_col, eye_t), _diag_row(ig_col, eye_t)
            last = CHUNK - 1 if d == 0 else 0
            b_last = b_col[last:last + 1, :]
            m_prev = m_s[d]
            dm = jnp.where(tri_t[d], b_col - b_row + ig_row, NEG)
            inter = b_col + m_prev
            m_i = jnp.maximum(inter, _seg_max(dm, lane_head))
            s = qk[d] * jnp.exp(dm - m_i)
            sv = _mm1(s, v4[d])
            ssum = _gsum(s, bd)
            w_inter = jnp.exp(inter - m_i)
            num = w_inter * qc[d] + sv
            den = w_inter * qn[d] + ssum
            outs.append(num / jnp.maximum(jnp.abs(den), jnp.exp(-m_i)))

            m_new = jnp.maximum(b_last + m_prev, _seg_max(b_last - b_row + ig_row, lane_head))
            wk = k[d] * jnp.exp(b_last - b_col + ig_col - m_new)
            dec = jnp.exp(b_last + m_prev - m_new)
            kv = _mm1(wk, v[d], TN)
            c_s[d] = dec * c_s[d] + jnp.where(hm, kv, 0.0)
            n_s[d] = dec * n_s[d] + jnp.sum(wk, axis=0, keepdims=True)
            m_s[d] = m_new
        of_s[pl.ds(rows[0], CHUNK), :] = outs[0]
        ob_s[pl.ds(rows[1], CHUNK), :] = outs[1]
        return carry

    lax.fori_loop(0, n_chunks, body, 0)
    if not has_cache:
        for d in range(2):
            for h in range(N_HEADS):
                cf_ref[d, h] = c_s[d, _hs(h), _hs(h)]
        nf_ref[...] = n_s[...]
        mf_ref[...] = m_s[...]
    _epilogue(T, of_s, ob_s, og_ref, _sigmoid, bd_ref, ng_ref, o_ref)


def _mlstm(P, G, T, n_seq, gate_par, ng, bd, l, state, prev):
    has_cache = state is not None
    st = (2, N_HEADS, HEAD_DIM, HEAD_DIM)
    rw = (2, 1, SEG)
    in_specs = _seq_specs(T, (S_CQ, S_CK, S_CV, S_CO)) + [
        _gate_spec(T), _pick_spec((8, GATE_LANES), l), _const_spec((1, SEG)), _const_spec((SEG, SEG))]
    args = [P, P, P, P, G, gate_par, ng, bd]
    if has_cache:
        in_specs += [_state_in_spec(l, st), _state_in_spec(l, rw), _state_in_spec(l, rw)]
        args += list(state)
    scratch = [pltpu.VMEM((T, SEG), F32), pltpu.VMEM((T, SEG), F32), pltpu.VMEM((2, SEG, SEG), F32),
               pltpu.VMEM(rw, F32), pltpu.VMEM(rw, F32)]
    return _mixer_call(functools.partial(_mlstm_kernel, T=T, has_cache=has_cache), "mlstm", T, n_seq,
                       in_specs, args, scratch, () if has_cache else (st, rw, rw), l, prev)


INV_BASE = 8
SOLVE_GROUP = 4
_MM_INV = _mm1
_MM_APPLY = _mm2r


def _inverse_level_masks():
    ii = lax.broadcasted_iota(jnp.int32, (CHUNK, CHUNK), 0)
    jj = lax.broadcasted_iota(jnp.int32, (CHUNK, CHUNK), 1)
    out = []
    for lo, hi in ((jj, ii), (ii, jj)):
        lv = [(lo // INV_BASE == hi // INV_BASE) & (lo < hi)]
        size = 2 * INV_BASE
        while size <= CHUNK:
            lv.append((lo // size == hi // size) & (hi % size >= size // 2) & (lo % size < size // 2))
            size *= 2
        out.append(lv)
    return out


def _delta_kernel(*refs, T, has_cache):
    if has_cache:
        (q_ref, k_ref, v_ref, z_ref, gt_ref, gp_ref, cw_ref, ng_ref, bd_ref, s0_ref,
         o_ref, of_s, ob_s, st_s, qs, ks, vs, u_s, w_s, att_s, qg_s, kd_s, gl_s) = refs
    else:
        (q_ref, k_ref, v_ref, z_ref, gt_ref, gp_ref, cw_ref, ng_ref, bd_ref,
         o_ref, sf_ref, of_s, ob_s, st_s, qs, ks, vs, u_s, w_s, att_s, qg_s, kd_s, gl_s) = refs
    n_chunks = T // CHUNK
    group = min(SOLVE_GROUP, n_chunks)
    n_blk = T // ROW_BLOCK
    masks = _tri_masks()
    tri = (masks[0][0].astype(BF16), masks[1][0].astype(BF16))
    masks_t, eye_t = _tiled_masks()
    incl_t = [masks_t[d][0] for d in range(2)]
    strict_t = [masks_t[d][1] for d in range(2)]
    hm = _head_mask()
    lane_head = _lane_head()
    lvl_masks = _inverse_level_masks()
    eye64 = (masks[0][0] & masks[1][0]).astype(F32)
    bd = bd_ref[...]
    bias, a_log = gp_ref[0:1, :], gp_ref[1:2, :]
    row = lax.broadcasted_iota(jnp.int32, (ROW_BLOCK, 1), 0)

    def prologue(i, carry):
        r0 = pl.multiple_of(i * ROW_BLOCK, ROW_BLOCK)
        rp = pl.multiple_of(jnp.maximum(r0 - 8, 0), 8)
        rn = pl.multiple_of(jnp.minimum(r0 + ROW_BLOCK, T - 8), 8)
        for j, (src, dst) in enumerate(((q_ref, qs), (k_ref, ks), (v_ref, vs))):
            cur = src[pl.ds(r0, ROW_BLOCK), :]
            before = jnp.where(i > 0, src[pl.ds(rp, 8), :][7:8, :], 0.0)
            after = jnp.where(i < n_blk - 1, src[pl.ds(rn, 8), :][0:1, :], 0.0)
            down = jnp.where(row == 0, before, pltpu.roll(cur, 1, axis=0))
            up = jnp.where(row == ROW_BLOCK - 1, after, pltpu.roll(cur, ROW_BLOCK - 1, axis=0))
            w = cw_ref[:, j * SEG:(j + 1) * SEG]
            y = _silu(w[0:1, :] * down + w[1:2, :] * cur + w[2:3, :] * up)
            if j < 2:
                y = y * lax.rsqrt(_gsum(y * y, bd) + EPS)
            if j == 0:
                y = y * (HEAD_DIM ** -0.5)
            dst[pl.ds(r0, ROW_BLOCK), :] = y
        return carry

    lax.fori_loop(0, n_blk, prologue, 0)

    for d in range(2):
        st_s[d] = jnp.zeros((SEG, SEG), F32)
        if has_cache:
            for h in range(N_HEADS):
                st_s[d, _hs(h), _hs(h)] = s0_ref[d, h]

    def solve_group(g, carry):
        items = []
        q, k, v, beta, cg_col, g_last, decay, kb, k4 = ([] for _ in range(9))
        for cc in range(group):
            c = g * group + cc
            r0 = pl.multiple_of(c * CHUNK, CHUNK)
            qc, kc, vc = qs[pl.ds(r0, CHUNK), :], ks[pl.ds(r0, CHUNK), :], vs[pl.ds(r0, CHUNK), :]
            pre = gt_ref[pl.ds(r0, CHUNK), :]
            g_all = -jnp.exp(a_log) * _softplus(pre + bias)
            beta_all = _sigmoid(pre)
            kc4 = _block_diag(kc, hm)
            for d in range(2):
                items.append((d, r0, pl.multiple_of(c * 8, 8)))
                cg = _sel_mm(tri[d], g_all)
                col = _col_dense(cg, L_ALPHA + d * N_HEADS, lane_head)
                row_ = _diag_row(col, eye_t)
                bt = _col_dense(beta_all, L_BETA + d * N_HEADS, lane_head)
                q.append(qc), k.append(kc), v.append(vc), k4.append(kc4)
                beta.append(bt), cg_col.append(col), kb.append(kc * bt)
                g_last.append(col[CHUNK - 1:CHUNK, :] if d == 0 else col[0:1, :])
                decay.append(jnp.exp(jnp.where(incl_t[d], col - row_, NEG)))
        n_it = len(items)
        kk = [_mm1(kb[i], k4[i], NT) for i in range(n_it)]
        qk = [_mm1(q[i], k4[i], NT) for i in range(n_it)]
        a = [jnp.where(strict_t[items[i][0]], kk[i] * decay[i], 0.0) for i in range(n_it)]
        xu = [v[i] * beta[i] for i in range(n_it)]
        xw = [kb[i] * jnp.exp(cg_col[i]) for i in range(n_it)]

        units = [(i, h) for i in range(n_it) for h in range(N_HEADS)]
        dirs = [items[i][0] for i, h in units]
        ah = [a[i][:, _hs(h)] for i, h in units]
        x = [jnp.concatenate([xu[i][:, _hs(h)], xw[i][:, _hs(h)]], axis=1) for i, h in units]
        dg = [jnp.where(lvl_masks[d][0], a_, 0.0) for a_, d in zip(ah, dirs)]
        t = [eye64 - d_ for d_ in dg]
        p = [_MM_INV(d_, d_) for d_ in dg]
        pt = [_MM_INV(p_, t_) for p_, t_ in zip(p, t)]
        t = [t_ + u_ for t_, u_ in zip(t, pt)]
        p = [_MM_INV(p_, p_) for p_ in p]
        pt = [_MM_INV(p_, t_) for p_, t_ in zip(p, t)]
        t = [t_ + u_ for t_, u_ in zip(t, pt)]
        for lvl in range(1, len(lvl_masks[0])):
            lo = [jnp.where(lvl_masks[d][lvl], a_, 0.0) for a_, d in zip(ah, dirs)]
            lt = [_MM_INV(l_, t_) for l_, t_ in zip(lo, t)]
            tlt = [_MM_INV(t_, u_) for t_, u_ in zip(t, lt)]
            t = [t_ - u_ for t_, u_ in zip(t, tlt)]
        x = [_MM_APPLY(t_, x_) for t_, x_ in zip(t, x)]
        for i, (d, r0, r8) in enumerate(items):
            xi = x[i * N_HEADS:(i + 1) * N_HEADS]
            u_s[d, pl.ds(r0, CHUNK), :] = jnp.concatenate([x_[:, :HEAD_DIM] for x_ in xi], axis=1)
            w_s[d, pl.ds(r0, CHUNK), :] = jnp.concatenate([x_[:, HEAD_DIM:] for x_ in xi], axis=1).astype(BF16)
            att_s[d, pl.ds(r0, CHUNK), :] = (qk[i] * decay[i]).astype(BF16)
            qg_s[d, pl.ds(r0, CHUNK), :] = (q[i] * jnp.exp(cg_col[i])).astype(BF16)
            kd_s[d, pl.ds(r0, CHUNK), :] = (k[i] * jnp.exp(g_last[i] - cg_col[i])).astype(BF16)
            gl_s[d, pl.ds(r8, 8), :] = jnp.broadcast_to(jnp.exp(g_last[i]), (8, SEG))
        return carry

    lax.fori_loop(0, n_chunks // group, solve_group, 0)

    def scan(n, carry):
        D2 = range(2)
        rows = [_chunk_rows(d, n, n_chunks) for d in D2]
        rows8 = [pl.multiple_of((n if d == 0 else n_chunks - 1 - n) * 8, 8) for d in D2]
        s = [st_s[d] for d in D2]
        ws = [_mm1(w_s[d, pl.ds(rows[d], CHUNK), :], s[d]) for d in D2]
        qs_ = [_mm1(qg_s[d, pl.ds(rows[d], CHUNK), :], s[d]) for d in D2]
        v_new = [u_s[d, pl.ds(rows[d], CHUNK), :] - ws[d] for d in D2]
        v4 = [_block_diag(x_, hm) for x_ in v_new]
        av = [_mm1(att_s[d, pl.ds(rows[d], CHUNK), :], v4[d]) for d in D2]
        kv = [_mm1(kd_s[d, pl.ds(rows[d], CHUNK), :], v_new[d], TN) for d in D2]
        for d, o_s in enumerate((of_s, ob_s)):
            st_s[d] = s[d] * gl_s[d, pl.ds(rows8[d], 1), :] + jnp.where(hm, kv[d], 0.0)
            o_s[pl.ds(rows[d], CHUNK), :] = qs_[d] + av[d]
        return carry

    lax.fori_loop(0, n_chunks, scan, 0)
    if not has_cache:
        for d in range(2):
            for h in range(N_HEADS):
                sf_ref[d, h] = st_s[d, _hs(h), _hs(h)]
    _epilogue(T, of_s, ob_s, z_ref, _silu, bd_ref, ng_ref, o_ref)


def _deltanet(P, G, T, n_seq, gate_par, conv_w, ng, bd, l, state, prev):
    has_cache = state is not None
    st = (2, N_HEADS, HEAD_DIM, HEAD_DIM)
    in_specs = _seq_specs(T, (S_AQ, S_AK, S_AV, S_AZ)) + [
        _gate_spec(T), _pick_spec((8, GATE_LANES), l), _pick_spec((3, 3 * SEG), l),
        _const_spec((1, SEG)), _const_spec((SEG, SEG))]
    args = [P, P, P, P, G, gate_par, conv_w, ng, bd]
    if has_cache:
        in_specs.append(_state_in_spec(l, st))
        args.append(state)
    scratch = ([pltpu.VMEM((T, SEG), F32)] * 2 + [pltpu.VMEM((2, SEG, SEG), F32)]
               + [pltpu.VMEM((T, SEG), F32)] * 3
               + [pltpu.VMEM((2, T, SEG), F32)] + [pltpu.VMEM((2, T, SEG), BF16)] * 4
               + [pltpu.VMEM((2, T // CHUNK * 8, SEG), F32)])
    return _mixer_call(functools.partial(_delta_kernel, T=T, has_cache=has_cache), "deltanet", T, n_seq,
                       in_specs, args, scratch, () if has_cache else (st,), l, prev)


def _diff_kernel(*refs, T, has_cache, lam_init):
    if has_cache:
        (q_ref, k_ref, v_ref, qg_ref, kg_ref, lam_ref, ng_ref, bd32_ref, bd64_ref,
         cos_ref, sin_ref, ck_ref, cv_ref, o_ref, qs, kh, vt) = refs
    else:
        (q_ref, k_ref, v_ref, qg_ref, kg_ref, lam_ref, ng_ref, bd32_ref, bd64_ref,
         o_ref, ko_ref, vo_ref, qs, kh, vt) = refs
    n_blk = T // ROW_BLOCK
    s0 = ck_ref.shape[1] if has_cache else 0
    bd32 = bd32_ref[...]
    lane = lax.broadcasted_iota(jnp.int32, (1, SEG), 1)
    first_half = (lane % 16) < 8

    if has_cache:
        for h in range(N_HEADS):
            kh[h, 0:s0, :] = ck_ref[h]
            vt[h, :, 0:s0] = cv_ref[h].T

    for i in range(n_blk):
        r0 = i * ROW_BLOCK
        for src, g_ref in ((q_ref, qg_ref), (k_ref, kg_ref)):
            x = src[pl.ds(r0, ROW_BLOCK), :]
            y = x * lax.rsqrt(_gsum(x * x, bd32) * (1.0 / DQK) + EPS) * g_ref[...]
            if has_cache:
                partner = jnp.where(first_half, pltpu.roll(y, SEG - 8, axis=1), pltpu.roll(y, 8, axis=1))
                y = y * cos_ref[pl.ds(r0, ROW_BLOCK), :] + partner * sin_ref[pl.ds(r0, ROW_BLOCK), :]
            if src is q_ref:
                qs[pl.ds(r0, ROW_BLOCK), :] = y
            else:
                for h in range(N_HEADS):
                    kh[h, pl.ds(s0 + r0, ROW_BLOCK), :] = y[:, _hs(h)]
                    if not has_cache:
                        ko_ref[h, pl.ds(r0, ROW_BLOCK), :] = y[:, _hs(h)]
        xv = v_ref[pl.ds(r0, ROW_BLOCK), :]
        xvt = xv.T
        for h in range(N_HEADS):
            vt[h, :, pl.ds(s0 + r0, ROW_BLOCK)] = xvt[_hs(h), :]
            if not has_cache:
                vo_ref[h, pl.ds(r0, ROW_BLOCK), :] = xv[:, _hs(h)]

    lp = lam_ref[...]
    lam = (jnp.exp(jnp.sum(lp[0:1, :] * lp[1:2, :], axis=1, keepdims=True))
           - jnp.exp(jnp.sum(lp[2:3, :] * lp[3:4, :], axis=1, keepdims=True)) + lam_init)
    scale = DQK ** -0.5 * math.log2(math.e)
    comp_rows = lax.broadcasted_iota(jnp.int32, (HEAD_DIM, 1), 0) < DQK
    bd64 = bd64_ref[...]
    ng = ng_ref[...]

    def qblock(i, carry):
        r0 = pl.multiple_of(i * Q_BLOCK, Q_BLOCK)
        qt = (qs[pl.ds(r0, Q_BLOCK), :] * scale).T
        qct = []
        for h in range(N_HEADS):
            qh = qt[_hs(h), :]
            q1 = jnp.where(comp_rows, qh, 0.0)
            qct += [q1, qh - q1]
        n_units = len(qct)
        st = _mm1(kh[0], qct[0])
        res = []
        for u in range(n_units):
            nxt = _mm1(kh[(u + 1) // 2], qct[u + 1]) if u + 1 < n_units else None
            e = jnp.exp2(st - st.max(axis=0, keepdims=True))
            res.append(_mm1(vt[u // 2], e) * (1.0 / e.sum(axis=0, keepdims=True)))
            st = nxt
        outs = [res[2 * h] - lam * res[2 * h + 1] for h in range(N_HEADS)]
        o = jnp.concatenate(outs, axis=0).T
        o_ref[pl.ds(r0, Q_BLOCK), :] = _head_norm_gate(o, bd64, ng, 1.0 - lam_init)
        return carry

    lax.fori_loop(0, T // Q_BLOCK, qblock, 0)


def _diffattn(P, T, n_seq, qg, kg, lam_par, ng, bd32, bd64, l, rope, cache, prev):
    has_cache = cache is not None
    lam_init = 0.8 - 0.6 * math.exp(-0.3 * l)
    kv = (N_HEADS, T, HEAD_DIM)
    in_specs = _seq_specs(T, (S_BQ, S_BK, S_BV)) + [
        _const_spec((1, SEG)), _const_spec((1, SEG)), _pick_spec((4, DQK), l), _const_spec((1, SEG)),
        _const_spec((SEG, SEG)), _const_spec((SEG, SEG))]
    args = [P, P, P, qg, kg, lam_par, ng, bd32, bd64]
    if has_cache:
        ckv = cache[0].shape[2:]
        in_specs += [_const_spec((T, SEG)), _const_spec((T, SEG)), _state_in_spec(l, ckv), _state_in_spec(l, ckv)]
        args += [rope[0], rope[1], cache[0], cache[1]]
    n_keys = T + (cache[0].shape[3] if has_cache else 0)
    scratch = [pltpu.VMEM((T, SEG), F32), pltpu.VMEM((N_HEADS, n_keys, HEAD_DIM), F32),
               pltpu.VMEM((N_HEADS, HEAD_DIM, n_keys), F32)]
    return _mixer_call(functools.partial(_diff_kernel, T=T, has_cache=has_cache, lam_init=lam_init),
                       "diff_attn", T, n_seq, in_specs, args, scratch, () if has_cache else (kv, kv), l, prev)


def _rope_tables(T):
    n_freq = DQK // 4
    t = jnp.arange(T)
    rows = (t // GRID_W).astype(F32)
    cols = (t % GRID_W).astype(F32)
    freqs = ROPE_BASE ** (-jnp.arange(n_freq, dtype=F32) / n_freq)
    ang_r, ang_c = rows[:, None] * freqs, cols[:, None] * freqs

    def comp(fn, sign):
        return jnp.concatenate([fn(ang_r), sign * fn(ang_r), fn(ang_c), sign * fn(ang_c)], axis=1)

    reps = SEG // DQK
    cos = jnp.tile(comp(jnp.cos, 1.0), (1, reps))
    sin = jnp.tile(jnp.concatenate([-jnp.sin(ang_r), jnp.sin(ang_r), -jnp.sin(ang_c), jnp.sin(ang_c)], axis=1),
                   (1, reps))
    return cos, sin


def _block_ones(group):
    i = np.arange(SEG)
    return jnp.asarray(i[:, None] // group == i[None, :] // group, BF16)


def kernel(x_prompt, x_sample, cache_diff_k, cache_diff_v, state_delta, state_mlstm_C, state_mlstm_n, state_mlstm_m, state_ret, c, c_ctx, w_ada, b_ada, norm_g, ffn_w_gate, ffn_w_up, ffn_w_down, w_in, dn_conv_w, dn_a_log, dn_dt_bias, dn_norm_g, da_qn_g, da_kn_g, da_lambda, da_norm_g, ml_i_bias, ml_f_bias, ml_norm_g, ret_decay_logit, ret_norm_g, w_branch, w_out):
    B, T, _ = x_prompt.shape
    Bs, Ts, _ = x_sample.shape
    cond8 = jnp.concatenate([c_ctx[None], c, jnp.zeros((8 - 1 - Bs, D_MODEL), F32)], axis=0)
    mod = _ada(cond8, w_ada, b_ada).reshape(DEPTH, 8, N_MOD, D_MODEL)
    bd32, bd64 = _block_ones(DQK), _block_ones(HEAD_DIM)
    rope = _rope_tables(Ts)
    tile_heads = lambda g: jnp.tile(g, SEG // g.shape[0])[None]

    wg, wu, wd = ffn_w_gate.astype(BF16), ffn_w_up.astype(BF16), ffn_w_down.astype(BF16)
    cols = lambda f: w_in[:, :, _IN_OFFS[f]:_IN_OFFS[f + 1]]
    w_mix = jnp.concatenate([cols(f) for f in _SEG_FIELDS], axis=2).astype(BF16)
    w_gate = jnp.concatenate([cols(f) for f in _GATE_FIELDS]
                             + [jnp.zeros((DEPTH, D_MODEL, GATE_LANES - 32), F32)], axis=2).astype(BF16)
    w_merge = cols(19).astype(BF16)
    wb, wo = w_branch.astype(BF16), w_out.astype(BF16)
    norm_g4 = norm_g.reshape(DEPTH, 3, 1, D_MODEL)
    lanes = lambda *parts: jnp.concatenate([p.reshape(DEPTH, 1, -1) for p in parts], axis=2)
    z8 = jnp.zeros((DEPTH, 8), F32)
    gate_par = jnp.concatenate([
        lanes(z8, dn_dt_bias, ml_i_bias, ml_f_bias, jnp.zeros((DEPTH, GATE_LANES - 32), F32)),
        lanes(z8, dn_a_log, jnp.zeros((DEPTH, GATE_LANES - 16), F32)),
        jnp.zeros((DEPTH, 6, GATE_LANES), F32)], axis=1)
    ml_state = (state_mlstm_C, state_mlstm_n.reshape(Bs, DEPTH, 2, 1, SEG),
                jnp.repeat(state_mlstm_m, HEAD_DIM, axis=-1).reshape(Bs, DEPTH, 2, 1, SEG))

    xs = {"ctx": x_prompt.reshape(B * T, D_MODEL), "smp": x_sample.reshape(Bs * Ts, D_MODEL)}
    geo = {"ctx": (T, B, B * T, 0), "smp": (Ts, Bs, Ts, 1)}
    states = {}
    for l in range(DEPTH):
        for path in ("ctx", "smp"):
            Tp, n_seq, rows_per_cond, first_cond = geo[path]
            mspec = _mod_spec(l, rows_per_cond, first_cond)
            smp = path == "smp"
            prev = states.get if not smp else (lambda name: None)
            x = _ffn(xs[path], mod, mspec, norm_g4, wg, wu, wd, l, 0)
            P, G = _inproj(x, mod, mspec, norm_g4, w_mix, w_gate, l)
            ra = _deltanet(P, G, Tp, n_seq, gate_par, dn_conv_w, tile_heads(dn_norm_g[l]), bd64, l,
                           state_delta if smp else None, prev("dn"))
            rb = _diffattn(P, Tp, n_seq, tile_heads(da_qn_g[l]), tile_heads(da_kn_g[l]), da_lambda,
                           tile_heads(da_norm_g[l]), bd32, bd64, l, rope if smp else None,
                           (cache_diff_k, cache_diff_v) if smp else None, prev("kv"))
            rc = _mlstm(P, G, Tp, n_seq, gate_par, tile_heads(ml_norm_g[l]), bd64, l,
                        ml_state if smp else None, prev("ml"))
            rd = _retention(P, Tp, n_seq, ret_decay_logit[l], tile_heads(ret_norm_g[l]), bd64, l,
                            state_ret if smp else None, prev("r"))
            if not smp:
                states = {"dn": ra[1:], "kv": rb[1:], "ml": rc[1:], "r": rd[1:]}
            x = _merge(x, mod, mspec, norm_g4, (ra[0], rb[0], rc[0], rd[0]), w_merge, wb, wo, l)
            xs[path] = _ffn(x, mod, mspec, norm_g4, wg, wu, wd, l, 1)
    (new_dn,), (new_k, new_v), (new_c, new_n, new_m), (new_r,) = (states[k] for k in ("dn", "kv", "ml", "r"))
    per_head = lambda rows: rows.reshape(B, DEPTH, 2, N_HEADS, HEAD_DIM)
    return (xs["ctx"].reshape(B, T, D_MODEL), xs["smp"].reshape(Bs, Ts, D_MODEL),
            new_k, new_v, new_dn, new_c, per_head(new_n), per_head(new_m)[..., 0], new_r)
```

```python
import functools
import math

import numpy as np
import jax
import jax.numpy as jnp
from jax import lax
from jax.experimental import pallas as pl
from jax.experimental.pallas import tpu as pltpu

F32 = jnp.float32
BF16 = jnp.bfloat16

D_MODEL = 1024
FFN_DIM = 2816
N_MOD = 9
DEPTH = 2
N_HEADS = 4
HEAD_DIM = 64
SEG = N_HEADS * HEAD_DIM
N_SEG = 15
CHUNK = 64
DQK = 32
GRID_W = 64
ROPE_BASE = 10000.0
EPS = 1e-6
N_BRANCH = 4
GATE_LANES = 128
NEG = -1e30

TM = 512
Q_BLOCK = 256
ROW_BLOCK = 128
SEQS_PER_STEP = 4
VMEM_LIMIT = 56 * 1024 * 1024

NN = (((1,), (0,)), ((), ()))
NT = (((1,), (1,)), ((), ()))
TN = (((0,), (0,)), ((), ()))

_IN_SIZES = (256, 256, 256, 256, 8, 8, 256, 256, 256, 256, 256, 256, 256, 8, 8, 256, 256, 256, 256, 4096)
_IN_OFFS = np.concatenate([[0], np.cumsum(_IN_SIZES)]).tolist()
_SEG_FIELDS = (0, 1, 2, 3, 6, 7, 8, 9, 10, 11, 12, 15, 16, 17, 18)
_GATE_FIELDS = (4, 5, 13, 14)
(S_AQ, S_AK, S_AV, S_AZ, S_BQ, S_BK, S_BV, S_CQ, S_CK, S_CV, S_CO, S_DQ, S_DK, S_DV, S_DG) = range(N_SEG)
L_BETA, L_ALPHA, L_CI, L_CF = 0, 8, 16, 24


def _dg(a, b, dims):
    return lax.dot_general(a, b, dims, preferred_element_type=F32)


def _split2(x):
    hi = x.astype(BF16)
    lo = (x - hi.astype(F32)).astype(BF16)
    return hi, lo


def _mm1(a, b, dims=NN):
    return _dg(a.astype(BF16), b.astype(BF16), dims)


def _mm3(a, b, dims=NN):
    ah, al = _split2(a)
    bh, bl = _split2(b)
    return _dg(ah, bh, dims) + (_dg(ah, bl, dims) + _dg(al, bh, dims))


def _mm2r(a, b, dims=NN):
    ah = a.astype(BF16)
    bh, bl = _split2(b)
    return _dg(ah, bh, dims) + _dg(ah, bl, dims)


def _sel_mm(sel, x, dims=NN):
    h0 = x.astype(BF16)
    r1 = x - h0.astype(F32)
    h1 = r1.astype(BF16)
    h2 = (r1 - h1.astype(F32)).astype(BF16)
    return _dg(sel, h0, dims) + (_dg(sel, h1, dims) + _dg(sel, h2, dims))


def _gsum(x, bd):
    hi, lo = _split2(x)
    return _dg(hi, bd, NN) + _dg(lo, bd, NN)


def _sigmoid(x):
    return 1.0 / (1.0 + jnp.exp(-x))


def _silu(x):
    return x * _sigmoid(x)


def _softplus(x):
    return jnp.maximum(x, 0.0) + jnp.log1p(jnp.exp(-jnp.abs(x)))


def _log_sigmoid(x):
    return -_softplus(-x)


def _norm_mod(x, g, shift, scale):
    ms = jnp.mean(x * x, axis=-1, keepdims=True)
    return (x * lax.rsqrt(ms + EPS) * g) * (1.0 + scale) + shift


def _tri_masks():
    ii = lax.broadcasted_iota(jnp.int32, (CHUNK, CHUNK), 0)
    jj = lax.broadcasted_iota(jnp.int32, (CHUNK, CHUNK), 1)
    return ((jj <= ii, jj < ii), (jj >= ii, jj > ii))


def _cparams(n_grid=1):
    return pltpu.CompilerParams(dimension_semantics=("arbitrary",) * n_grid,
                                vmem_limit_bytes=VMEM_LIMIT)


def _const_spec(shape):
    nd = len(shape)
    return pl.BlockSpec(shape, lambda *_: (0,) * nd)


def _ada_kernel(s_ref, w_ref, b_ref, o_ref):
    s = s_ref[...]
    o_ref[...] = _mm3(_silu(s), w_ref[...]) + b_ref[...]


def _ada(cond8, w_ada, b_ada):
    tn = 1536
    n_t = (N_MOD * D_MODEL) // tn
    return pl.pallas_call(
        _ada_kernel,
        grid=(DEPTH, n_t),
        in_specs=[pl.BlockSpec((8, D_MODEL), lambda l, j: (0, 0)),
                  pl.BlockSpec((None, D_MODEL, tn), lambda l, j: (l, 0, j)),
                  pl.BlockSpec((None, 1, tn), lambda l, j: (l, 0, j))],
        out_specs=pl.BlockSpec((None, 8, tn), lambda l, j: (l, 0, j)),
        out_shape=jax.ShapeDtypeStruct((DEPTH, 8, N_MOD * D_MODEL), F32),
        compiler_params=_cparams(2),
        name="ada_mod",
    )(cond8, w_ada, b_ada.reshape(DEPTH, 1, N_MOD * D_MODEL))


def _ffn_kernel(x_ref, mod_ref, g_ref, wg_ref, wu_ref, wd_ref, o_ref, *, mi):
    x = x_ref[...]
    h = _norm_mod(x, g_ref[...], mod_ref[mi:mi + 1, :], mod_ref[mi + 1:mi + 2, :]).astype(BF16)
    gate = jnp.dot(h, wg_ref[...], preferred_element_type=F32)
    up = jnp.dot(h, wu_ref[...], preferred_element_type=F32)
    act = (_silu(gate) * up).astype(BF16)
    y = jnp.dot(act, wd_ref[...], preferred_element_type=F32)
    o_ref[...] = x + (0.5 * mod_ref[mi + 2:mi + 3, :]) * y


def _pick_spec(tail, *lead):
    nd = len(tail)
    return pl.BlockSpec((None,) * len(lead) + tuple(tail), lambda *_: tuple(lead) + (0,) * nd,
                        pipeline_mode=pl.Buffered(1))


def _mod_spec(l, rows_per_cond, first_cond):
    per = rows_per_cond // TM
    return pl.BlockSpec((None, None, N_MOD, D_MODEL), lambda i: (l, first_cond + i // per, 0, 0))


def _ffn(x, mod, mod_spec, norm_g, wg, wu, wd, l, j):
    n = x.shape[0]
    return pl.pallas_call(
        functools.partial(_ffn_kernel, mi=6 * j),
        grid=(n // TM,),
        in_specs=[pl.BlockSpec((TM, D_MODEL), lambda i: (i, 0)),
                  mod_spec,
                  _pick_spec((1, D_MODEL), l, 2 * j),
                  _pick_spec((D_MODEL, FFN_DIM), l, j),
                  _pick_spec((D_MODEL, FFN_DIM), l, j),
                  _pick_spec((FFN_DIM, D_MODEL), l, j)],
        out_specs=pl.BlockSpec((TM, D_MODEL), lambda i: (i, 0)),
        out_shape=jax.ShapeDtypeStruct((n, D_MODEL), F32),
        compiler_params=_cparams(1),
        name="ffn",
    )(x, mod, norm_g, wg, wu, wd)


def _inproj_kernel(x_ref, mod_ref, g_ref, w_ref, wgt_ref, p_ref, gt_ref):
    h = _norm_mod(x_ref[...], g_ref[...], mod_ref[3:4, :], mod_ref[4:5, :]).astype(BF16)
    step = 3 * SEG
    for j in range(0, N_SEG * SEG, step):
        p_ref[:, j:j + step] = jnp.dot(h, w_ref[:, j:j + step], preferred_element_type=F32)
    gt_ref[...] = jnp.dot(h, wgt_ref[...], preferred_element_type=F32)


def _inproj(x, mod, mod_spec, norm_g, w_mix, w_gate, l):
    n = x.shape[0]
    return pl.pallas_call(
        _inproj_kernel,
        grid=(n // TM,),
        in_specs=[pl.BlockSpec((TM, D_MODEL), lambda i: (i, 0)),
                  mod_spec,
                  _pick_spec((1, D_MODEL), l, 1),
                  _pick_spec((D_MODEL, N_SEG * SEG), l),
                  _pick_spec((D_MODEL, GATE_LANES), l)],
        out_specs=[pl.BlockSpec((TM, N_SEG * SEG), lambda i: (i, 0)),
                   pl.BlockSpec((TM, GATE_LANES), lambda i: (i, 0))],
        out_shape=[jax.ShapeDtypeStruct((n, N_SEG * SEG), F32),
                   jax.ShapeDtypeStruct((n, GATE_LANES), F32)],
        compiler_params=_cparams(1),
        name="in_proj",
    )(x, mod, norm_g, w_mix, w_gate)


def _merge_kernel(x_ref, mod_ref, g_ref, ba_ref, bb_ref, bc_ref, bd_ref, wm_ref, wb_ref, wo_ref, o_ref):
    x = x_ref[...]
    h = _norm_mod(x, g_ref[...], mod_ref[3:4, :], mod_ref[4:5, :]).astype(BF16)
    mixed = None
    for m, b_ref in enumerate((ba_ref, bb_ref, bc_ref, bd_ref)):
        logits = jnp.dot(h, wm_ref[:, m * D_MODEL:(m + 1) * D_MODEL], preferred_element_type=F32)
        pb = jnp.dot(b_ref[...].astype(BF16), wb_ref[m], preferred_element_type=F32)
        term = _sigmoid(logits) * pb
        mixed = term if mixed is None else mixed + term
    y = jnp.dot(mixed.astype(BF16), wo_ref[...], preferred_element_type=F32)
    o_ref[...] = x + mod_ref[5:6, :] * y


def _merge(x, mod, mod_spec, norm_g, branches, w_merge, w_branch, w_out, l):
    n = x.shape[0]
    row = lambda w: pl.BlockSpec((TM, w), lambda i: (i, 0))
    return pl.pallas_call(
        _merge_kernel,
        grid=(n // TM,),
        in_specs=[row(D_MODEL), mod_spec, _pick_spec((1, D_MODEL), l, 1),
                  row(SEG), row(SEG), row(SEG), row(SEG),
                  _pick_spec((D_MODEL, N_BRANCH * D_MODEL), l),
                  _pick_spec((N_BRANCH, SEG, D_MODEL), l),
                  _pick_spec((D_MODEL, D_MODEL), l)],
        out_specs=row(D_MODEL),
        out_shape=jax.ShapeDtypeStruct((n, D_MODEL), F32),
        compiler_params=_cparams(1),
        name="merge",
    )(x, mod, norm_g, *branches, w_merge, w_branch, w_out)


def _state_in_spec(l, tail):
    nd = len(tail)
    return pl.BlockSpec((None, None) + tail, lambda s: (s, l) + (0,) * nd)


def _per_sequence(kernel, T, n_sub, n_rows_in, n_in, n_alias, n_state, l, creates, *refs):
    ins, rest = refs[:n_in], refs[n_in + n_alias:]
    main, states, scratch = rest[0], rest[1:1 + n_state], rest[1 + n_state:]
    if creates:
        for r in states:
            for other in range(DEPTH):
                if other != l:
                    r[:, other] = jnp.zeros((n_sub,) + r.shape[2:], F32)

    def one(sub, carry):
        rows = pl.ds(pl.multiple_of(sub * T, T), T)
        own = [r.at[sub, l] if creates else r.at[sub] for r in states]
        kernel(*[r.at[rows] for r in ins[:n_rows_in]], *ins[n_rows_in:], main.at[rows], *own, *scratch)
        return carry

    lax.fori_loop(0, n_sub, one, 0)


def _mixer_call(kernel, name, T, n_seq, P, segs, G, in_specs, args, scratch, state_tails=(), l=0, prev=None):
    creates = prev is None
    n_sub = SEQS_PER_STEP if state_tails else 1
    rows = T * n_sub
    row_specs = [pl.BlockSpec((rows, SEG), lambda s, j=j: (s, j)) for j in segs]
    row_args = [P] * len(segs)
    if G is not None:
        row_specs.append(pl.BlockSpec((rows, GATE_LANES), lambda s: (s, 0)))
        row_args.append(G)
    in_specs, args = row_specs + in_specs, row_args + args
    n_in, n_alias = len(args), 0 if creates else len(prev)

    def state_spec(t):
        if creates:
            return pl.BlockSpec((n_sub, DEPTH) + t, lambda s: (s,) + (0,) * (1 + len(t)))
        return pl.BlockSpec((n_sub, None) + t, lambda s: (s, l) + (0,) * len(t))

    out_specs = [pl.BlockSpec((rows, SEG), lambda s: (s, 0))] + [state_spec(t) for t in state_tails]
    out_shape = ([jax.ShapeDtypeStruct((n_seq * T, SEG), F32)]
                 + [jax.ShapeDtypeStruct((n_seq, DEPTH) + t, F32) for t in state_tails])
    aliases = {}
    if not creates:
        in_specs = in_specs + [pl.BlockSpec(memory_space=pl.ANY)] * n_alias
        args = args + list(prev)
        aliases = {n_in + i: 1 + i for i in range(n_alias)}
    body = functools.partial(_per_sequence, kernel, T, n_sub, len(row_args), n_in, n_alias,
                             len(state_tails), l, creates)
    return pl.pallas_call(
        body, grid=(n_seq // n_sub,), in_specs=in_specs, out_specs=out_specs, out_shape=out_shape,
        scratch_shapes=scratch, input_output_aliases=aliases, compiler_params=_cparams(1), name=name)(*args)


def _head_norm_gate(o, bd, ng, gate):
    ss = _gsum(o * o, bd)
    return o * lax.rsqrt(ss * (1.0 / HEAD_DIM) + EPS) * ng * gate


def _epilogue(T, of_s, ob_s, gate_ref, gate_fn, bd_ref, ng_ref, o_ref):
    rb = 256
    bd = bd_ref[...]
    ng = ng_ref[...]

    def blk(i, carry):
        r0 = pl.multiple_of(i * rb, rb)
        o = of_s[pl.ds(r0, rb), :] + ob_s[pl.ds(r0, rb), :]
        o_ref[pl.ds(r0, rb), :] = _head_norm_gate(o, bd, ng, gate_fn(gate_ref[pl.ds(r0, rb), :]))
        return carry

    lax.fori_loop(0, T // rb, blk, 0)


def _chunk_rows(d, n, n_chunks):
    c = n if d == 0 else n_chunks - 1 - n
    return pl.multiple_of(c * CHUNK, CHUNK)


def _hs(h):
    return slice(h * HEAD_DIM, (h + 1) * HEAD_DIM)


def _head_mask():
    ii = lax.broadcasted_iota(jnp.int32, (SEG, SEG), 0) // HEAD_DIM
    jj = lax.broadcasted_iota(jnp.int32, (SEG, SEG), 1) // HEAD_DIM
    return ii == jj


def _blocksum(x):
    c = CHUNK
    return (x[0:c] + x[c:2 * c]) + (x[2 * c:3 * c] + x[3 * c:4 * c])


def _lane_head():
    return lax.broadcasted_iota(jnp.int32, (1, SEG), 1) // HEAD_DIM


def _tiled_masks():
    ii = lax.broadcasted_iota(jnp.int32, (CHUNK, SEG), 0)
    jj = lax.broadcasted_iota(jnp.int32, (CHUNK, SEG), 1) % HEAD_DIM
    return ((jj <= ii, jj < ii), (jj >= ii, jj > ii)), jj == ii


def _block_diag(x, hm):
    return jnp.where(hm, jnp.concatenate([x] * N_HEADS, axis=0), 0.0)


def _col_dense(g, lane0, lane_head):
    out = jnp.broadcast_to(g[:, lane0:lane0 + 1], (CHUNK, SEG))
    for h in range(1, N_HEADS):
        out = jnp.where(lane_head == h, g[:, lane0 + h:lane0 + h + 1], out)
    return out


def _diag_row(col_dense, eye_t):
    return jnp.sum(jnp.where(eye_t, col_dense, 0.0), axis=0, keepdims=True)


def _seg_max(x, lane_head):
    out = None
    for h in range(N_HEADS):
        m = jnp.max(jnp.where(lane_head == h, x, NEG), axis=1, keepdims=True)
        out = m if out is None else jnp.where(lane_head == h, m, out)
    return jnp.broadcast_to(out, x.shape)


UNITS = tuple((d, h) for d in range(2) for h in range(N_HEADS))


def _ret_kernel(*refs, T, has_cache):
    if has_cache:
        (q_ref, k_ref, v_ref, gt_ref, dl_ref, ng_ref, bd_ref, r0_ref, o_ref,
         of_s, ob_s, st_s, dec_s, qdec_s, kdec_s, cdec_s) = refs
    else:
        (q_ref, k_ref, v_ref, gt_ref, dl_ref, ng_ref, bd_ref, o_ref, rf_ref,
         of_s, ob_s, st_s, dec_s, qdec_s, kdec_s, cdec_s) = refs
    n_chunks = T // CHUNK
    masks = _tri_masks()
    hm = _head_mask()

    @pl.when(pl.program_id(0) == 0)
    def _():
        ii = lax.broadcasted_iota(jnp.int32, (CHUNK, CHUNK), 0)
        jj = lax.broadcasted_iota(jnp.int32, (CHUNK, CHUNK), 1)
        rel = (ii - jj).astype(F32)
        pos = lax.broadcasted_iota(jnp.int32, (CHUNK, 1), 0).astype(F32)
        lg_all = _log_sigmoid(dl_ref[...])
        for d in range(2):
            dec, qdec, kdec, cdec = [], [], [], []
            for h in range(N_HEADS):
                lg = lg_all[d:d + 1, h:h + 1]
                if d == 0:
                    e, qd, kd = rel * lg, (pos + 1.0) * lg, (CHUNK - 1.0 - pos) * lg
                else:
                    e, qd, kd = -rel * lg, (CHUNK - pos) * lg, pos * lg
                dec.append(jnp.exp(jnp.where(masks[d][0], e, NEG)))
                qdec.append(jnp.broadcast_to(jnp.exp(qd), (CHUNK, HEAD_DIM)))
                kdec.append(jnp.broadcast_to(jnp.exp(kd), (CHUNK, HEAD_DIM)))
                cdec.append(jnp.broadcast_to(jnp.exp(CHUNK * lg), (HEAD_DIM, SEG)))
            dec_s[d] = jnp.concatenate(dec, axis=0)
            qdec_s[d] = jnp.concatenate(qdec, axis=1)
            kdec_s[d] = jnp.concatenate(kdec, axis=1)
            cdec_s[d] = jnp.concatenate(cdec, axis=0)

    for d in range(2):
        st_s[d] = jnp.zeros((SEG, SEG), F32)
        if has_cache:
            for h in range(N_HEADS):
                st_s[d, _hs(h), _hs(h)] = r0_ref[d, h]

    def body(n, carry):
        rows = [_chunk_rows(d, n, n_chunks) for d in range(2)]
        q = [q_ref[pl.ds(r0, CHUNK), :] for r0 in rows]
        k = [k_ref[pl.ds(r0, CHUNK), :] * (HEAD_DIM ** -0.5) for r0 in rows]
        v = [v_ref[pl.ds(r0, CHUNK), :] for r0 in rows]
        q4 = [jnp.where(hm, jnp.concatenate([x] * N_HEADS, axis=0), 0.0) for x in q]
        qk = [_mm1(a, b, NT) for a, b in zip(q4, k)]
        qr = [_mm1(q[d] * qdec_s[d], st_s[d]) for d in range(2)]
        kv = [_mm1(k[d] * kdec_s[d], v[d], TN) for d in range(2)]
        av = [_mm1(qk[d] * dec_s[d], v[d]) for d in range(2)]
        for d, o_s in enumerate((of_s, ob_s)):
            st_s[d] = cdec_s[d] * st_s[d] + jnp.where(hm, kv[d], 0.0)
            o_s[pl.ds(rows[d], CHUNK), :] = qr[d] + _blocksum(jnp.where(hm, av[d], 0.0))
        return carry

    lax.fori_loop(0, n_chunks, body, 0)
    if not has_cache:
        for d in range(2):
            for h in range(N_HEADS):
                rf_ref[d, h] = st_s[d, _hs(h), _hs(h)]
    _epilogue(T, of_s, ob_s, gt_ref, _silu, bd_ref, ng_ref, o_ref)


def _retention(P, T, n_seq, decay_logit, ng, bd, l, state, prev):
    has_cache = state is not None
    st = (2, N_HEADS, HEAD_DIM, HEAD_DIM)
    in_specs = [_const_spec((2, N_HEADS)), _const_spec((1, SEG)), _const_spec((SEG, SEG))]
    args = [decay_logit, ng, bd]
    if has_cache:
        in_specs.append(_state_in_spec(l, st))
        args.append(state)
    scratch = [pltpu.VMEM((T, SEG), F32), pltpu.VMEM((T, SEG), F32), pltpu.VMEM((2, SEG, SEG), F32),
               pltpu.VMEM((2, SEG, HEAD_DIM), F32), pltpu.VMEM((2, CHUNK, SEG), F32),
               pltpu.VMEM((2, CHUNK, SEG), F32), pltpu.VMEM((2, SEG, SEG), F32)]
    return _mixer_call(functools.partial(_ret_kernel, T=T, has_cache=has_cache), "retention", T, n_seq,
                       P, (S_DQ, S_DK, S_DV, S_DG), None,
                       in_specs, args, scratch, () if has_cache else (st,), l, prev)


def _mlstm_kernel(*refs, T, has_cache):
    if has_cache:
        (q_ref, k_ref, v_ref, og_ref, gt_ref, gp_ref, ng_ref, bd_ref, c0_ref, n0_ref, m0_ref,
         o_ref, of_s, ob_s, c_s, n_s, m_s) = refs
    else:
        (q_ref, k_ref, v_ref, og_ref, gt_ref, gp_ref, ng_ref, bd_ref,
         o_ref, cf_ref, nf_ref, mf_ref, of_s, ob_s, c_s, n_s, m_s) = refs
    n_chunks = T // CHUNK
    masks = _tri_masks()
    tri = (masks[0][0].astype(BF16), masks[1][0].astype(BF16))
    masks_t, eye_t = _tiled_masks()
    tri_t = [masks_t[d][0] for d in range(2)]
    hm = _head_mask()
    lane_head = _lane_head()
    bd = bd_ref[...]
    bias = gp_ref[0:1, :]

    for d in range(2):
        c_s[d] = jnp.zeros((SEG, SEG), F32)
        if has_cache:
            for h in range(N_HEADS):
                c_s[d, _hs(h), _hs(h)] = c0_ref[d, h]
            n_s[d] = n0_ref[d]
            m_s[d] = m0_ref[d]
        else:
            n_s[d] = jnp.zeros((1, SEG), F32)
            m_s[d] = jnp.zeros((1, SEG), F32)

    def body(n, carry):
        rows = [_chunk_rows(d, n, n_chunks) for d in range(2)]
        q = [q_ref[pl.ds(r0, CHUNK), :] * (HEAD_DIM ** -0.5) for r0 in rows]
        k = [k_ref[pl.ds(r0, CHUNK), :] for r0 in rows]
        v = [v_ref[pl.ds(r0, CHUNK), :] for r0 in rows]
        pre = [gt_ref[pl.ds(r0, CHUNK), :] + bias for r0 in rows]
        b = [_sel_mm(tri[d], _log_sigmoid(pre[d])) for d in range(2)]

        k4 = [_block_diag(x, hm) for x in k]
        v4 = [_block_diag(x, hm) for x in v]
        qk = [_mm1(q[d], k4[d], NT) for d in range(2)]
        qc = [_mm1(q[d], c_s[d]) for d in range(2)]
        qn = [_gsum(q[d] * n_s[d], bd) for d in range(2)]

        outs = []
        for d in range(2):
            b_col = _col_dense(b[d], L_CF + d * N_HEADS, lane_head)
            ig_col = _col_dense(pre[d], L_CI + d * N_HEADS, lane_head)
            b_row, ig_row = _diag_row(b_col, eye_t), _diag_row(ig_col, eye_t)
            last = CHUNK - 1 if d == 0 else 0
            b_last = b_col[last:last + 1, :]
            m_prev = m_s[d]
            dm = jnp.where(tri_t[d], b_col - b_row + ig_row, NEG)
            inter = b_col + m_prev
            m_i = jnp.maximum(inter, _seg_max(dm, lane_head))
            s = qk[d] * jnp.exp(dm - m_i)
            sv = _mm1(s, v4[d])
            ssum = _gsum(s, bd)
            w_inter = jnp.exp(inter - m_i)
            num = w_inter * qc[d] + sv
            den = w_inter * qn[d] + ssum
            outs.append(num / jnp.maximum(jnp.abs(den), jnp.exp(-m_i)))

            m_new = jnp.maximum(b_last + m_prev, _seg_max(b_last - b_row + ig_row, lane_head))
            wk = k[d] * jnp.exp(b_last - b_col + ig_col - m_new)
            dec = jnp.exp(b_last + m_prev - m_new)
            kv = _mm1(wk, v[d], TN)
            c_s[d] = dec * c_s[d] + jnp.where(hm, kv, 0.0)
            n_s[d] = dec * n_s[d] + jnp.sum(wk, axis=0, keepdims=True)
            m_s[d] = m_new
        of_s[pl.ds(rows[0], CHUNK), :] = outs[0]
        ob_s[pl.ds(rows[1], CHUNK), :] = outs[1]
        return carry

    lax.fori_loop(0, n_chunks, body, 0)
    if not has_cache:
        for d in range(2):
            for h in range(N_HEADS):
                cf_ref[d, h] = c_s[d, _hs(h), _hs(h)]
        nf_ref[...] = n_s[...]
        mf_ref[...] = m_s[...]
    _epilogue(T, of_s, ob_s, og_ref, _sigmoid, bd_ref, ng_ref, o_ref)


def _mlstm(P, G, T, n_seq, gate_par, ng, bd, l, state, prev):
    has_cache = state is not None
    st = (2, N_HEADS, HEAD_DIM, HEAD_DIM)
    rw = (2, 1, SEG)
    in_specs = [_pick_spec((8, GATE_LANES), l), _const_spec((1, SEG)), _const_spec((SEG, SEG))]
    args = [gate_par, ng, bd]
    if has_cache:
        in_specs += [_state_in_spec(l, st), _state_in_spec(l, rw), _state_in_spec(l, rw)]
        args += list(state)
    scratch = [pltpu.VMEM((T, SEG), F32), pltpu.VMEM((T, SEG), F32), pltpu.VMEM((2, SEG, SEG), F32),
               pltpu.VMEM(rw, F32), pltpu.VMEM(rw, F32)]
    return _mixer_call(functools.partial(_mlstm_kernel, T=T, has_cache=has_cache), "mlstm", T, n_seq,
                       P, (S_CQ, S_CK, S_CV, S_CO), G,
                       in_specs, args, scratch, () if has_cache else (st, rw, rw), l, prev)


INV_BASE = 8
SOLVE_GROUP = 4
_MM_INV = _mm1
_MM_APPLY = _mm2r


def _inverse_level_masks():
    ii = lax.broadcasted_iota(jnp.int32, (CHUNK, CHUNK), 0)
    jj = lax.broadcasted_iota(jnp.int32, (CHUNK, CHUNK), 1)
    out = []
    for lo, hi in ((jj, ii), (ii, jj)):
        lv = [(lo // INV_BASE == hi // INV_BASE) & (lo < hi)]
        size = 2 * INV_BASE
        while size <= CHUNK:
            lv.append((lo // size == hi // size) & (hi % size >= size // 2) & (lo % size < size // 2))
            size *= 2
        out.append(lv)
    return out


def _delta_kernel(*refs, T, has_cache):
    if has_cache:
        (q_ref, k_ref, v_ref, z_ref, gt_ref, gp_ref, cw_ref, ng_ref, bd_ref, s0_ref,
         o_ref, of_s, ob_s, st_s, qs, ks, vs, u_s, w_s, att_s, qg_s, kd_s, gl_s) = refs
    else:
        (q_ref, k_ref, v_ref, z_ref, gt_ref, gp_ref, cw_ref, ng_ref, bd_ref,
         o_ref, sf_ref, of_s, ob_s, st_s, qs, ks, vs, u_s, w_s, att_s, qg_s, kd_s, gl_s) = refs
    n_chunks = T // CHUNK
    group = min(SOLVE_GROUP, n_chunks)
    n_blk = T // ROW_BLOCK
    masks = _tri_masks()
    tri = (masks[0][0].astype(BF16), masks[1][0].astype(BF16))
    masks_t, eye_t = _tiled_masks()
    incl_t = [masks_t[d][0] for d in range(2)]
    strict_t = [masks_t[d][1] for d in range(2)]
    hm = _head_mask()
    lane_head = _lane_head()
    lvl_masks = _inverse_level_masks()
    eye64 = (masks[0][0] & masks[1][0]).astype(F32)
    bd = bd_ref[...]
    bias, a_log = gp_ref[0:1, :], gp_ref[1:2, :]
    row = lax.broadcasted_iota(jnp.int32, (ROW_BLOCK, 1), 0)

    def prologue(i, carry):
        r0 = pl.multiple_of(i * ROW_BLOCK, ROW_BLOCK)
        rp = pl.multiple_of(jnp.maximum(r0 - 8, 0), 8)
        rn = pl.multiple_of(jnp.minimum(r0 + ROW_BLOCK, T - 8), 8)
        for j, (src, dst) in enumerate(((q_ref, qs), (k_ref, ks), (v_ref, vs))):
            cur = src[pl.ds(r0, ROW_BLOCK), :]
            before = jnp.where(i > 0, src[pl.ds(rp, 8), :][7:8, :], 0.0)
            after = jnp.where(i < n_blk - 1, src[pl.ds(rn, 8), :][0:1, :], 0.0)
            down = jnp.where(row == 0, before, pltpu.roll(cur, 1, axis=0))
            up = jnp.where(row == ROW_BLOCK - 1, after, pltpu.roll(cur, ROW_BLOCK - 1, axis=0))
            w = cw_ref[:, j * SEG:(j + 1) * SEG]
            y = _silu(w[0:1, :] * down + w[1:2, :] * cur + w[2:3, :] * up)
            if j < 2:
                y = y * lax.rsqrt(_gsum(y * y, bd) + EPS)
            if j == 0:
                y = y * (HEAD_DIM ** -0.5)
            dst[pl.ds(r0, ROW_BLOCK), :] = y
        return carry

    lax.fori_loop(0, n_blk, prologue, 0)

    for d in range(2):
        st_s[d] = jnp.zeros((SEG, SEG), F32)
        if has_cache:
            for h in range(N_HEADS):
                st_s[d, _hs(h), _hs(h)] = s0_ref[d, h]

    def solve_group(g, carry):
        items = []
        q, k, v, beta, cg_col, g_last, decay, kb, k4 = ([] for _ in range(9))
        for cc in range(group):
            c = g * group + cc
            r0 = pl.multiple_of(c * CHUNK, CHUNK)
            qc, kc, vc = qs[pl.ds(r0, CHUNK), :], ks[pl.ds(r0, CHUNK), :], vs[pl.ds(r0, CHUNK), :]
            pre = gt_ref[pl.ds(r0, CHUNK), :]
            g_all = -jnp.exp(a_log) * _softplus(pre + bias)
            beta_all = _sigmoid(pre)
            kc4 = _block_diag(kc, hm)
            for d in range(2):
                items.append((d, r0, pl.multiple_of(c * 8, 8)))
                cg = _sel_mm(tri[d], g_all)
                col = _col_dense(cg, L_ALPHA + d * N_HEADS, lane_head)
                row_ = _diag_row(col, eye_t)
                bt = _col_dense(beta_all, L_BETA + d * N_HEADS, lane_head)
                q.append(qc), k.append(kc), v.append(vc), k4.append(kc4)
                beta.append(bt), cg_col.append(col), kb.append(kc * bt)
                g_last.append(col[CHUNK - 1:CHUNK, :] if d == 0 else col[0:1, :])
                decay.append(jnp.exp(jnp.where(incl_t[d], col - row_, NEG)))
        n_it = len(items)
        kk = [_mm1(kb[i], k4[i], NT) for i in range(n_it)]
        qk = [_mm1(q[i], k4[i], NT) for i in range(n_it)]
        a = [jnp.where(strict_t[items[i][0]], kk[i] * decay[i], 0.0) for i in range(n_it)]
        xu = [v[i] * beta[i] for i in range(n_it)]
        xw = [kb[i] * jnp.exp(cg_col[i]) for i in range(n_it)]

        units = [(i, h) for i in range(n_it) for h in range(N_HEADS)]
        dirs = [items[i][0] for i, h in units]
        ah = [a[i][:, _hs(h)] for i, h in units]
        x = [jnp.concatenate([xu[i][:, _hs(h)], xw[i][:, _hs(h)]], axis=1) for i, h in units]
        dg = [jnp.where(lvl_masks[d][0], a_, 0.0) for a_, d in zip(ah, dirs)]
        t = [eye64 - d_ for d_ in dg]
        p = [_MM_INV(d_, d_) for d_ in dg]
        pt = [_MM_INV(p_, t_) for p_, t_ in zip(p, t)]
        t = [t_ + u_ for t_, u_ in zip(t, pt)]
        p = [_MM_INV(p_, p_) for p_ in p]
        pt = [_MM_INV(p_, t_) for p_, t_ in zip(p, t)]
        t = [t_ + u_ for t_, u_ in zip(t, pt)]
        for lvl in range(1, len(lvl_masks[0])):
            lo = [jnp.where(lvl_masks[d][lvl], a_, 0.0) for a_, d in zip(ah, dirs)]
            lt = [_MM_INV(l_, t_) for l_, t_ in zip(lo, t)]
            tlt = [_MM_INV(t_, u_) for t_, u_ in zip(t, lt)]
            t = [t_ - u_ for t_, u_ in zip(t, tlt)]
        x = [_MM_APPLY(t_, x_) for t_, x_ in zip(t, x)]
        for i, (d, r0, r8) in enumerate(items):
            xi = x[i * N_HEADS:(i + 1) * N_HEADS]
            u_s[d, pl.ds(r0, CHUNK), :] = jnp.concatenate([x_[:, :HEAD_DIM] for x_ in xi], axis=1)
            w_s[d, pl.ds(r0, CHUNK), :] = jnp.concatenate([x_[:, HEAD_DIM:] for x_ in xi], axis=1).astype(BF16)
            att_s[d, pl.ds(r0, CHUNK), :] = (qk[i] * decay[i]).astype(BF16)
            qg_s[d, pl.ds(r0, CHUNK), :] = (q[i] * jnp.exp(cg_col[i])).astype(BF16)
            kd_s[d, pl.ds(r0, CHUNK), :] = (k[i] * jnp.exp(g_last[i] - cg_col[i])).astype(BF16)
            gl_s[d, pl.ds(r8, 8), :] = jnp.broadcast_to(jnp.exp(g_last[i]), (8, SEG))
        return carry

    lax.fori_loop(0, n_chunks // group, solve_group, 0)

    def scan(n, carry):
        D2 = range(2)
        rows = [_chunk_rows(d, n, n_chunks) for d in D2]
        rows8 = [pl.multiple_of((n if d == 0 else n_chunks - 1 - n) * 8, 8) for d in D2]
        s = [st_s[d] for d in D2]
        ws = [_mm1(w_s[d, pl.ds(rows[d], CHUNK), :], s[d]) for d in D2]
        qs_ = [_mm1(qg_s[d, pl.ds(rows[d], CHUNK), :], s[d]) for d in D2]
        v_new = [u_s[d, pl.ds(rows[d], CHUNK), :] - ws[d] for d in D2]
        v4 = [_block_diag(x_, hm) for x_ in v_new]
        av = [_mm1(att_s[d, pl.ds(rows[d], CHUNK), :], v4[d]) for d in D2]
        kv = [_mm1(kd_s[d, pl.ds(rows[d], CHUNK), :], v_new[d], TN) for d in D2]
        for d, o_s in enumerate((of_s, ob_s)):
            st_s[d] = s[d] * gl_s[d, pl.ds(rows8[d], 1), :] + jnp.where(hm, kv[d], 0.0)
            o_s[pl.ds(rows[d], CHUNK), :] = qs_[d] + av[d]
        return carry

    lax.fori_loop(0, n_chunks, scan, 0)
    if not has_cache:
        for d in range(2):
            for h in range(N_HEADS):
                sf_ref[d, h] = st_s[d, _hs(h), _hs(h)]
    _epilogue(T, of_s, ob_s, z_ref, _silu, bd_ref, ng_ref, o_ref)


def _deltanet(P, G, T, n_seq, gate_par, conv_w, ng, bd, l, state, prev):
    has_cache = state is not None
    st = (2, N_HEADS, HEAD_DIM, HEAD_DIM)
    in_specs = [_pick_spec((8, GATE_LANES), l), _pick_spec((3, 3 * SEG), l),
                _const_spec((1, SEG)), _const_spec((SEG, SEG))]
    args = [gate_par, conv_w, ng, bd]
    if has_cache:
        in_specs.append(_state_in_spec(l, st))
        args.append(state)
    scratch = ([pltpu.VMEM((T, SEG), F32)] * 2 + [pltpu.VMEM((2, SEG, SEG), F32)]
               + [pltpu.VMEM((T, SEG), F32)] * 3
               + [pltpu.VMEM((2, T, SEG), F32)] + [pltpu.VMEM((2, T, SEG), BF16)] * 4
               + [pltpu.VMEM((2, T // CHUNK * 8, SEG), F32)])
    return _mixer_call(functools.partial(_delta_kernel, T=T, has_cache=has_cache), "deltanet", T, n_seq,
                       P, (S_AQ, S_AK, S_AV, S_AZ), G,
                       in_specs, args, scratch, () if has_cache else (st,), l, prev)


def _diff_kernel(*refs, T, has_cache, lam_init):
    if has_cache:
        (q_ref, k_ref, v_ref, qg_ref, kg_ref, lam_ref, ng_ref, bd32_ref, bd64_ref,
         cos_ref, sin_ref, ck_ref, cv_ref, o_ref, qs, kh, vt) = refs
    else:
        (q_ref, k_ref, v_ref, qg_ref, kg_ref, lam_ref, ng_ref, bd32_ref, bd64_ref,
         o_ref, ko_ref, vo_ref, qs, kh, vt) = refs
    n_blk = T // ROW_BLOCK
    s0 = ck_ref.shape[1] if has_cache else 0
    bd32 = bd32_ref[...]
    lane = lax.broadcasted_iota(jnp.int32, (1, SEG), 1)
    first_half = (lane % 16) < 8

    if has_cache:
        for h in range(N_HEADS):
            kh[h, 0:s0, :] = ck_ref[h]
            vt[h, :, 0:s0] = cv_ref[h].T

    for i in range(n_blk):
        r0 = i * ROW_BLOCK
        for src, g_ref in ((q_ref, qg_ref), (k_ref, kg_ref)):
            x = src[pl.ds(r0, ROW_BLOCK), :]
            y = x * lax.rsqrt(_gsum(x * x, bd32) * (1.0 / DQK) + EPS) * g_ref[...]
            if has_cache:
                partner = jnp.where(first_half, pltpu.roll(y, SEG - 8, axis=1), pltpu.roll(y, 8, axis=1))
                y = y * cos_ref[pl.ds(r0, ROW_BLOCK), :] + partner * sin_ref[pl.ds(r0, ROW_BLOCK), :]
            if src is q_ref:
                qs[pl.ds(r0, ROW_BLOCK), :] = y
            else:
                for h in range(N_HEADS):
                    kh[h, pl.ds(s0 + r0, ROW_BLOCK), :] = y[:, _hs(h)]
                    if not has_cache:
                        ko_ref[h, pl.ds(r0, ROW_BLOCK), :] = y[:, _hs(h)]
        xv = v_ref[pl.ds(r0, ROW_BLOCK), :]
        xvt = xv.T
        for h in range(N_HEADS):
            vt[h, :, pl.ds(s0 + r0, ROW_BLOCK)] = xvt[_hs(h), :]
            if not has_cache:
                vo_ref[h, pl.ds(r0, ROW_BLOCK), :] = xv[:, _hs(h)]

    lp = lam_ref[...]
    lam = (jnp.exp(jnp.sum(lp[0:1, :] * lp[1:2, :], axis=1, keepdims=True))
           - jnp.exp(jnp.sum(lp[2:3, :] * lp[3:4, :], axis=1, keepdims=True)) + lam_init)
    scale = DQK ** -0.5 * math.log2(math.e)
    comp_rows = lax.broadcasted_iota(jnp.int32, (HEAD_DIM, 1), 0) < DQK
    bd64 = bd64_ref[...]
    ng = ng_ref[...]

    def qblock(i, carry):
        r0 = pl.multiple_of(i * Q_BLOCK, Q_BLOCK)
        qt = (qs[pl.ds(r0, Q_BLOCK), :] * scale).T
        qct = []
        for h in range(N_HEADS):
            qh = qt[_hs(h), :]
            q1 = jnp.where(comp_rows, qh, 0.0)
            qct += [q1, qh - q1]
        n_units = len(qct)
        st = _mm1(kh[0], qct[0])
        res = []
        for u in range(n_units):
            nxt = _mm1(kh[(u + 1) // 2], qct[u + 1]) if u + 1 < n_units else None
            e = jnp.exp2(st - st.max(axis=0, keepdims=True))
            res.append(_mm1(vt[u // 2], e) * (1.0 / e.sum(axis=0, keepdims=True)))
            st = nxt
        outs = [res[2 * h] - lam * res[2 * h + 1] for h in range(N_HEADS)]
        o = jnp.concatenate(outs, axis=0).T
        o_ref[pl.ds(r0, Q_BLOCK), :] = _head_norm_gate(o, bd64, ng, 1.0 - lam_init)
        return carry

    lax.fori_loop(0, T // Q_BLOCK, qblock, 0)


def _diffattn(P, T, n_seq, qg, kg, lam_par, ng, bd32, bd64, l, rope, cache, prev):
    has_cache = cache is not None
    lam_init = 0.8 - 0.6 * math.exp(-0.3 * l)
    kv = (N_HEADS, T, HEAD_DIM)
    in_specs = [_const_spec((1, SEG)), _const_spec((1, SEG)), _pick_spec((4, DQK), l), _const_spec((1, SEG)),
                _const_spec((SEG, SEG)), _const_spec((SEG, SEG))]
    args = [qg, kg, lam_par, ng, bd32, bd64]
    if has_cache:
        ckv = cache[0].shape[2:]
        in_specs += [_const_spec((T, SEG)), _const_spec((T, SEG)), _state_in_spec(l, ckv), _state_in_spec(l, ckv)]
        args += [rope[0], rope[1], cache[0], cache[1]]
    n_keys = T + (cache[0].shape[3] if has_cache else 0)
    scratch = [pltpu.VMEM((T, SEG), F32), pltpu.VMEM((N_HEADS, n_keys, HEAD_DIM), F32),
               pltpu.VMEM((N_HEADS, HEAD_DIM, n_keys), F32)]
    return _mixer_call(functools.partial(_diff_kernel, T=T, has_cache=has_cache, lam_init=lam_init),
                       "diff_attn", T, n_seq, P, (S_BQ, S_BK, S_BV), None,
                       in_specs, args, scratch, () if has_cache else (kv, kv), l, prev)


def _rope_tables(T):
    n_freq = DQK // 4
    t = jnp.arange(T)
    rows = (t // GRID_W).astype(F32)
    cols = (t % GRID_W).astype(F32)
    freqs = ROPE_BASE ** (-jnp.arange(n_freq, dtype=F32) / n_freq)
    ang_r, ang_c = rows[:, None] * freqs, cols[:, None] * freqs

    def comp(fn, sign):
        return jnp.concatenate([fn(ang_r), sign * fn(ang_r), fn(ang_c), sign * fn(ang_c)], axis=1)

    reps = SEG // DQK
    cos = jnp.tile(comp(jnp.cos, 1.0), (1, reps))
    sin = jnp.tile(jnp.concatenate([-jnp.sin(ang_r), jnp.sin(ang_r), -jnp.sin(ang_c), jnp.sin(ang_c)], axis=1),
                   (1, reps))
    return cos, sin


def _block_ones(group):
    i = np.arange(SEG)
    return jnp.asarray(i[:, None] // group == i[None, :] // group, BF16)


def kernel(x_prompt, x_sample, cache_diff_k, cache_diff_v, state_delta, state_mlstm_C, state_mlstm_n, state_mlstm_m, state_ret, c, c_ctx, w_ada, b_ada, norm_g, ffn_w_gate, ffn_w_up, ffn_w_down, w_in, dn_conv_w, dn_a_log, dn_dt_bias, dn_norm_g, da_qn_g, da_kn_g, da_lambda, da_norm_g, ml_i_bias, ml_f_bias, ml_norm_g, ret_decay_logit, ret_norm_g, w_branch, w_out):
    B, T, _ = x_prompt.shape
    Bs, Ts, _ = x_sample.shape
    cond8 = jnp.concatenate([c_ctx[None], c, jnp.zeros((8 - 1 - Bs, D_MODEL), F32)], axis=0)
    mod = _ada(cond8, w_ada, b_ada).reshape(DEPTH, 8, N_MOD, D_MODEL)
    bd32, bd64 = _block_ones(DQK), _block_ones(HEAD_DIM)
    rope = _rope_tables(Ts)
    tile_heads = lambda g: jnp.tile(g, SEG // g.shape[0])[None]

    wg, wu, wd = ffn_w_gate.astype(BF16), ffn_w_up.astype(BF16), ffn_w_down.astype(BF16)
    cols = lambda f: w_in[:, :, _IN_OFFS[f]:_IN_OFFS[f + 1]]
    w_mix = jnp.concatenate([cols(f) for f in _SEG_FIELDS], axis=2).astype(BF16)
    w_gate = jnp.concatenate([cols(f) for f in _GATE_FIELDS]
                             + [jnp.zeros((DEPTH, D_MODEL, GATE_LANES - 32), F32)], axis=2).astype(BF16)
    w_merge = cols(19).astype(BF16)
    wb, wo = w_branch.astype(BF16), w_out.astype(BF16)
    norm_g4 = norm_g.reshape(DEPTH, 3, 1, D_MODEL)
    lanes = lambda *parts: jnp.concatenate([p.reshape(DEPTH, 1, -1) for p in parts], axis=2)
    z8 = jnp.zeros((DEPTH, 8), F32)
    gate_par = jnp.concatenate([
        lanes(z8, dn_dt_bias, ml_i_bias, ml_f_bias, jnp.zeros((DEPTH, GATE_LANES - 32), F32)),
        lanes(z8, dn_a_log, jnp.zeros((DEPTH, GATE_LANES - 16), F32)),
        jnp.zeros((DEPTH, 6, GATE_LANES), F32)], axis=1)
    ml_state = (state_mlstm_C, state_mlstm_n.reshape(Bs, DEPTH, 2, 1, SEG),
                jnp.repeat(state_mlstm_m, HEAD_DIM, axis=-1).reshape(Bs, DEPTH, 2, 1, SEG))

    xs = {"ctx": x_prompt.reshape(B * T, D_MODEL), "smp": x_sample.reshape(Bs * Ts, D_MODEL)}
    geo = {"ctx": (T, B, B * T, 0), "smp": (Ts, Bs, Ts, 1)}
    states = {}
    for l in range(DEPTH):
        for path in ("ctx", "smp"):
            Tp, n_seq, rows_per_cond, first_cond = geo[path]
            mspec = _mod_spec(l, rows_per_cond, first_cond)
            smp = path == "smp"
            prev = states.get if not smp else (lambda name: None)
            x = _ffn(xs[path], mod, mspec, norm_g4, wg, wu, wd, l, 0)
            P, G = _inproj(x, mod, mspec, norm_g4, w_mix, w_gate, l)
            ra = _deltanet(P, G, Tp, n_seq, gate_par, dn_conv_w, tile_heads(dn_norm_g[l]), bd64, l,
                           state_delta if smp else None, prev("dn"))
            rb = _diffattn(P, Tp, n_seq, tile_heads(da_qn_g[l]), tile_heads(da_kn_g[l]), da_lambda,
                           tile_heads(da_norm_g[l]), bd32, bd64, l, rope if smp else None,
                           (cache_diff_k, cache_diff_v) if smp else None, prev("kv"))
            rc = _mlstm(P, G, Tp, n_seq, gate_par, tile_heads(ml_norm_g[l]), bd64, l,
                        ml_state if smp else None, prev("ml"))
            rd = _retention(P, Tp, n_seq, ret_decay_logit[l], tile_heads(ret_norm_g[l]), bd64, l,
                            state_ret if smp else None, prev("r"))
            if not smp:
                states = {"dn": ra[1:], "kv": rb[1:], "ml": rc[1:], "r": rd[1:]}
            x = _merge(x, mod, mspec, norm_g4, (ra[0], rb[0], rc[0], rd[0]), w_merge, wb, wo, l)
            xs[path] = _ffn(x, mod, mspec, norm_g4, wg, wu, wd, l, 1)
    (new_dn,), (new_k, new_v), (new_c, new_n, new_m), (new_r,) = (states[k] for k in ("dn", "kv", "ml", "r"))
    per_head = lambda rows: rows.reshape(B, DEPTH, 2, N_HEADS, HEAD_DIM)
    return (xs["ctx"].reshape(B, T, D_MODEL), xs["smp"].reshape(Bs, Ts, D_MODEL),
            new_k, new_v, new_dn, new_c, per_head(new_n), per_head(new_m)[..., 0], new_r)
```

```python
import functools
import math

import numpy as np
import jax
import jax.numpy as jnp
from jax import lax
from jax.experimental import pallas as pl
from jax.experimental.pallas import tpu as pltpu

F32 = jnp.float32
BF16 = jnp.bfloat16

D_MODEL = 1024
FFN_DIM = 2816
N_MOD = 9
DEPTH = 2
N_HEADS = 4
HEAD_DIM = 64
SEG = N_HEADS * HEAD_DIM
N_SEG = 15
CHUNK = 64
DQK = 32
GRID_W = 64
ROPE_BASE = 10000.0
EPS = 1e-6
N_BRANCH = 4
GATE_LANES = 128
NEG = -1e30

TM = 512
Q_BLOCK = 256
ROW_BLOCK = 128
SEQS_PER_STEP = 4
VMEM_LIMIT = 56 * 1024 * 1024

NN = (((1,), (0,)), ((), ()))
NT = (((1,), (1,)), ((), ()))
TN = (((0,), (0,)), ((), ()))

_IN_SIZES = (256, 256, 256, 256, 8, 8, 256, 256, 256, 256, 256, 256, 256, 8, 8, 256, 256, 256, 256, 4096)
_IN_OFFS = np.concatenate([[0], np.cumsum(_IN_SIZES)]).tolist()
_SEG_FIELDS = (0, 1, 2, 3, 6, 7, 8, 9, 10, 11, 12, 15, 16, 17, 18)
_GATE_FIELDS = (4, 5, 13, 14)
(S_AQ, S_AK, S_AV, S_AZ, S_BQ, S_BK, S_BV, S_CQ, S_CK, S_CV, S_CO, S_DQ, S_DK, S_DV, S_DG) = range(N_SEG)
L_BETA, L_ALPHA, L_CI, L_CF = 0, 8, 16, 24


def _dg(a, b, dims):
    return lax.dot_general(a, b, dims, preferred_element_type=F32)


def _split2(x):
    hi = x.astype(BF16)
    lo = (x - hi.astype(F32)).astype(BF16)
    return hi, lo


def _mm1(a, b, dims=NN):
    return _dg(a.astype(BF16), b.astype(BF16), dims)


def _mm3(a, b, dims=NN):
    ah, al = _split2(a)
    bh, bl = _split2(b)
    return _dg(ah, bh, dims) + (_dg(ah, bl, dims) + _dg(al, bh, dims))


def _mm2r(a, b, dims=NN):
    ah = a.astype(BF16)
    bh, bl = _split2(b)
    return _dg(ah, bh, dims) + _dg(ah, bl, dims)


def _sel_mm(sel, x, dims=NN):
    h0 = x.astype(BF16)
    r1 = x - h0.astype(F32)
    h1 = r1.astype(BF16)
    h2 = (r1 - h1.astype(F32)).astype(BF16)
    return _dg(sel, h0, dims) + (_dg(sel, h1, dims) + _dg(sel, h2, dims))


def _gsum(x, bd):
    hi, lo = _split2(x)
    return _dg(hi, bd, NN) + _dg(lo, bd, NN)


def _sigmoid(x):
    return 1.0 / (1.0 + jnp.exp(-x))


def _silu(x):
    return x * _sigmoid(x)


def _softplus(x):
    return jnp.maximum(x, 0.0) + jnp.log1p(jnp.exp(-jnp.abs(x)))


def _log_sigmoid(x):
    return -_softplus(-x)


def _norm_mod(x, g, shift, scale):
    ms = jnp.mean(x * x, axis=-1, keepdims=True)
    return (x * lax.rsqrt(ms + EPS) * g) * (1.0 + scale) + shift


def _tri_masks():
    ii = lax.broadcasted_iota(jnp.int32, (CHUNK, CHUNK), 0)
    jj = lax.broadcasted_iota(jnp.int32, (CHUNK, CHUNK), 1)
    return ((jj <= ii, jj < ii), (jj >= ii, jj > ii))


def _cparams(n_grid=1):
    return pltpu.CompilerParams(dimension_semantics=("arbitrary",) * n_grid,
                                vmem_limit_bytes=VMEM_LIMIT)


def _const_spec(shape):
    nd = len(shape)
    return pl.BlockSpec(shape, lambda *_: (0,) * nd)


def _ada_kernel(s_ref, w_ref, b_ref, o_ref):
    s = s_ref[...]
    o_ref[...] = _mm3(_silu(s), w_ref[...]) + b_ref[...]


def _ada(cond8, w_ada, b_ada):
    tn = 1536
    n_t = (N_MOD * D_MODEL) // tn
    return pl.pallas_call(
        _ada_kernel,
        grid=(DEPTH, n_t),
        in_specs=[pl.BlockSpec((8, D_MODEL), lambda l, j: (0, 0)),
                  pl.BlockSpec((None, D_MODEL, tn), lambda l, j: (l, 0, j)),
                  pl.BlockSpec((None, 1, tn), lambda l, j: (l, 0, j))],
        out_specs=pl.BlockSpec((None, 8, tn), lambda l, j: (l, 0, j)),
        out_shape=jax.ShapeDtypeStruct((DEPTH, 8, N_MOD * D_MODEL), F32),
        compiler_params=_cparams(2),
        name="ada_mod",
    )(cond8, w_ada, b_ada.reshape(DEPTH, 1, N_MOD * D_MODEL))


def _ffn_kernel(x_ref, mod_ref, g_ref, wg_ref, wu_ref, wd_ref, o_ref, *, mi):
    x = x_ref[...]
    h = _norm_mod(x, g_ref[...], mod_ref[mi:mi + 1, :], mod_ref[mi + 1:mi + 2, :]).astype(BF16)
    gate = jnp.dot(h, wg_ref[...], preferred_element_type=F32)
    up = jnp.dot(h, wu_ref[...], preferred_element_type=F32)
    act = (_silu(gate) * up).astype(BF16)
    y = jnp.dot(act, wd_ref[...], preferred_element_type=F32)
    o_ref[...] = x + (0.5 * mod_ref[mi + 2:mi + 3, :]) * y


def _pick_spec(tail, *lead):
    nd = len(tail)
    return pl.BlockSpec((None,) * len(lead) + tuple(tail), lambda *_: tuple(lead) + (0,) * nd,
                        pipeline_mode=pl.Buffered(1))


def _mod_spec(l, rows_per_cond, first_cond):
    per = rows_per_cond // TM
    return pl.BlockSpec((None, None, N_MOD, D_MODEL), lambda i: (l, first_cond + i // per, 0, 0))


def _ffn(x, mod, mod_spec, norm_g, wg, wu, wd, l, j):
    n = x.shape[0]
    return pl.pallas_call(
        functools.partial(_ffn_kernel, mi=6 * j),
        grid=(n // TM,),
        in_specs=[pl.BlockSpec((TM, D_MODEL), lambda i: (i, 0)),
                  mod_spec,
                  _pick_spec((1, D_MODEL), l, 2 * j),
                  _pick_spec((D_MODEL, FFN_DIM), l, j),
                  _pick_spec((D_MODEL, FFN_DIM), l, j),
                  _pick_spec((FFN_DIM, D_MODEL), l, j)],
        out_specs=pl.BlockSpec((TM, D_MODEL), lambda i: (i, 0)),
        out_shape=jax.ShapeDtypeStruct((n, D_MODEL), F32),
        compiler_params=_cparams(1),
        name="ffn",
    )(x, mod, norm_g, wg, wu, wd)


def _inproj_kernel(x_ref, mod_ref, g_ref, w_ref, wgt_ref, p_ref, gt_ref):
    h = _norm_mod(x_ref[...], g_ref[...], mod_ref[3:4, :], mod_ref[4:5, :]).astype(BF16)
    step = 3 * SEG
    for j in range(0, N_SEG * SEG, step):
        p_ref[:, j:j + step] = jnp.dot(h, w_ref[:, j:j + step], preferred_element_type=F32)
    gt_ref[...] = jnp.dot(h, wgt_ref[...], preferred_element_type=F32)


def _inproj(x, mod, mod_spec, norm_g, w_mix, w_gate, l):
    n = x.shape[0]
    return pl.pallas_call(
        _inproj_kernel,
        grid=(n // TM,),
        in_specs=[pl.BlockSpec((TM, D_MODEL), lambda i: (i, 0)),
                  mod_spec,
                  _pick_spec((1, D_MODEL), l, 1),
                  _pick_spec((D_MODEL, N_SEG * SEG), l),
                  _pick_spec((D_MODEL, GATE_LANES), l)],
        out_specs=[pl.BlockSpec((TM, N_SEG * SEG), lambda i: (i, 0)),
                   pl.BlockSpec((TM, GATE_LANES), lambda i: (i, 0))],
        out_shape=[jax.ShapeDtypeStruct((n, N_SEG * SEG), F32),
                   jax.ShapeDtypeStruct((n, GATE_LANES), F32)],
        compiler_params=_cparams(1),
        name="in_proj",
    )(x, mod, norm_g, w_mix, w_gate)


def _merge_kernel(x_ref, mod_ref, g_ref, ba_ref, bb_ref, bc_ref, bd_ref, wm_ref, wb_ref, wo_ref, o_ref):
    x = x_ref[...]
    h = _norm_mod(x, g_ref[...], mod_ref[3:4, :], mod_ref[4:5, :]).astype(BF16)
    mixed = None
    for m, b_ref in enumerate((ba_ref, bb_ref, bc_ref, bd_ref)):
        logits = jnp.dot(h, wm_ref[:, m * D_MODEL:(m + 1) * D_MODEL], preferred_element_type=F32)
        pb = jnp.dot(b_ref[...].astype(BF16), wb_ref[m], preferred_element_type=F32)
        term = _sigmoid(logits) * pb
        mixed = term if mixed is None else mixed + term
    y = jnp.dot(mixed.astype(BF16), wo_ref[...], preferred_element_type=F32)
    o_ref[...] = x + mod_ref[5:6, :] * y


def _merge(x, mod, mod_spec, norm_g, branches, w_merge, w_branch, w_out, l):
    n = x.shape[0]
    row = lambda w: pl.BlockSpec((TM, w), lambda i: (i, 0))
    return pl.pallas_call(
        _merge_kernel,
        grid=(n // TM,),
        in_specs=[row(D_MODEL), mod_spec, _pick_spec((1, D_MODEL), l, 1),
                  row(SEG), row(SEG), row(SEG), row(SEG),
                  _pick_spec((D_MODEL, N_BRANCH * D_MODEL), l),
                  _pick_spec((N_BRANCH, SEG, D_MODEL), l),
                  _pick_spec((D_MODEL, D_MODEL), l)],
        out_specs=row(D_MODEL),
        out_shape=jax.ShapeDtypeStruct((n, D_MODEL), F32),
        compiler_params=_cparams(1),
        name="merge",
    )(x, mod, norm_g, *branches, w_merge, w_branch, w_out)


def _state_in_spec(l, tail):
    nd = len(tail)
    return pl.BlockSpec((None, None) + tail, lambda s: (s, l) + (0,) * nd)


def _per_sequence(kernel, T, n_sub, n_rows_in, n_in, n_alias, n_state, l, creates, *refs):
    ins, rest = refs[:n_in], refs[n_in + n_alias:]
    main, states, scratch = rest[0], rest[1:1 + n_state], rest[1 + n_state:]
    if creates:
        for r in states:
            for other in range(DEPTH):
                if other != l:
                    r[:, other] = jnp.zeros((n_sub,) + r.shape[2:], F32)

    def one(sub, carry):
        rows = pl.ds(pl.multiple_of(sub * T, T), T)
        own = [r.at[sub, l] if creates else r.at[sub] for r in states]
        kernel(*[r.at[rows] for r in ins[:n_rows_in]], *ins[n_rows_in:], main.at[rows], *own, *scratch)
        return carry

    lax.fori_loop(0, n_sub, one, 0)


def _all_sequences(kernel, n_sub, n_in, n_alias, n_state, l, creates, *refs):
    ins, rest = refs[:n_in], refs[n_in + n_alias:]
    main, states, scratch = rest[0], rest[1:1 + n_state], rest[1 + n_state:]
    if creates:
        for r in states:
            for other in range(DEPTH):
                if other != l:
                    r[:, other] = jnp.zeros((n_sub,) + r.shape[2:], F32)
    own = [r.at[sub, l] if creates else r.at[sub] for r in states for sub in range(n_sub)]
    kernel(*ins, main, *own, *scratch)


def _mixer_call(kernel, name, T, n_seq, P, segs, G, in_specs, args, scratch, state_tails=(), l=0, prev=None,
                interleaved=False):
    creates = prev is None
    n_sub = SEQS_PER_STEP if state_tails else 1
    rows = T * n_sub
    row_specs = [pl.BlockSpec((rows, SEG), lambda s, j=j: (s, j)) for j in segs]
    row_args = [P] * len(segs)
    if G is not None:
        row_specs.append(pl.BlockSpec((rows, GATE_LANES), lambda s: (s, 0)))
        row_args.append(G)
    in_specs, args = row_specs + in_specs, row_args + args
    n_in, n_alias = len(args), 0 if creates else len(prev)

    def state_spec(t):
        if creates:
            return pl.BlockSpec((n_sub, DEPTH) + t, lambda s: (s,) + (0,) * (1 + len(t)))
        return pl.BlockSpec((n_sub, None) + t, lambda s: (s, l) + (0,) * len(t))

    out_specs = [pl.BlockSpec((rows, SEG), lambda s: (s, 0))] + [state_spec(t) for t in state_tails]
    out_shape = ([jax.ShapeDtypeStruct((n_seq * T, SEG), F32)]
                 + [jax.ShapeDtypeStruct((n_seq, DEPTH) + t, F32) for t in state_tails])
    aliases = {}
    if not creates:
        in_specs = in_specs + [pl.BlockSpec(memory_space=pl.ANY)] * n_alias
        args = args + list(prev)
        aliases = {n_in + i: 1 + i for i in range(n_alias)}
    if interleaved:
        body = functools.partial(_all_sequences, kernel, n_sub, n_in, n_alias, len(state_tails), l, creates)
    else:
        body = functools.partial(_per_sequence, kernel, T, n_sub, len(row_args), n_in, n_alias,
                                 len(state_tails), l, creates)
    return pl.pallas_call(
        body, grid=(n_seq // n_sub,), in_specs=in_specs, out_specs=out_specs, out_shape=out_shape,
        scratch_shapes=scratch, input_output_aliases=aliases, compiler_params=_cparams(1), name=name)(*args)


def _head_norm_gate(o, bd, ng, gate):
    ss = _gsum(o * o, bd)
    return o * lax.rsqrt(ss * (1.0 / HEAD_DIM) + EPS) * ng * gate


def _epilogue(T, of_s, ob_s, gate_ref, gate_fn, bd_ref, ng_ref, o_ref):
    rb = 256
    bd = bd_ref[...]
    ng = ng_ref[...]

    def blk(i, carry):
        r0 = pl.multiple_of(i * rb, rb)
        o = of_s[pl.ds(r0, rb), :] + ob_s[pl.ds(r0, rb), :]
        o_ref[pl.ds(r0, rb), :] = _head_norm_gate(o, bd, ng, gate_fn(gate_ref[pl.ds(r0, rb), :]))
        return carry

    lax.fori_loop(0, T // rb, blk, 0)


def _chunk_rows(d, n, n_chunks, base=0):
    c = n if d == 0 else n_chunks - 1 - n
    return pl.multiple_of(base + c * CHUNK, CHUNK)


def _hs(h):
    return slice(h * HEAD_DIM, (h + 1) * HEAD_DIM)


def _head_mask():
    ii = lax.broadcasted_iota(jnp.int32, (SEG, SEG), 0) // HEAD_DIM
    jj = lax.broadcasted_iota(jnp.int32, (SEG, SEG), 1) // HEAD_DIM
    return ii == jj


def _blocksum(x):
    c = CHUNK
    return (x[0:c] + x[c:2 * c]) + (x[2 * c:3 * c] + x[3 * c:4 * c])


def _lane_head():
    return lax.broadcasted_iota(jnp.int32, (1, SEG), 1) // HEAD_DIM


def _tiled_masks():
    ii = lax.broadcasted_iota(jnp.int32, (CHUNK, SEG), 0)
    jj = lax.broadcasted_iota(jnp.int32, (CHUNK, SEG), 1) % HEAD_DIM
    return ((jj <= ii, jj < ii), (jj >= ii, jj > ii)), jj == ii


def _block_diag(x, hm):
    return jnp.where(hm, jnp.concatenate([x] * N_HEADS, axis=0), 0.0)


def _col_dense(g, lane0, lane_head):
    out = jnp.broadcast_to(g[:, lane0:lane0 + 1], (CHUNK, SEG))
    for h in range(1, N_HEADS):
        out = jnp.where(lane_head == h, g[:, lane0 + h:lane0 + h + 1], out)
    return out


def _diag_row(col_dense, eye_t):
    return jnp.sum(jnp.where(eye_t, col_dense, 0.0), axis=0, keepdims=True)


def _seg_max(x, lane_head):
    out = None
    for h in range(N_HEADS):
        m = jnp.max(jnp.where(lane_head == h, x, NEG), axis=1, keepdims=True)
        out = m if out is None else jnp.where(lane_head == h, m, out)
    return jnp.broadcast_to(out, x.shape)


UNITS = tuple((d, h) for d in range(2) for h in range(N_HEADS))


def _ret_kernel(*refs, T, has_cache, n_sub):
    if has_cache:
        (q_ref, k_ref, v_ref, gt_ref, dl_ref, ng_ref, bd_ref, r0_ref, o_ref,
         of_s, ob_s, st_s, dec_s, qdec_s, kdec_s, cdec_s) = refs
    else:
        q_ref, k_ref, v_ref, gt_ref, dl_ref, ng_ref, bd_ref, o_ref = refs[:8]
        rf_refs = refs[8:8 + n_sub]
        of_s, ob_s, st_s, dec_s, qdec_s, kdec_s, cdec_s = refs[8 + n_sub:]
    n_chunks = T // CHUNK
    units = [(s, d) for s in range(n_sub) for d in range(2)]
    masks = _tri_masks()
    hm = _head_mask()

    @pl.when(pl.program_id(0) == 0)
    def _():
        ii = lax.broadcasted_iota(jnp.int32, (CHUNK, CHUNK), 0)
        jj = lax.broadcasted_iota(jnp.int32, (CHUNK, CHUNK), 1)
        rel = (ii - jj).astype(F32)
        pos = lax.broadcasted_iota(jnp.int32, (CHUNK, 1), 0).astype(F32)
        lg_all = _log_sigmoid(dl_ref[...])
        for d in range(2):
            dec, qdec, kdec, cdec = [], [], [], []
            for h in range(N_HEADS):
                lg = lg_all[d:d + 1, h:h + 1]
                if d == 0:
                    e, qd, kd = rel * lg, (pos + 1.0) * lg, (CHUNK - 1.0 - pos) * lg
                else:
                    e, qd, kd = -rel * lg, (CHUNK - pos) * lg, pos * lg
                dec.append(jnp.exp(jnp.where(masks[d][0], e, NEG)))
                qdec.append(jnp.broadcast_to(jnp.exp(qd), (CHUNK, HEAD_DIM)))
                kdec.append(jnp.broadcast_to(jnp.exp(kd), (CHUNK, HEAD_DIM)))
                cdec.append(jnp.broadcast_to(jnp.exp(CHUNK * lg), (HEAD_DIM, SEG)))
            dec_s[d] = jnp.concatenate(dec, axis=0)
            qdec_s[d] = jnp.concatenate(qdec, axis=1)
            kdec_s[d] = jnp.concatenate(kdec, axis=1)
            cdec_s[d] = jnp.concatenate(cdec, axis=0)

    for s, d in units:
        st_s[s, d] = jnp.zeros((SEG, SEG), F32)
        if has_cache:
            for h in range(N_HEADS):
                st_s[s, d, _hs(h), _hs(h)] = r0_ref[d, h]

    def body(n, carry):
        rows = [_chunk_rows(d, n, n_chunks, s * T) for s, d in units]
        q = [q_ref[pl.ds(r0, CHUNK), :] for r0 in rows]
        k = [k_ref[pl.ds(r0, CHUNK), :] * (HEAD_DIM ** -0.5) for r0 in rows]
        v = [v_ref[pl.ds(r0, CHUNK), :] for r0 in rows]
        st = [st_s[u] for u in units]
        q4 = [jnp.where(hm, jnp.concatenate([x] * N_HEADS, axis=0), 0.0) for x in q]
        qk = [_mm1(a, b, NT) for a, b in zip(q4, k)]
        qr = [_mm1(q[i] * qdec_s[d], st[i]) for i, (s, d) in enumerate(units)]
        kv = [_mm1(k[i] * kdec_s[d], v[i], TN) for i, (s, d) in enumerate(units)]
        av = [_mm1(qk[i] * dec_s[d], v[i]) for i, (s, d) in enumerate(units)]
        for i, (s, d) in enumerate(units):
            st_s[s, d] = cdec_s[d] * st[i] + jnp.where(hm, kv[i], 0.0)
            (of_s, ob_s)[d][pl.ds(rows[i], CHUNK), :] = qr[i] + _blocksum(jnp.where(hm, av[i], 0.0))
        return carry

    lax.fori_loop(0, n_chunks, body, 0)
    if not has_cache:
        for s, d in units:
            for h in range(N_HEADS):
                rf_refs[s][d, h] = st_s[s, d, _hs(h), _hs(h)]
    _epilogue(T * n_sub, of_s, ob_s, gt_ref, _silu, bd_ref, ng_ref, o_ref)


def _retention(P, T, n_seq, decay_logit, ng, bd, l, state, prev):
    has_cache = state is not None
    st = (2, N_HEADS, HEAD_DIM, HEAD_DIM)
    in_specs = [_const_spec((2, N_HEADS)), _const_spec((1, SEG)), _const_spec((SEG, SEG))]
    args = [decay_logit, ng, bd]
    if has_cache:
        in_specs.append(_state_in_spec(l, st))
        args.append(state)
    n_sub = 1 if has_cache else SEQS_PER_STEP
    scratch = [pltpu.VMEM((n_sub * T, SEG), F32), pltpu.VMEM((n_sub * T, SEG), F32),
               pltpu.VMEM((n_sub, 2, SEG, SEG), F32),
               pltpu.VMEM((2, SEG, HEAD_DIM), F32), pltpu.VMEM((2, CHUNK, SEG), F32),
               pltpu.VMEM((2, CHUNK, SEG), F32), pltpu.VMEM((2, SEG, SEG), F32)]
    return _mixer_call(functools.partial(_ret_kernel, T=T, has_cache=has_cache, n_sub=n_sub), "retention",
                       T, n_seq, P, (S_DQ, S_DK, S_DV, S_DG), None,
                       in_specs, args, scratch, () if has_cache else (st,), l, prev, interleaved=True)


def _mlstm_kernel(*refs, T, has_cache, n_sub):
    if has_cache:
        (q_ref, k_ref, v_ref, og_ref, gt_ref, gp_ref, ng_ref, bd_ref, c0_ref, n0_ref, m0_ref,
         o_ref, of_s, ob_s, c_s, n_s, m_s) = refs
    else:
        q_ref, k_ref, v_ref, og_ref, gt_ref, gp_ref, ng_ref, bd_ref, o_ref = refs[:9]
        cf_refs, nf_refs, mf_refs = (refs[9 + i * n_sub:9 + (i + 1) * n_sub] for i in range(3))
        of_s, ob_s, c_s, n_s, m_s = refs[9 + 3 * n_sub:]
    n_chunks = T // CHUNK
    units = [(s, d) for s in range(n_sub) for d in range(2)]
    masks = _tri_masks()
    tri = (masks[0][0].astype(BF16), masks[1][0].astype(BF16))
    masks_t, eye_t = _tiled_masks()
    tri_t = [masks_t[d][0] for d in range(2)]
    hm = _head_mask()
    lane_head = _lane_head()
    bd = bd_ref[...]
    bias = gp_ref[0:1, :]

    for s, d in units:
        c_s[s, d] = jnp.zeros((SEG, SEG), F32)
        if has_cache:
            for h in range(N_HEADS):
                c_s[s, d, _hs(h), _hs(h)] = c0_ref[d, h]
            n_s[s, d] = n0_ref[d]
            m_s[s, d] = m0_ref[d]
        else:
            n_s[s, d] = jnp.zeros((1, SEG), F32)
            m_s[s, d] = jnp.zeros((1, SEG), F32)

    def body(n, carry):
        rows = [_chunk_rows(d, n, n_chunks, s * T) for s, d in units]
        q = [q_ref[pl.ds(r0, CHUNK), :] * (HEAD_DIM ** -0.5) for r0 in rows]
        k = [k_ref[pl.ds(r0, CHUNK), :] for r0 in rows]
        v = [v_ref[pl.ds(r0, CHUNK), :] for r0 in rows]
        pre = [gt_ref[pl.ds(r0, CHUNK), :] + bias for r0 in rows]
        b = [_sel_mm(tri[d], _log_sigmoid(pre[i])) for i, (s, d) in enumerate(units)]
        c_prev = [c_s[u] for u in units]
        n_prev = [n_s[u] for u in units]

        k4 = [_block_diag(x, hm) for x in k]
        v4 = [_block_diag(x, hm) for x in v]
        qk = [_mm1(a_, b_, NT) for a_, b_ in zip(q, k4)]
        qc = [_mm1(a_, b_) for a_, b_ in zip(q, c_prev)]
        qn = [_gsum(a_ * b_, bd) for a_, b_ in zip(q, n_prev)]

        sc, w_inter, m_i, wk, dec, m_new = [], [], [], [], [], []
        for i, (s, d) in enumerate(units):
            b_col = _col_dense(b[i], L_CF + d * N_HEADS, lane_head)
            ig_col = _col_dense(pre[i], L_CI + d * N_HEADS, lane_head)
            b_row, ig_row = _diag_row(b_col, eye_t), _diag_row(ig_col, eye_t)
            last = CHUNK - 1 if d == 0 else 0
            b_last = b_col[last:last + 1, :]
            m_prev = m_s[s, d]
            dm = jnp.where(tri_t[d], b_col - b_row + ig_row, NEG)
            inter = b_col + m_prev
            mi = jnp.maximum(inter, _seg_max(dm, lane_head))
            sc.append(qk[i] * jnp.exp(dm - mi))
            w_inter.append(jnp.exp(inter - mi))
            m_i.append(mi)
            mn = jnp.maximum(b_last + m_prev, _seg_max(b_last - b_row + ig_row, lane_head))
            wk.append(k[i] * jnp.exp(b_last - b_col + ig_col - mn))
            dec.append(jnp.exp(b_last + m_prev - mn))
            m_new.append(mn)

        sv = [_mm1(a_, b_) for a_, b_ in zip(sc, v4)]
        ssum = [_gsum(a_, bd) for a_ in sc]
        kv = [_mm1(a_, b_, TN) for a_, b_ in zip(wk, v)]
        for i, (s, d) in enumerate(units):
            num = w_inter[i] * qc[i] + sv[i]
            den = w_inter[i] * qn[i] + ssum[i]
            (of_s, ob_s)[d][pl.ds(rows[i], CHUNK), :] = num / jnp.maximum(jnp.abs(den), jnp.exp(-m_i[i]))
            c_s[s, d] = dec[i] * c_prev[i] + jnp.where(hm, kv[i], 0.0)
            n_s[s, d] = dec[i] * n_prev[i] + jnp.sum(wk[i], axis=0, keepdims=True)
            m_s[s, d] = m_new[i]
        return carry

    lax.fori_loop(0, n_chunks, body, 0)
    if not has_cache:
        for s, d in units:
            for h in range(N_HEADS):
                cf_refs[s][d, h] = c_s[s, d, _hs(h), _hs(h)]
            nf_refs[s][d] = n_s[s, d]
            mf_refs[s][d] = m_s[s, d]
    _epilogue(T * n_sub, of_s, ob_s, og_ref, _sigmoid, bd_ref, ng_ref, o_ref)


def _mlstm(P, G, T, n_seq, gate_par, ng, bd, l, state, prev):
    has_cache = state is not None
    st = (2, N_HEADS, HEAD_DIM, HEAD_DIM)
    rw = (2, 1, SEG)
    in_specs = [_pick_spec((8, GATE_LANES), l), _const_spec((1, SEG)), _const_spec((SEG, SEG))]
    args = [gate_par, ng, bd]
    if has_cache:
        in_specs += [_state_in_spec(l, st), _state_in_spec(l, rw), _state_in_spec(l, rw)]
        args += list(state)
    n_sub = 1 if has_cache else SEQS_PER_STEP
    scratch = [pltpu.VMEM((n_sub * T, SEG), F32), pltpu.VMEM((n_sub * T, SEG), F32),
               pltpu.VMEM((n_sub, 2, SEG, SEG), F32), pltpu.VMEM((n_sub,) + rw, F32), pltpu.VMEM((n_sub,) + rw, F32)]
    return _mixer_call(functools.partial(_mlstm_kernel, T=T, has_cache=has_cache, n_sub=n_sub), "mlstm",
                       T, n_seq, P, (S_CQ, S_CK, S_CV, S_CO), G,
                       in_specs, args, scratch, () if has_cache else (st, rw, rw), l, prev, interleaved=True)


INV_BASE = 8
SOLVE_GROUP = 4
_MM_INV = _mm1
_MM_APPLY = _mm2r


def _inverse_level_masks():
    ii = lax.broadcasted_iota(jnp.int32, (CHUNK, CHUNK), 0)
    jj = lax.broadcasted_iota(jnp.int32, (CHUNK, CHUNK), 1)
    out = []
    for lo, hi in ((jj, ii), (ii, jj)):
        lv = [(lo // INV_BASE == hi // INV_BASE) & (lo < hi)]
        size = 2 * INV_BASE
        while size <= CHUNK:
            lv.append((lo // size == hi // size) & (hi % size >= size // 2) & (lo % size < size // 2))
            size *= 2
        out.append(lv)
    return out


def _delta_kernel(*refs, T, has_cache):
    if has_cache:
        (q_ref, k_ref, v_ref, z_ref, gt_ref, gp_ref, cw_ref, ng_ref, bd_ref, s0_ref,
         o_ref, of_s, ob_s, st_s, qs, ks, vs, u_s, w_s, att_s, qg_s, kd_s, gl_s) = refs
    else:
        (q_ref, k_ref, v_ref, z_ref, gt_ref, gp_ref, cw_ref, ng_ref, bd_ref,
         o_ref, sf_ref, of_s, ob_s, st_s, qs, ks, vs, u_s, w_s, att_s, qg_s, kd_s, gl_s) = refs
    n_chunks = T // CHUNK
    group = min(SOLVE_GROUP, n_chunks)
    n_blk = T // ROW_BLOCK
    masks = _tri_masks()
    tri = (masks[0][0].astype(BF16), masks[1][0].astype(BF16))
    masks_t, eye_t = _tiled_masks()
    incl_t = [masks_t[d][0] for d in range(2)]
    strict_t = [masks_t[d][1] for d in range(2)]
    hm = _head_mask()
    lane_head = _lane_head()
    lvl_masks = _inverse_level_masks()
    eye64 = (masks[0][0] & masks[1][0]).astype(F32)
    bd = bd_ref[...]
    bias, a_log = gp_ref[0:1, :], gp_ref[1:2, :]
    row = lax.broadcasted_iota(jnp.int32, (ROW_BLOCK, 1), 0)

    def prologue(i, carry):
        r0 = pl.multiple_of(i * ROW_BLOCK, ROW_BLOCK)
        rp = pl.multiple_of(jnp.maximum(r0 - 8, 0), 8)
        rn = pl.multiple_of(jnp.minimum(r0 + ROW_BLOCK, T - 8), 8)
        for j, (src, dst) in enumerate(((q_ref, qs), (k_ref, ks), (v_ref, vs))):
            cur = src[pl.ds(r0, ROW_BLOCK), :]
            before = jnp.where(i > 0, src[pl.ds(rp, 8), :][7:8, :], 0.0)
            after = jnp.where(i < n_blk - 1, src[pl.ds(rn, 8), :][0:1, :], 0.0)
            down = jnp.where(row == 0, before, pltpu.roll(cur, 1, axis=0))
            up = jnp.where(row == ROW_BLOCK - 1, after, pltpu.roll(cur, ROW_BLOCK - 1, axis=0))
            w = cw_ref[:, j * SEG:(j + 1) * SEG]
            y = _silu(w[0:1, :] * down + w[1:2, :] * cur + w[2:3, :] * up)
            if j < 2:
                y = y * lax.rsqrt(_gsum(y * y, bd) + EPS)
            if j == 0:
                y = y * (HEAD_DIM ** -0.5)
            dst[pl.ds(r0, ROW_BLOCK), :] = y
        return carry

    lax.fori_loop(0, n_blk, prologue, 0)

    for d in range(2):
        st_s[d] = jnp.zeros((SEG, SEG), F32)
        if has_cache:
            for h in range(N_HEADS):
                st_s[d, _hs(h), _hs(h)] = s0_ref[d, h]

    def solve_group(g, carry):
        items = []
        q, k, v, beta, cg_col, g_last, decay, kb, k4 = ([] for _ in range(9))
        for cc in range(group):
            c = g * group + cc
            r0 = pl.multiple_of(c * CHUNK, CHUNK)
            qc, kc, vc = qs[pl.ds(r0, CHUNK), :], ks[pl.ds(r0, CHUNK), :], vs[pl.ds(r0, CHUNK), :]
            pre = gt_ref[pl.ds(r0, CHUNK), :]
            g_all = -jnp.exp(a_log) * _softplus(pre + bias)
            beta_all = _sigmoid(pre)
            kc4 = _block_diag(kc, hm)
            for d in range(2):
                items.append((d, r0, pl.multiple_of(c * 8, 8)))
                cg = _sel_mm(tri[d], g_all)
                col = _col_dense(cg, L_ALPHA + d * N_HEADS, lane_head)
                row_ = _diag_row(col, eye_t)
                bt = _col_dense(beta_all, L_BETA + d * N_HEADS, lane_head)
                q.append(qc), k.append(kc), v.append(vc), k4.append(kc4)
                beta.append(bt), cg_col.append(col), kb.append(kc * bt)
                g_last.append(col[CHUNK - 1:CHUNK, :] if d == 0 else col[0:1, :])
                decay.append(jnp.exp(jnp.where(incl_t[d], col - row_, NEG)))
        n_it = len(items)
        kk = [_mm1(kb[i], k4[i], NT) for i in range(n_it)]
        qk = [_mm1(q[i], k4[i], NT) for i in range(n_it)]
        a = [jnp.where(strict_t[items[i][0]], kk[i] * decay[i], 0.0) for i in range(n_it)]
        xu = [v[i] * beta[i] for i in range(n_it)]
        xw = [kb[i] * jnp.exp(cg_col[i]) for i in range(n_it)]

        units = [(i, h) for i in range(n_it) for h in range(N_HEADS)]
        dirs = [items[i][0] for i, h in units]
        ah = [a[i][:, _hs(h)] for i, h in units]
        x = [jnp.concatenate([xu[i][:, _hs(h)], xw[i][:, _hs(h)]], axis=1) for i, h in units]
        dg = [jnp.where(lvl_masks[d][0], a_, 0.0) for a_, d in zip(ah, dirs)]
        t = [eye64 - d_ for d_ in dg]
        p = [_MM_INV(d_, d_) for d_ in dg]
        pt = [_MM_INV(p_, t_) for p_, t_ in zip(p, t)]
        t = [t_ + u_ for t_, u_ in zip(t, pt)]
        p = [_MM_INV(p_, p_) for p_ in p]
        pt = [_MM_INV(p_, t_) for p_, t_ in zip(p, t)]
        t = [t_ + u_ for t_, u_ in zip(t, pt)]
        for lvl in range(1, len(lvl_masks[0])):
            lo = [jnp.where(lvl_masks[d][lvl], a_, 0.0) for a_, d in zip(ah, dirs)]
            lt = [_MM_INV(l_, t_) for l_, t_ in zip(lo, t)]
            tlt = [_MM_INV(t_, u_) for t_, u_ in zip(t, lt)]
            t = [t_ - u_ for t_, u_ in zip(t, tlt)]
        x = [_MM_APPLY(t_, x_) for t_, x_ in zip(t, x)]
        for i, (d, r0, r8) in enumerate(items):
            xi = x[i * N_HEADS:(i + 1) * N_HEADS]
            u_s[d, pl.ds(r0, CHUNK), :] = jnp.concatenate([x_[:, :HEAD_DIM] for x_ in xi], axis=1)
            w_s[d, pl.ds(r0, CHUNK), :] = jnp.concatenate([x_[:, HEAD_DIM:] for x_ in xi], axis=1).astype(BF16)
            att_s[d, pl.ds(r0, CHUNK), :] = (qk[i] * decay[i]).astype(BF16)
            qg_s[d, pl.ds(r0, CHUNK), :] = (q[i] * jnp.exp(cg_col[i])).astype(BF16)
            kd_s[d, pl.ds(r0, CHUNK), :] = (k[i] * jnp.exp(g_last[i] - cg_col[i])).astype(BF16)
            gl_s[d, pl.ds(r8, 8), :] = jnp.broadcast_to(jnp.exp(g_last[i]), (8, SEG))
        return carry

    lax.fori_loop(0, n_chunks // group, solve_group, 0)

    def scan(n, carry):
        D2 = range(2)
        rows = [_chunk_rows(d, n, n_chunks) for d in D2]
        rows8 = [pl.multiple_of((n if d == 0 else n_chunks - 1 - n) * 8, 8) for d in D2]
        s = [st_s[d] for d in D2]
        ws = [_mm1(w_s[d, pl.ds(rows[d], CHUNK), :], s[d]) for d in D2]
        qs_ = [_mm1(qg_s[d, pl.ds(rows[d], CHUNK), :], s[d]) for d in D2]
        v_new = [u_s[d, pl.ds(rows[d], CHUNK), :] - ws[d] for d in D2]
        v4 = [_block_diag(x_, hm) for x_ in v_new]
        av = [_mm1(att_s[d, pl.ds(rows[d], CHUNK), :], v4[d]) for d in D2]
        kv = [_mm1(kd_s[d, pl.ds(rows[d], CHUNK), :], v_new[d], TN) for d in D2]
        for d, o_s in enumerate((of_s, ob_s)):
            st_s[d] = s[d] * gl_s[d, pl.ds(rows8[d], 1), :] + jnp.where(hm, kv[d], 0.0)
            o_s[pl.ds(rows[d], CHUNK), :] = qs_[d] + av[d]
        return carry

    lax.fori_loop(0, n_chunks, scan, 0)
    if not has_cache:
        for d in range(2):
            for h in range(N_HEADS):
                sf_ref[d, h] = st_s[d, _hs(h), _hs(h)]
    _epilogue(T, of_s, ob_s, z_ref, _silu, bd_ref, ng_ref, o_ref)


def _deltanet(P, G, T, n_seq, gate_par, conv_w, ng, bd, l, state, prev):
    has_cache = state is not None
    st = (2, N_HEADS, HEAD_DIM, HEAD_DIM)
    in_specs = [_pick_spec((8, GATE_LANES), l), _pick_spec((3, 3 * SEG), l),
                _const_spec((1, SEG)), _const_spec((SEG, SEG))]
    args = [gate_par, conv_w, ng, bd]
    if has_cache:
        in_specs.append(_state_in_spec(l, st))
        args.append(state)
    scratch = ([pltpu.VMEM((T, SEG), F32)] * 2 + [pltpu.VMEM((2, SEG, SEG), F32)]
               + [pltpu.VMEM((T, SEG), F32)] * 3
               + [pltpu.VMEM((2, T, SEG), F32)] + [pltpu.VMEM((2, T, SEG), BF16)] * 4
               + [pltpu.VMEM((2, T // CHUNK * 8, SEG), F32)])
    return _mixer_call(functools.partial(_delta_kernel, T=T, has_cache=has_cache), "deltanet", T, n_seq,
                       P, (S_AQ, S_AK, S_AV, S_AZ), G,
                       in_specs, args, scratch, () if has_cache else (st,), l, prev)


def _diff_kernel(*refs, T, has_cache, lam_init):
    if has_cache:
        (q_ref, k_ref, v_ref, qg_ref, kg_ref, lam_ref, ng_ref, bd32_ref, bd64_ref,
         cos_ref, sin_ref, ck_ref, cv_ref, o_ref, qs, kh, vt) = refs
    else:
        (q_ref, k_ref, v_ref, qg_ref, kg_ref, lam_ref, ng_ref, bd32_ref, bd64_ref,
         o_ref, ko_ref, vo_ref, qs, kh, vt) = refs
    n_blk = T // ROW_BLOCK
    s0 = ck_ref.shape[1] if has_cache else 0
    bd32 = bd32_ref[...]
    lane = lax.broadcasted_iota(jnp.int32, (1, SEG), 1)
    first_half = (lane % 16) < 8

    if has_cache:
        for h in range(N_HEADS):
            kh[h, 0:s0, :] = ck_ref[h]
            vt[h, :, 0:s0] = cv_ref[h].T

    for i in range(n_blk):
        r0 = i * ROW_BLOCK
        for src, g_ref in ((q_ref, qg_ref), (k_ref, kg_ref)):
            x = src[pl.ds(r0, ROW_BLOCK), :]
            y = x * lax.rsqrt(_gsum(x * x, bd32) * (1.0 / DQK) + EPS) * g_ref[...]
            if has_cache:
                partner = jnp.where(first_half, pltpu.roll(y, SEG - 8, axis=1), pltpu.roll(y, 8, axis=1))
                y = y * cos_ref[pl.ds(r0, ROW_BLOCK), :] + partner * sin_ref[pl.ds(r0, ROW_BLOCK), :]
            if src is q_ref:
                qs[pl.ds(r0, ROW_BLOCK), :] = y
            else:
                for h in range(N_HEADS):
                    kh[h, pl.ds(s0 + r0, ROW_BLOCK), :] = y[:, _hs(h)]
                    if not has_cache:
                        ko_ref[h, pl.ds(r0, ROW_BLOCK), :] = y[:, _hs(h)]
        xv = v_ref[pl.ds(r0, ROW_BLOCK), :]
        xvt = xv.T
        for h in range(N_HEADS):
            vt[h, :, pl.ds(s0 + r0, ROW_BLOCK)] = xvt[_hs(h), :]
            if not has_cache:
                vo_ref[h, pl.ds(r0, ROW_BLOCK), :] = xv[:, _hs(h)]

    lp = lam_ref[...]
    lam = (jnp.exp(jnp.sum(lp[0:1, :] * lp[1:2, :], axis=1, keepdims=True))
           - jnp.exp(jnp.sum(lp[2:3, :] * lp[3:4, :], axis=1, keepdims=True)) + lam_init)
    scale = DQK ** -0.5 * math.log2(math.e)
    comp_rows = lax.broadcasted_iota(jnp.int32, (HEAD_DIM, 1), 0) < DQK
    bd64 = bd64_ref[...]
    ng = ng_ref[...]

    def qblock(i, carry):
        r0 = pl.multiple_of(i * Q_BLOCK, Q_BLOCK)
        qt = (qs[pl.ds(r0, Q_BLOCK), :] * scale).T
        qct = []
        for h in range(N_HEADS):
            qh = qt[_hs(h), :]
            q1 = jnp.where(comp_rows, qh, 0.0)
            qct += [q1, qh - q1]
        n_units = len(qct)
        st = _mm1(kh[0], qct[0])
        res = []
        for u in range(n_units):
            nxt = _mm1(kh[(u + 1) // 2], qct[u + 1]) if u + 1 < n_units else None
            e = jnp.exp2(st - st.max(axis=0, keepdims=True))
            res.append(_mm1(vt[u // 2], e) * (1.0 / e.sum(axis=0, keepdims=True)))
            st = nxt
        outs = [res[2 * h] - lam * res[2 * h + 1] for h in range(N_HEADS)]
        o = jnp.concatenate(outs, axis=0).T
        o_ref[pl.ds(r0, Q_BLOCK), :] = _head_norm_gate(o, bd64, ng, 1.0 - lam_init)
        return carry

    lax.fori_loop(0, T // Q_BLOCK, qblock, 0)


def _diffattn(P, T, n_seq, qg, kg, lam_par, ng, bd32, bd64, l, rope, cache, prev):
    has_cache = cache is not None
    lam_init = 0.8 - 0.6 * math.exp(-0.3 * l)
    kv = (N_HEADS, T, HEAD_DIM)
    in_specs = [_const_spec((1, SEG)), _const_spec((1, SEG)), _pick_spec((4, DQK), l), _const_spec((1, SEG)),
                _const_spec((SEG, SEG)), _const_spec((SEG, SEG))]
    args = [qg, kg, lam_par, ng, bd32, bd64]
    if has_cache:
        ckv = cache[0].shape[2:]
        in_specs += [_const_spec((T, SEG)), _const_spec((T, SEG)), _state_in_spec(l, ckv), _state_in_spec(l, ckv)]
        args += [rope[0], rope[1], cache[0], cache[1]]
    n_keys = T + (cache[0].shape[3] if has_cache else 0)
    scratch = [pltpu.VMEM((T, SEG), F32), pltpu.VMEM((N_HEADS, n_keys, HEAD_DIM), F32),
               pltpu.VMEM((N_HEADS, HEAD_DIM, n_keys), F32)]
    return _mixer_call(functools.partial(_diff_kernel, T=T, has_cache=has_cache, lam_init=lam_init),
                       "diff_attn", T, n_seq, P, (S_BQ, S_BK, S_BV), None,
                       in_specs, args, scratch, () if has_cache else (kv, kv), l, prev)


def _rope_tables(T):
    n_freq = DQK // 4
    t = jnp.arange(T)
    rows = (t // GRID_W).astype(F32)
    cols = (t % GRID_W).astype(F32)
    freqs = ROPE_BASE ** (-jnp.arange(n_freq, dtype=F32) / n_freq)
    ang_r, ang_c = rows[:, None] * freqs, cols[:, None] * freqs

    def comp(fn, sign):
        return jnp.concatenate([fn(ang_r), sign * fn(ang_r), fn(ang_c), sign * fn(ang_c)], axis=1)

    reps = SEG // DQK
    cos = jnp.tile(comp(jnp.cos, 1.0), (1, reps))
    sin = jnp.tile(jnp.concatenate([-jnp.sin(ang_r), jnp.sin(ang_r), -jnp.sin(ang_c), jnp.sin(ang_c)], axis=1),
                   (1, reps))
    return cos, sin


def _block_ones(group):
    i = np.arange(SEG)
    return jnp.asarray(i[:, None] // group == i[None, :] // group, BF16)


def kernel(x_prompt, x_sample, cache_diff_k, cache_diff_v, state_delta, state_mlstm_C, state_mlstm_n, state_mlstm_m, state_ret, c, c_ctx, w_ada, b_ada, norm_g, ffn_w_gate, ffn_w_up, ffn_w_down, w_in, dn_conv_w, dn_a_log, dn_dt_bias, dn_norm_g, da_qn_g, da_kn_g, da_lambda, da_norm_g, ml_i_bias, ml_f_bias, ml_norm_g, ret_decay_logit, ret_norm_g, w_branch, w_out):
    B, T, _ = x_prompt.shape
    Bs, Ts, _ = x_sample.shape
    cond8 = jnp.concatenate([c_ctx[None], c, jnp.zeros((8 - 1 - Bs, D_MODEL), F32)], axis=0)
    mod = _ada(cond8, w_ada, b_ada).reshape(DEPTH, 8, N_MOD, D_MODEL)
    bd32, bd64 = _block_ones(DQK), _block_ones(HEAD_DIM)
    rope = _rope_tables(Ts)
    tile_heads = lambda g: jnp.tile(g, SEG // g.shape[0])[None]

    wg, wu, wd = ffn_w_gate.astype(BF16), ffn_w_up.astype(BF16), ffn_w_down.astype(BF16)
    cols = lambda f: w_in[:, :, _IN_OFFS[f]:_IN_OFFS[f + 1]]
    w_mix = jnp.concatenate([cols(f) for f in _SEG_FIELDS], axis=2).astype(BF16)
    w_gate = jnp.concatenate([cols(f) for f in _GATE_FIELDS]
                             + [jnp.zeros((DEPTH, D_MODEL, GATE_LANES - 32), F32)], axis=2).astype(BF16)
    w_merge = cols(19).astype(BF16)
    wb, wo = w_branch.astype(BF16), w_out.astype(BF16)
    norm_g4 = norm_g.reshape(DEPTH, 3, 1, D_MODEL)
    lanes = lambda *parts: jnp.concatenate([p.reshape(DEPTH, 1, -1) for p in parts], axis=2)
    z8 = jnp.zeros((DEPTH, 8), F32)
    gate_par = jnp.concatenate([
        lanes(z8, dn_dt_bias, ml_i_bias, ml_f_bias, jnp.zeros((DEPTH, GATE_LANES - 32), F32)),
        lanes(z8, dn_a_log, jnp.zeros((DEPTH, GATE_LANES - 16), F32)),
        jnp.zeros((DEPTH, 6, GATE_LANES), F32)], axis=1)
    ml_state = (state_mlstm_C, state_mlstm_n.reshape(Bs, DEPTH, 2, 1, SEG),
                jnp.repeat(state_mlstm_m, HEAD_DIM, axis=-1).reshape(Bs, DEPTH, 2, 1, SEG))

    xs = {"ctx": x_prompt.reshape(B * T, D_MODEL), "smp": x_sample.reshape(Bs * Ts, D_MODEL)}
    geo = {"ctx": (T, B, B * T, 0), "smp": (Ts, Bs, Ts, 1)}
    states = {}
    for l in range(DEPTH):
        for path in ("ctx", "smp"):
            Tp, n_seq, rows_per_cond, first_cond = geo[path]
            mspec = _mod_spec(l, rows_per_cond, first_cond)
            smp = path == "smp"
            prev = states.get if not smp else (lambda name: None)
            x = _ffn(xs[path], mod, mspec, norm_g4, wg, wu, wd, l, 0)
            P, G = _inproj(x, mod, mspec, norm_g4, w_mix, w_gate, l)
            ra = _deltanet(P, G, Tp, n_seq, gate_par, dn_conv_w, tile_heads(dn_norm_g[l]), bd64, l,
                           state_delta if smp else None, prev("dn"))
            rb = _diffattn(P, Tp, n_seq, tile_heads(da_qn_g[l]), tile_heads(da_kn_g[l]), da_lambda,
                           tile_heads(da_norm_g[l]), bd32, bd64, l, rope if smp else None,
                           (cache_diff_k, cache_diff_v) if smp else None, prev("kv"))
            rc = _mlstm(P, G, Tp, n_seq, gate_par, tile_heads(ml_norm_g[l]), bd64, l,
                        ml_state if smp else None, prev("ml"))
            rd = _retention(P, Tp, n_seq, ret_decay_logit[l], tile_heads(ret_norm_g[l]), bd64, l,
                            state_ret if smp else None, prev("r"))
            if not smp:
                states = {"dn": ra[1:], "kv": rb[1:], "ml": rc[1:], "r": rd[1:]}
            x = _merge(x, mod, mspec, norm_g4, (ra[0], rb[0], rc[0], rd[0]), w_merge, wb, wo, l)
            xs[path] = _ffn(x, mod, mspec, norm_g4, wg, wu, wd, l, 1)
    (new_dn,), (new_k, new_v), (new_c, new_n, new_m), (new_r,) = (states[k] for k in ("dn", "kv", "ml", "r"))
    per_head = lambda rows: rows.reshape(B, DEPTH, 2, N_HEADS, HEAD_DIM)
    return (xs["ctx"].reshape(B, T, D_MODEL), xs["smp"].reshape(Bs, Ts, D_MODEL),
            new_k, new_v, new_dn, new_c, per_head(new_n), per_head(new_m)[..., 0], new_r)
```

```python
import functools
import math

import numpy as np
import jax
import jax.numpy as jnp
from jax import lax
from jax.experimental import pallas as pl
from jax.experimental.pallas import tpu as pltpu

F32 = jnp.float32
BF16 = jnp.bfloat16

D_MODEL = 1024
FFN_DIM = 2816
N_MOD = 9
DEPTH = 2
N_HEADS = 4
HEAD_DIM = 64
SEG = N_HEADS * HEAD_DIM
N_SEG = 15
CHUNK = 64
DQK = 32
GRID_W = 64
ROPE_BASE = 10000.0
EPS = 1e-6
N_BRANCH = 4
GATE_LANES = 128
NEG = -1e30

TM = 512
Q_BLOCK = 256
ROW_BLOCK = 128
SEQS_PER_STEP = 4
VMEM_LIMIT = 56 * 1024 * 1024

NN = (((1,), (0,)), ((), ()))
NT = (((1,), (1,)), ((), ()))
TN = (((0,), (0,)), ((), ()))

_IN_SIZES = (256, 256, 256, 256, 8, 8, 256, 256, 256, 256, 256, 256, 256, 8, 8, 256, 256, 256, 256, 4096)
_IN_OFFS = np.concatenate([[0], np.cumsum(_IN_SIZES)]).tolist()
_SEG_FIELDS = (0, 1, 2, 3, 6, 7, 8, 9, 10, 11, 12, 15, 16, 17, 18)
_GATE_FIELDS = (4, 5, 13, 14)
(S_AQ, S_AK, S_AV, S_AZ, S_BQ, S_BK, S_BV, S_CQ, S_CK, S_CV, S_CO, S_DQ, S_DK, S_DV, S_DG) = range(N_SEG)
L_BETA, L_ALPHA, L_CI, L_CF = 0, 8, 16, 24


def _dg(a, b, dims):
    return lax.dot_general(a, b, dims, preferred_element_type=F32)


def _split2(x):
    hi = x.astype(BF16)
    lo = (x - hi.astype(F32)).astype(BF16)
    return hi, lo


def _mm1(a, b, dims=NN):
    return _dg(a.astype(BF16), b.astype(BF16), dims)


def _mm3(a, b, dims=NN):
    ah, al = _split2(a)
    bh, bl = _split2(b)
    return _dg(ah, bh, dims) + (_dg(ah, bl, dims) + _dg(al, bh, dims))


def _mm2r(a, b, dims=NN):
    ah = a.astype(BF16)
    bh, bl = _split2(b)
    return _dg(ah, bh, dims) + _dg(ah, bl, dims)


def _sel_mm(sel, x, dims=NN):
    h0 = x.astype(BF16)
    r1 = x - h0.astype(F32)
    h1 = r1.astype(BF16)
    h2 = (r1 - h1.astype(F32)).astype(BF16)
    return _dg(sel, h0, dims) + (_dg(sel, h1, dims) + _dg(sel, h2, dims))


def _gsum(x, bd):
    hi, lo = _split2(x)
    return _dg(hi, bd, NN) + _dg(lo, bd, NN)


def _sigmoid(x):
    return 1.0 / (1.0 + jnp.exp(-x))


def _silu(x):
    return x * _sigmoid(x)


def _softplus(x):
    return jnp.maximum(x, 0.0) + jnp.log1p(jnp.exp(-jnp.abs(x)))


def _log_sigmoid(x):
    return -_softplus(-x)


def _norm_mod(x, g, shift, scale):
    ms = jnp.mean(x * x, axis=-1, keepdims=True)
    return (x * lax.rsqrt(ms + EPS) * g) * (1.0 + scale) + shift


def _tri_masks():
    ii = lax.broadcasted_iota(jnp.int32, (CHUNK, CHUNK), 0)
    jj = lax.broadcasted_iota(jnp.int32, (CHUNK, CHUNK), 1)
    return ((jj <= ii, jj < ii), (jj >= ii, jj > ii))


def _cparams(n_grid=1):
    return pltpu.CompilerParams(dimension_semantics=("arbitrary",) * n_grid,
                                vmem_limit_bytes=VMEM_LIMIT)


def _const_spec(shape):
    nd = len(shape)
    return pl.BlockSpec(shape, lambda *_: (0,) * nd)


def _ada_kernel(s_ref, w_ref, b_ref, o_ref):
    s = s_ref[...]
    o_ref[...] = _mm3(_silu(s), w_ref[...]) + b_ref[...]


def _ada(cond8, w_ada, b_ada):
    tn = 1536
    n_t = (N_MOD * D_MODEL) // tn
    return pl.pallas_call(
        _ada_kernel,
        grid=(DEPTH, n_t),
        in_specs=[pl.BlockSpec((8, D_MODEL), lambda l, j: (0, 0)),
                  pl.BlockSpec((None, D_MODEL, tn), lambda l, j: (l, 0, j)),
                  pl.BlockSpec((None, 1, tn), lambda l, j: (l, 0, j))],
        out_specs=pl.BlockSpec((None, 8, tn), lambda l, j: (l, 0, j)),
        out_shape=jax.ShapeDtypeStruct((DEPTH, 8, N_MOD * D_MODEL), F32),
        compiler_params=_cparams(2),
        name="ada_mod",
    )(cond8, w_ada, b_ada.reshape(DEPTH, 1, N_MOD * D_MODEL))


def _ffn_kernel(x_ref, mod_ref, g_ref, wg_ref, wu_ref, wd_ref, o_ref, *, mi):
    x = x_ref[...]
    h = _norm_mod(x, g_ref[...], mod_ref[mi:mi + 1, :], mod_ref[mi + 1:mi + 2, :]).astype(BF16)
    gate = jnp.dot(h, wg_ref[...], preferred_element_type=F32)
    up = jnp.dot(h, wu_ref[...], preferred_element_type=F32)
    act = (_silu(gate) * up).astype(BF16)
    y = jnp.dot(act, wd_ref[...], preferred_element_type=F32)
    o_ref[...] = x + (0.5 * mod_ref[mi + 2:mi + 3, :]) * y


def _pick_spec(tail, *lead):
    nd = len(tail)
    return pl.BlockSpec((None,) * len(lead) + tuple(tail), lambda *_: tuple(lead) + (0,) * nd,
                        pipeline_mode=pl.Buffered(1))


def _mod_spec(l, rows_per_cond, first_cond):
    per = rows_per_cond // TM
    return pl.BlockSpec((None, None, N_MOD, D_MODEL), lambda i: (l, first_cond + i // per, 0, 0))


def _ffn(x, mod, mod_spec, norm_g, wg, wu, wd, l, j):
    n = x.shape[0]
    return pl.pallas_call(
        functools.partial(_ffn_kernel, mi=6 * j),
        grid=(n // TM,),
        in_specs=[pl.BlockSpec((TM, D_MODEL), lambda i: (i, 0)),
                  mod_spec,
                  _pick_spec((1, D_MODEL), l, 2 * j),
                  _pick_spec((D_MODEL, FFN_DIM), l, j),
                  _pick_spec((D_MODEL, FFN_DIM), l, j),
                  _pick_spec((FFN_DIM, D_MODEL), l, j)],
        out_specs=pl.BlockSpec((TM, D_MODEL), lambda i: (i, 0)),
        out_shape=jax.ShapeDtypeStruct((n, D_MODEL), F32),
        compiler_params=_cparams(1),
        name="ffn",
    )(x, mod, norm_g, wg, wu, wd)


def _inproj_kernel(x_ref, mod_ref, g_ref, w_ref, wgt_ref, p_ref, gt_ref):
    h = _norm_mod(x_ref[...], g_ref[...], mod_ref[3:4, :], mod_ref[4:5, :]).astype(BF16)
    step = 3 * SEG
    for j in range(0, N_SEG * SEG, step):
        p_ref[:, j:j + step] = jnp.dot(h, w_ref[:, j:j + step], preferred_element_type=F32)
    gt_ref[...] = jnp.dot(h, wgt_ref[...], preferred_element_type=F32)


def _inproj(x, mod, mod_spec, norm_g, w_mix, w_gate, l):
    n = x.shape[0]
    return pl.pallas_call(
        _inproj_kernel,
        grid=(n // TM,),
        in_specs=[pl.BlockSpec((TM, D_MODEL), lambda i: (i, 0)),
                  mod_spec,
                  _pick_spec((1, D_MODEL), l, 1),
                  _pick_spec((D_MODEL, N_SEG * SEG), l),
                  _pick_spec((D_MODEL, GATE_LANES), l)],
        out_specs=[pl.BlockSpec((TM, N_SEG * SEG), lambda i: (i, 0)),
                   pl.BlockSpec((TM, GATE_LANES), lambda i: (i, 0))],
        out_shape=[jax.ShapeDtypeStruct((n, N_SEG * SEG), F32),
                   jax.ShapeDtypeStruct((n, GATE_LANES), F32)],
        compiler_params=_cparams(1),
        name="in_proj",
    )(x, mod, norm_g, w_mix, w_gate)


def _merge_kernel(x_ref, mod_ref, g_ref, ba_ref, bb_ref, bc_ref, bd_ref, wm_ref, wb_ref, wo_ref, o_ref):
    x = x_ref[...]
    h = _norm_mod(x, g_ref[...], mod_ref[3:4, :], mod_ref[4:5, :]).astype(BF16)
    mixed = None
    for m, b_ref in enumerate((ba_ref, bb_ref, bc_ref, bd_ref)):
        logits = jnp.dot(h, wm_ref[:, m * D_MODEL:(m + 1) * D_MODEL], preferred_element_type=F32)
        pb = jnp.dot(b_ref[...].astype(BF16), wb_ref[m], preferred_element_type=F32)
        term = _sigmoid(logits) * pb
        mixed = term if mixed is None else mixed + term
    y = jnp.dot(mixed.astype(BF16), wo_ref[...], preferred_element_type=F32)
    o_ref[...] = x + mod_ref[5:6, :] * y


def _merge(x, mod, mod_spec, norm_g, branches, w_merge, w_branch, w_out, l):
    n = x.shape[0]
    row = lambda w: pl.BlockSpec((TM, w), lambda i: (i, 0))
    return pl.pallas_call(
        _merge_kernel,
        grid=(n // TM,),
        in_specs=[row(D_MODEL), mod_spec, _pick_spec((1, D_MODEL), l, 1),
                  row(SEG), row(SEG), row(SEG), row(SEG),
                  _pick_spec((D_MODEL, N_BRANCH * D_MODEL), l),
                  _pick_spec((N_BRANCH, SEG, D_MODEL), l),
                  _pick_spec((D_MODEL, D_MODEL), l)],
        out_specs=row(D_MODEL),
        out_shape=jax.ShapeDtypeStruct((n, D_MODEL), F32),
        compiler_params=_cparams(1),
        name="merge",
    )(x, mod, norm_g, *branches, w_merge, w_branch, w_out)


def _state_in_spec(l, tail):
    nd = len(tail)
    return pl.BlockSpec((None, None) + tail, lambda s: (s, l) + (0,) * nd)


def _per_sequence(kernel, T, n_sub, n_rows_in, n_in, n_alias, n_state, l, creates, *refs):
    ins, rest = refs[:n_in], refs[n_in + n_alias:]
    main, states, scratch = rest[0], rest[1:1 + n_state], rest[1 + n_state:]
    if creates:
        for r in states:
            for other in range(DEPTH):
                if other != l:
                    r[:, other] = jnp.zeros((n_sub,) + r.shape[2:], F32)

    def one(sub, carry):
        rows = pl.ds(pl.multiple_of(sub * T, T), T)
        own = [r.at[sub, l] if creates else r.at[sub] for r in states]
        kernel(*[r.at[rows] for r in ins[:n_rows_in]], *ins[n_rows_in:], main.at[rows], *own, *scratch)
        return carry

    lax.fori_loop(0, n_sub, one, 0)


def _all_sequences(kernel, n_sub, n_in, n_alias, n_state, l, creates, *refs):
    ins, rest = refs[:n_in], refs[n_in + n_alias:]
    main, states, scratch = rest[0], rest[1:1 + n_state], rest[1 + n_state:]
    if creates:
        for r in states:
            for other in range(DEPTH):
                if other != l:
                    r[:, other] = jnp.zeros((n_sub,) + r.shape[2:], F32)
    own = [r.at[sub, l] if creates else r.at[sub] for r in states for sub in range(n_sub)]
    kernel(*ins, main, *own, *scratch)


def _mixer_call(kernel, name, T, n_seq, P, segs, G, in_specs, args, scratch, state_tails=(), l=0, prev=None,
                interleaved=False):
    creates = prev is None
    n_sub = SEQS_PER_STEP if state_tails else 1
    rows = T * n_sub
    row_specs = [pl.BlockSpec((rows, SEG), lambda s, j=j: (s, j)) for j in segs]
    row_args = [P] * len(segs)
    if G is not None:
        row_specs.append(pl.BlockSpec((rows, GATE_LANES), lambda s: (s, 0)))
        row_args.append(G)
    in_specs, args = row_specs + in_specs, row_args + args
    n_in, n_alias = len(args), 0 if creates else len(prev)

    def state_spec(t):
        if creates:
            return pl.BlockSpec((n_sub, DEPTH) + t, lambda s: (s,) + (0,) * (1 + len(t)))
        return pl.BlockSpec((n_sub, None) + t, lambda s: (s, l) + (0,) * len(t))

    out_specs = [pl.BlockSpec((rows, SEG), lambda s: (s, 0))] + [state_spec(t) for t in state_tails]
    out_shape = ([jax.ShapeDtypeStruct((n_seq * T, SEG), F32)]
                 + [jax.ShapeDtypeStruct((n_seq, DEPTH) + t, F32) for t in state_tails])
    aliases = {}
    if not creates:
        in_specs = in_specs + [pl.BlockSpec(memory_space=pl.ANY)] * n_alias
        args = args + list(prev)
        aliases = {n_in + i: 1 + i for i in range(n_alias)}
    if interleaved:
        body = functools.partial(_all_sequences, kernel, n_sub, n_in, n_alias, len(state_tails), l, creates)
    else:
        body = functools.partial(_per_sequence, kernel, T, n_sub, len(row_args), n_in, n_alias,
                                 len(state_tails), l, creates)
    return pl.pallas_call(
        body, grid=(n_seq // n_sub,), in_specs=in_specs, out_specs=out_specs, out_shape=out_shape,
        scratch_shapes=scratch, input_output_aliases=aliases, compiler_params=_cparams(1), name=name)(*args)


def _head_norm_gate(o, bd, ng, gate):
    ss = _gsum(o * o, bd)
    return o * lax.rsqrt(ss * (1.0 / HEAD_DIM) + EPS) * ng * gate


def _epilogue(T, of_s, ob_s, gate_ref, gate_fn, bd_ref, ng_ref, o_ref):
    rb = 256
    bd = bd_ref[...]
    ng = ng_ref[...]

    def blk(i, carry):
        r0 = pl.multiple_of(i * rb, rb)
        o = of_s[pl.ds(r0, rb), :] + ob_s[pl.ds(r0, rb), :]
        o_ref[pl.ds(r0, rb), :] = _head_norm_gate(o, bd, ng, gate_fn(gate_ref[pl.ds(r0, rb), :]))
        return carry

    lax.fori_loop(0, T // rb, blk, 0)


def _chunk_rows(d, n, n_chunks, base=0):
    c = n if d == 0 else n_chunks - 1 - n
    return pl.multiple_of(base + c * CHUNK, CHUNK)


def _hs(h):
    return slice(h * HEAD_DIM, (h + 1) * HEAD_DIM)


def _head_mask():
    ii = lax.broadcasted_iota(jnp.int32, (SEG, SEG), 0) // HEAD_DIM
    jj = lax.broadcasted_iota(jnp.int32, (SEG, SEG), 1) // HEAD_DIM
    return ii == jj


def _blocksum(x):
    c = CHUNK
    return (x[0:c] + x[c:2 * c]) + (x[2 * c:3 * c] + x[3 * c:4 * c])


def _lane_head():
    return lax.broadcasted_iota(jnp.int32, (1, SEG), 1) // HEAD_DIM


def _tiled_masks():
    ii = lax.broadcasted_iota(jnp.int32, (CHUNK, SEG), 0)
    jj = lax.broadcasted_iota(jnp.int32, (CHUNK, SEG), 1) % HEAD_DIM
    return ((jj <= ii, jj < ii), (jj >= ii, jj > ii)), jj == ii


def _block_diag(x, hm):
    return jnp.where(hm, jnp.concatenate([x] * N_HEADS, axis=0), 0.0)


def _col_dense(g, lane0, lane_head):
    out = jnp.broadcast_to(g[:, lane0:lane0 + 1], (CHUNK, SEG))
    for h in range(1, N_HEADS):
        out = jnp.where(lane_head == h, g[:, lane0 + h:lane0 + h + 1], out)
    return out


def _diag_row(col_dense, eye_t):
    return jnp.sum(jnp.where(eye_t, col_dense, 0.0), axis=0, keepdims=True)


def _seg_max(x, lane_head):
    out = None
    for h in range(N_HEADS):
        m = jnp.max(jnp.where(lane_head == h, x, NEG), axis=1, keepdims=True)
        out = m if out is None else jnp.where(lane_head == h, m, out)
    return jnp.broadcast_to(out, x.shape)


UNITS = tuple((d, h) for d in range(2) for h in range(N_HEADS))


def _ret_kernel(*refs, T, has_cache, n_sub):
    if has_cache:
        (q_ref, k_ref, v_ref, gt_ref, dl_ref, ng_ref, bd_ref, r0_ref, o_ref,
         of_s, ob_s, st_s, dec_s, qdec_s, kdec_s, cdec_s) = refs
    else:
        q_ref, k_ref, v_ref, gt_ref, dl_ref, ng_ref, bd_ref, o_ref = refs[:8]
        rf_refs = refs[8:8 + n_sub]
        of_s, ob_s, st_s, dec_s, qdec_s, kdec_s, cdec_s = refs[8 + n_sub:]
    n_chunks = T // CHUNK
    units = [(s, d) for s in range(n_sub) for d in range(2)]
    masks = _tri_masks()
    hm = _head_mask()

    @pl.when(pl.program_id(0) == 0)
    def _():
        ii = lax.broadcasted_iota(jnp.int32, (CHUNK, CHUNK), 0)
        jj = lax.broadcasted_iota(jnp.int32, (CHUNK, CHUNK), 1)
        rel = (ii - jj).astype(F32)
        pos = lax.broadcasted_iota(jnp.int32, (CHUNK, 1), 0).astype(F32)
        lg_all = _log_sigmoid(dl_ref[...])
        for d in range(2):
            dec, qdec, kdec, cdec = [], [], [], []
            for h in range(N_HEADS):
                lg = lg_all[d:d + 1, h:h + 1]
                if d == 0:
                    e, qd, kd = rel * lg, (pos + 1.0) * lg, (CHUNK - 1.0 - pos) * lg
                else:
                    e, qd, kd = -rel * lg, (CHUNK - pos) * lg, pos * lg
                dec.append(jnp.exp(jnp.where(masks[d][0], e, NEG)))
                qdec.append(jnp.broadcast_to(jnp.exp(qd), (CHUNK, HEAD_DIM)))
                kdec.append(jnp.broadcast_to(jnp.exp(kd), (CHUNK, HEAD_DIM)))
                cdec.append(jnp.broadcast_to(jnp.exp(CHUNK * lg), (HEAD_DIM, SEG)))
            dec_s[d] = jnp.concatenate(dec, axis=0)
            qdec_s[d] = jnp.concatenate(qdec, axis=1)
            kdec_s[d] = jnp.concatenate(kdec, axis=1)
            cdec_s[d] = jnp.concatenate(cdec, axis=0)

    for s, d in units:
        st_s[s, d] = jnp.zeros((SEG, SEG), F32)
        if has_cache:
            for h in range(N_HEADS):
                st_s[s, d, _hs(h), _hs(h)] = r0_ref[d, h]

    def body(n, carry):
        rows = [_chunk_rows(d, n, n_chunks, s * T) for s, d in units]
        q = [q_ref[pl.ds(r0, CHUNK), :] for r0 in rows]
        k = [k_ref[pl.ds(r0, CHUNK), :] * (HEAD_DIM ** -0.5) for r0 in rows]
        v = [v_ref[pl.ds(r0, CHUNK), :] for r0 in rows]
        st = [st_s[u] for u in units]
        q4 = [jnp.where(hm, jnp.concatenate([x] * N_HEADS, axis=0), 0.0) for x in q]
        qk = [_mm1(a, b, NT) for a, b in zip(q4, k)]
        qr = [_mm1(q[i] * qdec_s[d], st[i]) for i, (s, d) in enumerate(units)]
        kv = [_mm1(k[i] * kdec_s[d], v[i], TN) for i, (s, d) in enumerate(units)]
        av = [_mm1(qk[i] * dec_s[d], v[i]) for i, (s, d) in enumerate(units)]
        for i, (s, d) in enumerate(units):
            st_s[s, d] = cdec_s[d] * st[i] + jnp.where(hm, kv[i], 0.0)
            (of_s, ob_s)[d][pl.ds(rows[i], CHUNK), :] = qr[i] + _blocksum(jnp.where(hm, av[i], 0.0))
        return carry

    lax.fori_loop(0, n_chunks, body, 0)
    if not has_cache:
        for s, d in units:
            for h in range(N_HEADS):
                rf_refs[s][d, h] = st_s[s, d, _hs(h), _hs(h)]
    _epilogue(T * n_sub, of_s, ob_s, gt_ref, _silu, bd_ref, ng_ref, o_ref)


def _retention(P, T, n_seq, decay_logit, ng, bd, l, state, prev):
    has_cache = state is not None
    st = (2, N_HEADS, HEAD_DIM, HEAD_DIM)
    in_specs = [_const_spec((2, N_HEADS)), _const_spec((1, SEG)), _const_spec((SEG, SEG))]
    args = [decay_logit, ng, bd]
    if has_cache:
        in_specs.append(_state_in_spec(l, st))
        args.append(state)
    n_sub = 1 if has_cache else SEQS_PER_STEP
    scratch = [pltpu.VMEM((n_sub * T, SEG), F32), pltpu.VMEM((n_sub * T, SEG), F32),
               pltpu.VMEM((n_sub, 2, SEG, SEG), F32),
               pltpu.VMEM((2, SEG, HEAD_DIM), F32), pltpu.VMEM((2, CHUNK, SEG), F32),
               pltpu.VMEM((2, CHUNK, SEG), F32), pltpu.VMEM((2, SEG, SEG), F32)]
    return _mixer_call(functools.partial(_ret_kernel, T=T, has_cache=has_cache, n_sub=n_sub), "retention",
                       T, n_seq, P, (S_DQ, S_DK, S_DV, S_DG), None,
                       in_specs, args, scratch, () if has_cache else (st,), l, prev, interleaved=True)


def _mlstm_kernel(*refs, T, has_cache, n_sub):
    if has_cache:
        (q_ref, k_ref, v_ref, og_ref, gt_ref, gp_ref, ng_ref, bd_ref, c0_ref, n0_ref, m0_ref,
         o_ref, of_s, ob_s, c_s, n_s, m_s) = refs
    else:
        q_ref, k_ref, v_ref, og_ref, gt_ref, gp_ref, ng_ref, bd_ref, o_ref = refs[:9]
        cf_refs, nf_refs, mf_refs = (refs[9 + i * n_sub:9 + (i + 1) * n_sub] for i in range(3))
        of_s, ob_s, c_s, n_s, m_s = refs[9 + 3 * n_sub:]
    n_chunks = T // CHUNK
    units = [(s, d) for s in range(n_sub) for d in range(2)]
    masks = _tri_masks()
    tri = (masks[0][0].astype(BF16), masks[1][0].astype(BF16))
    masks_t, eye_t = _tiled_masks()
    tri_t = [masks_t[d][0] for d in range(2)]
    hm = _head_mask()
    lane_head = _lane_head()
    bd = bd_ref[...]
    bias = gp_ref[0:1, :]

    for s, d in units:
        c_s[s, d] = jnp.zeros((SEG, SEG), F32)
        if has_cache:
            for h in range(N_HEADS):
                c_s[s, d, _hs(h), _hs(h)] = c0_ref[d, h]
            n_s[s, d] = n0_ref[d]
            m_s[s, d] = m0_ref[d]
        else:
            n_s[s, d] = jnp.zeros((1, SEG), F32)
            m_s[s, d] = jnp.zeros((1, SEG), F32)

    def body(n, carry):
        rows = [_chunk_rows(d, n, n_chunks, s * T) for s, d in units]
        q = [q_ref[pl.ds(r0, CHUNK), :] * (HEAD_DIM ** -0.5) for r0 in rows]
        k = [k_ref[pl.ds(r0, CHUNK), :] for r0 in rows]
        v = [v_ref[pl.ds(r0, CHUNK), :] for r0 in rows]
        pre = [gt_ref[pl.ds(r0, CHUNK), :] + bias for r0 in rows]
        b = [_sel_mm(tri[d], _log_sigmoid(pre[i])) for i, (s, d) in enumerate(units)]
        c_prev = [c_s[u] for u in units]
        n_prev = [n_s[u] for u in units]

        k4 = [_block_diag(x, hm) for x in k]
        v4 = [_block_diag(x, hm) for x in v]
        qk = [_mm1(a_, b_, NT) for a_, b_ in zip(q, k4)]
        qc = [_mm1(a_, b_) for a_, b_ in zip(q, c_prev)]
        qn = [_gsum(a_ * b_, bd) for a_, b_ in zip(q, n_prev)]

        sc, w_inter, m_i, wk, dec, m_new = [], [], [], [], [], []
        for i, (s, d) in enumerate(units):
            b_col = _col_dense(b[i], L_CF + d * N_HEADS, lane_head)
            ig_col = _col_dense(pre[i], L_CI + d * N_HEADS, lane_head)
            b_row, ig_row = _diag_row(b_col, eye_t), _diag_row(ig_col, eye_t)
            last = CHUNK - 1 if d == 0 else 0
            b_last = b_col[last:last + 1, :]
            m_prev = m_s[s, d]
            dm = jnp.where(tri_t[d], b_col - b_row + ig_row, NEG)
            inter = b_col + m_prev
            mi = jnp.maximum(inter, _seg_max(dm, lane_head))
            sc.append(qk[i] * jnp.exp(dm - mi))
            w_inter.append(jnp.exp(inter - mi))
            m_i.append(mi)
            mn = jnp.maximum(b_last + m_prev, _seg_max(b_last - b_row + ig_row, lane_head))
            wk.append(k[i] * jnp.exp(b_last - b_col + ig_col - mn))
            dec.append(jnp.exp(b_last + m_prev - mn))
            m_new.append(mn)

        sv = [_mm1(a_, b_) for a_, b_ in zip(sc, v4)]
        ssum = [_gsum(a_, bd) for a_ in sc]
        kv = [_mm1(a_, b_, TN) for a_, b_ in zip(wk, v)]
        for i, (s, d) in enumerate(units):
            num = w_inter[i] * qc[i] + sv[i]
            den = w_inter[i] * qn[i] + ssum[i]
            (of_s, ob_s)[d][pl.ds(rows[i], CHUNK), :] = num / jnp.maximum(jnp.abs(den), jnp.exp(-m_i[i]))
            c_s[s, d] = dec[i] * c_prev[i] + jnp.where(hm, kv[i], 0.0)
            n_s[s, d] = dec[i] * n_prev[i] + jnp.sum(wk[i], axis=0, keepdims=True)
            m_s[s, d] = m_new[i]
        return carry

    lax.fori_loop(0, n_chunks, body, 0)
    if not has_cache:
        for s, d in units:
            for h in range(N_HEADS):
                cf_refs[s][d, h] = c_s[s, d, _hs(h), _hs(h)]
            nf_refs[s][d] = n_s[s, d]
            mf_refs[s][d] = m_s[s, d]
    _epilogue(T * n_sub, of_s, ob_s, og_ref, _sigmoid, bd_ref, ng_ref, o_ref)


def _mlstm(P, G, T, n_seq, gate_par, ng, bd, l, state, prev):
    has_cache = state is not None
    st = (2, N_HEADS, HEAD_DIM, HEAD_DIM)
    rw = (2, 1, SEG)
    in_specs = [_pick_spec((8, GATE_LANES), l), _const_spec((1, SEG)), _const_spec((SEG, SEG))]
    args = [gate_par, ng, bd]
    if has_cache:
        in_specs += [_state_in_spec(l, st), _state_in_spec(l, rw), _state_in_spec(l, rw)]
        args += list(state)
    n_sub = 1 if has_cache else SEQS_PER_STEP
    scratch = [pltpu.VMEM((n_sub * T, SEG), F32), pltpu.VMEM((n_sub * T, SEG), F32),
               pltpu.VMEM((n_sub, 2, SEG, SEG), F32), pltpu.VMEM((n_sub,) + rw, F32), pltpu.VMEM((n_sub,) + rw, F32)]
    return _mixer_call(functools.partial(_mlstm_kernel, T=T, has_cache=has_cache, n_sub=n_sub), "mlstm",
                       T, n_seq, P, (S_CQ, S_CK, S_CV, S_CO), G,
                       in_specs, args, scratch, () if has_cache else (st, rw, rw), l, prev, interleaved=True)


INV_BASE = 8
SOLVE_GROUP = 4
_MM_INV = _mm1
_MM_APPLY = _mm2r


def _inverse_level_masks():
    ii = lax.broadcasted_iota(jnp.int32, (CHUNK, CHUNK), 0)
    jj = lax.broadcasted_iota(jnp.int32, (CHUNK, CHUNK), 1)
    out = []
    for lo, hi in ((jj, ii), (ii, jj)):
        lv = [(lo // INV_BASE == hi // INV_BASE) & (lo < hi)]
        size = 2 * INV_BASE
        while size <= CHUNK:
            lv.append((lo // size == hi // size) & (hi % size >= size // 2) & (lo % size < size // 2))
            size *= 2
        out.append(lv)
    return out


def _delta_kernel(*refs, T, has_cache, n_sub):
    if has_cache:
        (q_ref, k_ref, v_ref, z_ref, gt_ref, gp_ref, cw_ref, ng_ref, bd_ref, s0_ref,
         o_ref, of_s, ob_s, st_s, qs, ks, vs, u_s, w_s, att_s, qg_s, kd_s, gl_s) = refs
    else:
        q_ref, k_ref, v_ref, z_ref, gt_ref, gp_ref, cw_ref, ng_ref, bd_ref, o_ref = refs[:10]
        sf_refs = refs[10:10 + n_sub]
        of_s, ob_s, st_s, qs, ks, vs, u_s, w_s, att_s, qg_s, kd_s, gl_s = refs[10 + n_sub:]
    n_chunks = T // CHUNK
    group = min(SOLVE_GROUP, n_chunks * n_sub)
    n_blk = T // ROW_BLOCK
    seq_dirs = [(s, d) for s in range(n_sub) for d in range(2)]
    masks = _tri_masks()
    tri = (masks[0][0].astype(BF16), masks[1][0].astype(BF16))
    masks_t, eye_t = _tiled_masks()
    incl_t = [masks_t[d][0] for d in range(2)]
    strict_t = [masks_t[d][1] for d in range(2)]
    hm = _head_mask()
    lane_head = _lane_head()
    lvl_masks = _inverse_level_masks()
    eye64 = (masks[0][0] & masks[1][0]).astype(F32)
    bd = bd_ref[...]
    bias, a_log = gp_ref[0:1, :], gp_ref[1:2, :]
    row = lax.broadcasted_iota(jnp.int32, (ROW_BLOCK, 1), 0)

    def prologue(i, carry):
        r0 = pl.multiple_of(i * ROW_BLOCK, ROW_BLOCK)
        rp = pl.multiple_of(jnp.maximum(r0 - 8, 0), 8)
        rn = pl.multiple_of(jnp.minimum(r0 + ROW_BLOCK, n_sub * T - 8), 8)
        in_seq = lax.rem(i, n_blk)
        for j, (src, dst) in enumerate(((q_ref, qs), (k_ref, ks), (v_ref, vs))):
            cur = src[pl.ds(r0, ROW_BLOCK), :]
            before = jnp.where(in_seq > 0, src[pl.ds(rp, 8), :][7:8, :], 0.0)
            after = jnp.where(in_seq < n_blk - 1, src[pl.ds(rn, 8), :][0:1, :], 0.0)
            down = jnp.where(row == 0, before, pltpu.roll(cur, 1, axis=0))
            up = jnp.where(row == ROW_BLOCK - 1, after, pltpu.roll(cur, ROW_BLOCK - 1, axis=0))
            w = cw_ref[:, j * SEG:(j + 1) * SEG]
            y = _silu(w[0:1, :] * down + w[1:2, :] * cur + w[2:3, :] * up)
            if j < 2:
                y = y * lax.rsqrt(_gsum(y * y, bd) + EPS)
            if j == 0:
                y = y * (HEAD_DIM ** -0.5)
            dst[pl.ds(r0, ROW_BLOCK), :] = y
        return carry

    lax.fori_loop(0, n_sub * n_blk, prologue, 0)

    for s, d in seq_dirs:
        st_s[s, d] = jnp.zeros((SEG, SEG), F32)
        if has_cache:
            for h in range(N_HEADS):
                st_s[s, d, _hs(h), _hs(h)] = s0_ref[d, h]

    def solve_group(g, carry):
        items = []
        q, k, v, beta, cg_col, g_last, decay, kb, k4 = ([] for _ in range(9))
        for cc in range(group):
            c = g * group + cc
            r0 = pl.multiple_of(c * CHUNK, CHUNK)
            qc, kc, vc = qs[pl.ds(r0, CHUNK), :], ks[pl.ds(r0, CHUNK), :], vs[pl.ds(r0, CHUNK), :]
            pre = gt_ref[pl.ds(r0, CHUNK), :]
            g_all = -jnp.exp(a_log) * _softplus(pre + bias)
            beta_all = _sigmoid(pre)
            kc4 = _block_diag(kc, hm)
            for d in range(2):
                items.append((d, r0, pl.multiple_of(c * 8, 8)))
                cg = _sel_mm(tri[d], g_all)
                col = _col_dense(cg, L_ALPHA + d * N_HEADS, lane_head)
                row_ = _diag_row(col, eye_t)
                bt = _col_dense(beta_all, L_BETA + d * N_HEADS, lane_head)
                q.append(qc), k.append(kc), v.append(vc), k4.append(kc4)
                beta.append(bt), cg_col.append(col), kb.append(kc * bt)
                g_last.append(col[CHUNK - 1:CHUNK, :] if d == 0 else col[0:1, :])
                decay.append(jnp.exp(jnp.where(incl_t[d], col - row_, NEG)))
        n_it = len(items)
        kk = [_mm1(kb[i], k4[i], NT) for i in range(n_it)]
        qk = [_mm1(q[i], k4[i], NT) for i in range(n_it)]
        a = [jnp.where(strict_t[items[i][0]], kk[i] * decay[i], 0.0) for i in range(n_it)]
        xu = [v[i] * beta[i] for i in range(n_it)]
        xw = [kb[i] * jnp.exp(cg_col[i]) for i in range(n_it)]

        units = [(i, h) for i in range(n_it) for h in range(N_HEADS)]
        dirs = [items[i][0] for i, h in units]
        ah = [a[i][:, _hs(h)] for i, h in units]
        x = [jnp.concatenate([xu[i][:, _hs(h)], xw[i][:, _hs(h)]], axis=1) for i, h in units]
        dg = [jnp.where(lvl_masks[d][0], a_, 0.0) for a_, d in zip(ah, dirs)]
        t = [eye64 - d_ for d_ in dg]
        p = [_MM_INV(d_, d_) for d_ in dg]
        pt = [_MM_INV(p_, t_) for p_, t_ in zip(p, t)]
        t = [t_ + u_ for t_, u_ in zip(t, pt)]
        p = [_MM_INV(p_, p_) for p_ in p]
        pt = [_MM_INV(p_, t_) for p_, t_ in zip(p, t)]
        t = [t_ + u_ for t_, u_ in zip(t, pt)]
        for lvl in range(1, len(lvl_masks[0])):
            lo = [jnp.where(lvl_masks[d][lvl], a_, 0.0) for a_, d in zip(ah, dirs)]
            lt = [_MM_INV(l_, t_) for l_, t_ in zip(lo, t)]
            tlt = [_MM_INV(t_, u_) for t_, u_ in zip(t, lt)]
            t = [t_ - u_ for t_, u_ in zip(t, tlt)]
        x = [_MM_APPLY(t_, x_) for t_, x_ in zip(t, x)]
        for i, (d, r0, r8) in enumerate(items):
            xi = x[i * N_HEADS:(i + 1) * N_HEADS]
            u_s[d, pl.ds(r0, CHUNK), :] = jnp.concatenate([x_[:, :HEAD_DIM] for x_ in xi], axis=1)
            w_s[d, pl.ds(r0, CHUNK), :] = jnp.concatenate([x_[:, HEAD_DIM:] for x_ in xi], axis=1).astype(BF16)
            att_s[d, pl.ds(r0, CHUNK), :] = (qk[i] * decay[i]).astype(BF16)
            qg_s[d, pl.ds(r0, CHUNK), :] = (q[i] * jnp.exp(cg_col[i])).astype(BF16)
            kd_s[d, pl.ds(r0, CHUNK), :] = (k[i] * jnp.exp(g_last[i] - cg_col[i])).astype(BF16)
            gl_s[d, pl.ds(r8, 8), :] = jnp.broadcast_to(jnp.exp(g_last[i]), (8, SEG))
        return carry

    lax.fori_loop(0, n_sub * n_chunks // group, solve_group, 0)

    def scan(n, carry):
        ds = [d for s, d in seq_dirs]
        rows = [_chunk_rows(d, n, n_chunks, s * T) for s, d in seq_dirs]
        rows8 = [pl.multiple_of((s * n_chunks + (n if d == 0 else n_chunks - 1 - n)) * 8, 8) for s, d in seq_dirs]
        st = [st_s[u] for u in seq_dirs]
        ws = [_mm1(w_s[d, pl.ds(r, CHUNK), :], x_) for d, r, x_ in zip(ds, rows, st)]
        qs_ = [_mm1(qg_s[d, pl.ds(r, CHUNK), :], x_) for d, r, x_ in zip(ds, rows, st)]
        v_new = [u_s[d, pl.ds(r, CHUNK), :] - x_ for d, r, x_ in zip(ds, rows, ws)]
        v4 = [_block_diag(x_, hm) for x_ in v_new]
        av = [_mm1(att_s[d, pl.ds(r, CHUNK), :], x_) for d, r, x_ in zip(ds, rows, v4)]
        kv = [_mm1(kd_s[d, pl.ds(r, CHUNK), :], x_, TN) for d, r, x_ in zip(ds, rows, v_new)]
        for i, (s, d) in enumerate(seq_dirs):
            st_s[s, d] = st[i] * gl_s[d, pl.ds(rows8[i], 1), :] + jnp.where(hm, kv[i], 0.0)
            (of_s, ob_s)[d][pl.ds(rows[i], CHUNK), :] = qs_[i] + av[i]
        return carry

    lax.fori_loop(0, n_chunks, scan, 0)
    if not has_cache:
        for s, d in seq_dirs:
            for h in range(N_HEADS):
                sf_refs[s][d, h] = st_s[s, d, _hs(h), _hs(h)]
    _epilogue(T * n_sub, of_s, ob_s, z_ref, _silu, bd_ref, ng_ref, o_ref)


def _deltanet(P, G, T, n_seq, gate_par, conv_w, ng, bd, l, state, prev):
    has_cache = state is not None
    st = (2, N_HEADS, HEAD_DIM, HEAD_DIM)
    in_specs = [_pick_spec((8, GATE_LANES), l), _pick_spec((3, 3 * SEG), l),
                _const_spec((1, SEG)), _const_spec((SEG, SEG))]
    args = [gate_par, conv_w, ng, bd]
    if has_cache:
        in_specs.append(_state_in_spec(l, st))
        args.append(state)
    n_sub = 1 if has_cache else SEQS_PER_STEP
    rows = n_sub * T
    scratch = ([pltpu.VMEM((rows, SEG), F32)] * 2 + [pltpu.VMEM((n_sub, 2, SEG, SEG), F32)]
               + [pltpu.VMEM((rows, SEG), F32)] * 3
               + [pltpu.VMEM((2, rows, SEG), F32)] + [pltpu.VMEM((2, rows, SEG), BF16)] * 4
               + [pltpu.VMEM((2, rows // CHUNK * 8, SEG), F32)])
    return _mixer_call(functools.partial(_delta_kernel, T=T, has_cache=has_cache, n_sub=n_sub), "deltanet",
                       T, n_seq, P, (S_AQ, S_AK, S_AV, S_AZ), G,
                       in_specs, args, scratch, () if has_cache else (st,), l, prev, interleaved=True)


def _diff_kernel(*refs, T, has_cache, lam_init):
    if has_cache:
        (q_ref, k_ref, v_ref, qg_ref, kg_ref, lam_ref, ng_ref, bd32_ref, bd64_ref,
         cos_ref, sin_ref, ck_ref, cv_ref, o_ref, qs, kh, vt) = refs
    else:
        (q_ref, k_ref, v_ref, qg_ref, kg_ref, lam_ref, ng_ref, bd32_ref, bd64_ref,
         o_ref, ko_ref, vo_ref, qs, kh, vt) = refs
    n_blk = T // ROW_BLOCK
    s0 = ck_ref.shape[1] if has_cache else 0
    bd32 = bd32_ref[...]
    lane = lax.broadcasted_iota(jnp.int32, (1, SEG), 1)
    first_half = (lane % 16) < 8

    if has_cache:
        for h in range(N_HEADS):
            kh[h, 0:s0, :] = ck_ref[h]
            vt[h, :, 0:s0] = cv_ref[h].T

    for i in range(n_blk):
        r0 = i * ROW_BLOCK
        for src, g_ref in ((q_ref, qg_ref), (k_ref, kg_ref)):
            x = src[pl.ds(r0, ROW_BLOCK), :]
            y = x * lax.rsqrt(_gsum(x * x, bd32) * (1.0 / DQK) + EPS) * g_ref[...]
            if has_cache:
                partner = jnp.where(first_half, pltpu.roll(y, SEG - 8, axis=1), pltpu.roll(y, 8, axis=1))
                y = y * cos_ref[pl.ds(r0, ROW_BLOCK), :] + partner * sin_ref[pl.ds(r0, ROW_BLOCK), :]
            if src is q_ref:
                qs[pl.ds(r0, ROW_BLOCK), :] = y
            else:
                for h in range(N_HEADS):
                    kh[h, pl.ds(s0 + r0, ROW_BLOCK), :] = y[:, _hs(h)]
                    if not has_cache:
                        ko_ref[h, pl.ds(r0, ROW_BLOCK), :] = y[:, _hs(h)]
        xv = v_ref[pl.ds(r0, ROW_BLOCK), :]
        xvt = xv.T
        for h in range(N_HEADS):
            vt[h, :, pl.ds(s0 + r0, ROW_BLOCK)] = xvt[_hs(h), :]
            if not has_cache:
                vo_ref[h, pl.ds(r0, ROW_BLOCK), :] = xv[:, _hs(h)]

    lp = lam_ref[...]
    lam = (jnp.exp(jnp.sum(lp[0:1, :] * lp[1:2, :], axis=1, keepdims=True))
           - jnp.exp(jnp.sum(lp[2:3, :] * lp[3:4, :], axis=1, keepdims=True)) + lam_init)
    scale = DQK ** -0.5 * math.log2(math.e)
    comp_rows = lax.broadcasted_iota(jnp.int32, (HEAD_DIM, 1), 0) < DQK
    bd64 = bd64_ref[...]
    ng = ng_ref[...]

    def qblock(i, carry):
        r0 = pl.multiple_of(i * Q_BLOCK, Q_BLOCK)
        qt = (qs[pl.ds(r0, Q_BLOCK), :] * scale).T
        qct = []
        for h in range(N_HEADS):
            qh = qt[_hs(h), :]
            q1 = jnp.where(comp_rows, qh, 0.0)
            qct += [q1, qh - q1]
        n_units = len(qct)
        st = _mm1(kh[0], qct[0])
        res = []
        for u in range(n_units):
            nxt = _mm1(kh[(u + 1) // 2], qct[u + 1]) if u + 1 < n_units else None
            e = jnp.exp2(st - st.max(axis=0, keepdims=True))
            res.append(_mm1(vt[u // 2], e) * (1.0 / e.sum(axis=0, keepdims=True)))
            st = nxt
        outs = [res[2 * h] - lam * res[2 * h + 1] for h in range(N_HEADS)]
        o = jnp.concatenate(outs, axis=0).T
        o_ref[pl.ds(r0, Q_BLOCK), :] = _head_norm_gate(o, bd64, ng, 1.0 - lam_init)
        return carry

    lax.fori_loop(0, T // Q_BLOCK, qblock, 0)


def _diffattn(P, T, n_seq, qg, kg, lam_par, ng, bd32, bd64, l, rope, cache, prev):
    has_cache = cache is not None
    lam_init = 0.8 - 0.6 * math.exp(-0.3 * l)
    kv = (N_HEADS, T, HEAD_DIM)
    in_specs = [_const_spec((1, SEG)), _const_spec((1, SEG)), _pick_spec((4, DQK), l), _const_spec((1, SEG)),
                _const_spec((SEG, SEG)), _const_spec((SEG, SEG))]
    args = [qg, kg, lam_par, ng, bd32, bd64]
    if has_cache:
        ckv = cache[0].shape[2:]
        in_specs += [_const_spec((T, SEG)), _const_spec((T, SEG)), _state_in_spec(l, ckv), _state_in_spec(l, ckv)]
        args += [rope[0], rope[1], cache[0], cache[1]]
    n_keys = T + (cache[0].shape[3] if has_cache else 0)
    scratch = [pltpu.VMEM((T, SEG), F32), pltpu.VMEM((N_HEADS, n_keys, HEAD_DIM), F32),
               pltpu.VMEM((N_HEADS, HEAD_DIM, n_keys), F32)]
    return _mixer_call(functools.partial(_diff_kernel, T=T, has_cache=has_cache, lam_init=lam_init),
                       "diff_attn", T, n_seq, P, (S_BQ, S_BK, S_BV), None,
                       in_specs, args, scratch, () if has_cache else (kv, kv), l, prev)


def _rope_tables(T):
    n_freq = DQK // 4
    t = jnp.arange(T)
    rows = (t // GRID_W).astype(F32)
    cols = (t % GRID_W).astype(F32)
    freqs = ROPE_BASE ** (-jnp.arange(n_freq, dtype=F32) / n_freq)
    ang_r, ang_c = rows[:, None] * freqs, cols[:, None] * freqs

    def comp(fn, sign):
        return jnp.concatenate([fn(ang_r), sign * fn(ang_r), fn(ang_c), sign * fn(ang_c)], axis=1)

    reps = SEG // DQK
    cos = jnp.tile(comp(jnp.cos, 1.0), (1, reps))
    sin = jnp.tile(jnp.concatenate([-jnp.sin(ang_r), jnp.sin(ang_r), -jnp.sin(ang_c), jnp.sin(ang_c)], axis=1),
                   (1, reps))
    return cos, sin


def _block_ones(group):
    i = np.arange(SEG)
    return jnp.asarray(i[:, None] // group == i[None, :] // group, BF16)


def kernel(x_prompt, x_sample, cache_diff_k, cache_diff_v, state_delta, state_mlstm_C, state_mlstm_n, state_mlstm_m, state_ret, c, c_ctx, w_ada, b_ada, norm_g, ffn_w_gate, ffn_w_up, ffn_w_down, w_in, dn_conv_w, dn_a_log, dn_dt_bias, dn_norm_g, da_qn_g, da_kn_g, da_lambda, da_norm_g, ml_i_bias, ml_f_bias, ml_norm_g, ret_decay_logit, ret_norm_g, w_branch, w_out):
    B, T, _ = x_prompt.shape
    Bs, Ts, _ = x_sample.shape
    cond8 = jnp.concatenate([c_ctx[None], c, jnp.zeros((8 - 1 - Bs, D_MODEL), F32)], axis=0)
    mod = _ada(cond8, w_ada, b_ada).reshape(DEPTH, 8, N_MOD, D_MODEL)
    bd32, bd64 = _block_ones(DQK), _block_ones(HEAD_DIM)
    rope = _rope_tables(Ts)
    tile_heads = lambda g: jnp.tile(g, SEG // g.shape[0])[None]

    wg, wu, wd = ffn_w_gate.astype(BF16), ffn_w_up.astype(BF16), ffn_w_down.astype(BF16)
    cols = lambda f: w_in[:, :, _IN_OFFS[f]:_IN_OFFS[f + 1]]
    w_mix = jnp.concatenate([cols(f) for f in _SEG_FIELDS], axis=2).astype(BF16)
    w_gate = jnp.concatenate([cols(f) for f in _GATE_FIELDS]
                             + [jnp.zeros((DEPTH, D_MODEL, GATE_LANES - 32), F32)], axis=2).astype(BF16)
    w_merge = cols(19).astype(BF16)
    wb, wo = w_branch.astype(BF16), w_out.astype(BF16)
    norm_g4 = norm_g.reshape(DEPTH, 3, 1, D_MODEL)
    lanes = lambda *parts: jnp.concatenate([p.reshape(DEPTH, 1, -1) for p in parts], axis=2)
    z8 = jnp.zeros((DEPTH, 8), F32)
    gate_par = jnp.concatenate([
        lanes(z8, dn_dt_bias, ml_i_bias, ml_f_bias, jnp.zeros((DEPTH, GATE_LANES - 32), F32)),
        lanes(z8, dn_a_log, jnp.zeros((DEPTH, GATE_LANES - 16), F32)),
        jnp.zeros((DEPTH, 6, GATE_LANES), F32)], axis=1)
    ml_state = (state_mlstm_C, state_mlstm_n.reshape(Bs, DEPTH, 2, 1, SEG),
                jnp.repeat(state_mlstm_m, HEAD_DIM, axis=-1).reshape(Bs, DEPTH, 2, 1, SEG))

    xs = {"ctx": x_prompt.reshape(B * T, D_MODEL), "smp": x_sample.reshape(Bs * Ts, D_MODEL)}
    geo = {"ctx": (T, B, B * T, 0), "smp": (Ts, Bs, Ts, 1)}
    states = {}
    for l in range(DEPTH):
        for path in ("ctx", "smp"):
            Tp, n_seq, rows_per_cond, first_cond = geo[path]
            mspec = _mod_spec(l, rows_per_cond, first_cond)
            smp = path == "smp"
            prev = states.get if not smp else (lambda name: None)
            x = _ffn(xs[path], mod, mspec, norm_g4, wg, wu, wd, l, 0)
            P, G = _inproj(x, mod, mspec, norm_g4, w_mix, w_gate, l)
            ra = _deltanet(P, G, Tp, n_seq, gate_par, dn_conv_w, tile_heads(dn_norm_g[l]), bd64, l,
                           state_delta if smp else None, prev("dn"))
            rb = _diffattn(P, Tp, n_seq, tile_heads(da_qn_g[l]), tile_heads(da_kn_g[l]), da_lambda,
                           tile_heads(da_norm_g[l]), bd32, bd64, l, rope if smp else None,
                           (cache_diff_k, cache_diff_v) if smp else None, prev("kv"))
            rc = _mlstm(P, G, Tp, n_seq, gate_par, tile_heads(ml_norm_g[l]), bd64, l,
                        ml_state if smp else None, prev("ml"))
            rd = _retention(P, Tp, n_seq, ret_decay_logit[l], tile_heads(ret_norm_g[l]), bd64, l,
                            state_ret if smp else None, prev("r"))
            if not smp:
                states = {"dn": ra[1:], "kv": rb[1:], "ml": rc[1:], "r": rd[1:]}
            x = _merge(x, mod, mspec, norm_g4, (ra[0], rb[0], rc[0], rd[0]), w_merge, wb, wo, l)
            xs[path] = _ffn(x, mod, mspec, norm_g4, wg, wu, wd, l, 1)
    (new_dn,), (new_k, new_v), (new_c, new_n, new_m), (new_r,) = (states[k] for k in ("dn", "kv", "ml", "r"))
    per_head = lambda rows: rows.reshape(B, DEPTH, 2, N_HEADS, HEAD_DIM)
    return (xs["ctx"].reshape(B, T, D_MODEL), xs["smp"].reshape(Bs, Ts, D_MODEL),
            new_k, new_v, new_dn, new_c, per_head(new_n), per_head(new_m)[..., 0], new_r)
```

```python
import functools
import math

import numpy as np
import jax
import jax.numpy as jnp
from jax import lax
from jax.experimental import pallas as pl
from jax.experimental.pallas import tpu as pltpu

F32 = jnp.float32
BF16 = jnp.bfloat16

D_MODEL = 1024
FFN_DIM = 2816
N_MOD = 9
DEPTH = 2
N_HEADS = 4
HEAD_DIM = 64
SEG = N_HEADS * HEAD_DIM
N_SEG = 15
CHUNK = 64
DQK = 32
GRID_W = 64
ROPE_BASE = 10000.0
EPS = 1e-6
N_BRANCH = 4
GATE_LANES = 128
NEG = -1e30

TM = 512
Q_BLOCK = 256
ROW_BLOCK = 128
SEQS_PER_STEP = 4
VMEM_LIMIT = 56 * 1024 * 1024

NN = (((1,), (0,)), ((), ()))
NT = (((1,), (1,)), ((), ()))
TN = (((0,), (0,)), ((), ()))

_IN_SIZES = (256, 256, 256, 256, 8, 8, 256, 256, 256, 256, 256, 256, 256, 8, 8, 256, 256, 256, 256, 4096)
_IN_OFFS = np.concatenate([[0], np.cumsum(_IN_SIZES)]).tolist()
_SEG_FIELDS = (0, 1, 2, 3, 6, 7, 8, 9, 10, 11, 12, 15, 16, 17, 18)
_GATE_FIELDS = (4, 5, 13, 14)
(S_AQ, S_AK, S_AV, S_AZ, S_BQ, S_BK, S_BV, S_CQ, S_CK, S_CV, S_CO, S_DQ, S_DK, S_DV, S_DG) = range(N_SEG)
L_BETA, L_ALPHA, L_CI, L_CF = 0, 8, 16, 24


def _dg(a, b, dims):
    return lax.dot_general(a, b, dims, preferred_element_type=F32)


def _split2(x):
    hi = x.astype(BF16)
    lo = (x - hi.astype(F32)).astype(BF16)
    return hi, lo


def _mm1(a, b, dims=NN):
    return _dg(a.astype(BF16), b.astype(BF16), dims)


def _mm3(a, b, dims=NN):
    ah, al = _split2(a)
    bh, bl = _split2(b)
    return _dg(ah, bh, dims) + (_dg(ah, bl, dims) + _dg(al, bh, dims))


def _mm2r(a, b, dims=NN):
    ah = a.astype(BF16)
    bh, bl = _split2(b)
    return _dg(ah, bh, dims) + _dg(ah, bl, dims)


def _sel_mm(sel, x, dims=NN):
    h0 = x.astype(BF16)
    r1 = x - h0.astype(F32)
    h1 = r1.astype(BF16)
    h2 = (r1 - h1.astype(F32)).astype(BF16)
    return _dg(sel, h0, dims) + (_dg(sel, h1, dims) + _dg(sel, h2, dims))


def _gsum(x, bd):
    hi, lo = _split2(x)
    return _dg(hi, bd, NN) + _dg(lo, bd, NN)


def _sigmoid(x):
    return 1.0 / (1.0 + jnp.exp(-x))


def _silu(x):
    return x * _sigmoid(x)


def _softplus(x):
    return jnp.maximum(x, 0.0) + jnp.log1p(jnp.exp(-jnp.abs(x)))


def _log_sigmoid(x):
    return -_softplus(-x)


def _norm_mod(x, g, shift, scale):
    ms = jnp.mean(x * x, axis=-1, keepdims=True)
    return (x * lax.rsqrt(ms + EPS) * g) * (1.0 + scale) + shift


def _tri_masks():
    ii = lax.broadcasted_iota(jnp.int32, (CHUNK, CHUNK), 0)
    jj = lax.broadcasted_iota(jnp.int32, (CHUNK, CHUNK), 1)
    return ((jj <= ii, jj < ii), (jj >= ii, jj > ii))


def _cparams(n_grid=1):
    return pltpu.CompilerParams(dimension_semantics=("arbitrary",) * n_grid,
                                vmem_limit_bytes=VMEM_LIMIT)


def _const_spec(shape):
    nd = len(shape)
    return pl.BlockSpec(shape, lambda *_: (0,) * nd)


def _ada_kernel(s_ref, w_ref, b_ref, o_ref):
    s = s_ref[...]
    o_ref[...] = _mm3(_silu(s), w_ref[...]) + b_ref[...]


def _ada(cond8, w_ada, b_ada):
    tn = 1536
    n_t = (N_MOD * D_MODEL) // tn
    return pl.pallas_call(
        _ada_kernel,
        grid=(DEPTH, n_t),
        in_specs=[pl.BlockSpec((8, D_MODEL), lambda l, j: (0, 0)),
                  pl.BlockSpec((None, D_MODEL, tn), lambda l, j: (l, 0, j)),
                  pl.BlockSpec((None, 1, tn), lambda l, j: (l, 0, j))],
        out_specs=pl.BlockSpec((None, 8, tn), lambda l, j: (l, 0, j)),
        out_shape=jax.ShapeDtypeStruct((DEPTH, 8, N_MOD * D_MODEL), F32),
        compiler_params=_cparams(2),
        name="ada_mod",
    )(cond8, w_ada, b_ada.reshape(DEPTH, 1, N_MOD * D_MODEL))


def _ffn_kernel(x_ref, mod_ref, g_ref, wg_ref, wu_ref, wd_ref, o_ref, *, mi):
    x = x_ref[...]
    h = _norm_mod(x, g_ref[...], mod_ref[mi:mi + 1, :], mod_ref[mi + 1:mi + 2, :]).astype(BF16)
    gate = jnp.dot(h, wg_ref[...], preferred_element_type=F32)
    up = jnp.dot(h, wu_ref[...], preferred_element_type=F32)
    act = (_silu(gate) * up).astype(BF16)
    y = jnp.dot(act, wd_ref[...], preferred_element_type=F32)
    o_ref[...] = x + (0.5 * mod_ref[mi + 2:mi + 3, :]) * y


def _pick_spec(tail, *lead):
    nd = len(tail)
    return pl.BlockSpec((None,) * len(lead) + tuple(tail), lambda *_: tuple(lead) + (0,) * nd,
                        pipeline_mode=pl.Buffered(1))


def _mod_spec(l, rows_per_cond, first_cond):
    per = rows_per_cond // TM
    return pl.BlockSpec((None, None, N_MOD, D_MODEL), lambda i: (l, first_cond + i // per, 0, 0))


def _ffn(x, mod, mod_spec, norm_g, wg, wu, wd, l, j):
    n = x.shape[0]
    return pl.pallas_call(
        functools.partial(_ffn_kernel, mi=6 * j),
        grid=(n // TM,),
        in_specs=[pl.BlockSpec((TM, D_MODEL), lambda i: (i, 0)),
                  mod_spec,
                  _pick_spec((1, D_MODEL), l, 2 * j),
                  _pick_spec((D_MODEL, FFN_DIM), l, j),
                  _pick_spec((D_MODEL, FFN_DIM), l, j),
                  _pick_spec((FFN_DIM, D_MODEL), l, j)],
        out_specs=pl.BlockSpec((TM, D_MODEL), lambda i: (i, 0)),
        out_shape=jax.ShapeDtypeStruct((n, D_MODEL), F32),
        compiler_params=_cparams(1),
        name="ffn",
    )(x, mod, norm_g, wg, wu, wd)


def _inproj_kernel(x_ref, mod_ref, g_ref, w_ref, wgt_ref, p_ref, gt_ref):
    h = _norm_mod(x_ref[...], g_ref[...], mod_ref[3:4, :], mod_ref[4:5, :]).astype(BF16)
    step = 3 * SEG
    for j in range(0, N_SEG * SEG, step):
        p_ref[:, j:j + step] = jnp.dot(h, w_ref[:, j:j + step], preferred_element_type=F32)
    gt_ref[...] = jnp.dot(h, wgt_ref[...], preferred_element_type=F32)


def _inproj(x, mod, mod_spec, norm_g, w_mix, w_gate, l):
    n = x.shape[0]
    return pl.pallas_call(
        _inproj_kernel,
        grid=(n // TM,),
        in_specs=[pl.BlockSpec((TM, D_MODEL), lambda i: (i, 0)),
                  mod_spec,
                  _pick_spec((1, D_MODEL), l, 1),
                  _pick_spec((D_MODEL, N_SEG * SEG), l),
                  _pick_spec((D_MODEL, GATE_LANES), l)],
        out_specs=[pl.BlockSpec((TM, N_SEG * SEG), lambda i: (i, 0)),
                   pl.BlockSpec((TM, GATE_LANES), lambda i: (i, 0))],
        out_shape=[jax.ShapeDtypeStruct((n, N_SEG * SEG), F32),
                   jax.ShapeDtypeStruct((n, GATE_LANES), F32)],
        compiler_params=_cparams(1),
        name="in_proj",
    )(x, mod, norm_g, w_mix, w_gate)


def _merge_kernel(x_ref, mod_ref, g_ref, ba_ref, bb_ref, bc_ref, bd_ref, wm_ref, wb_ref, wo_ref, o_ref):
    x = x_ref[...]
    h = _norm_mod(x, g_ref[...], mod_ref[3:4, :], mod_ref[4:5, :]).astype(BF16)
    mixed = None
    for m, b_ref in enumerate((ba_ref, bb_ref, bc_ref, bd_ref)):
        logits = jnp.dot(h, wm_ref[:, m * D_MODEL:(m + 1) * D_MODEL], preferred_element_type=F32)
        pb = jnp.dot(b_ref[...].astype(BF16), wb_ref[m], preferred_element_type=F32)
        term = _sigmoid(logits) * pb
        mixed = term if mixed is None else mixed + term
    y = jnp.dot(mixed.astype(BF16), wo_ref[...], preferred_element_type=F32)
    o_ref[...] = x + mod_ref[5:6, :] * y


def _merge(x, mod, mod_spec, norm_g, branches, w_merge, w_branch, w_out, l):
    n = x.shape[0]
    row = lambda w: pl.BlockSpec((TM, w), lambda i: (i, 0))
    return pl.pallas_call(
        _merge_kernel,
        grid=(n // TM,),
        in_specs=[row(D_MODEL), mod_spec, _pick_spec((1, D_MODEL), l, 1),
                  row(SEG), row(SEG), row(SEG), row(SEG),
                  _pick_spec((D_MODEL, N_BRANCH * D_MODEL), l),
                  _pick_spec((N_BRANCH, SEG, D_MODEL), l),
                  _pick_spec((D_MODEL, D_MODEL), l)],
        out_specs=row(D_MODEL),
        out_shape=jax.ShapeDtypeStruct((n, D_MODEL), F32),
        compiler_params=_cparams(1),
        name="merge",
    )(x, mod, norm_g, *branches, w_merge, w_branch, w_out)


def _state_in_spec(l, tail):
    nd = len(tail)
    return pl.BlockSpec((None, None) + tail, lambda s: (s, l) + (0,) * nd)


def _all_sequences(kernel, n_sub, n_in, n_alias, n_state, l, creates, *refs):
    ins, rest = refs[:n_in], refs[n_in + n_alias:]
    main, states, scratch = rest[0], rest[1:1 + n_state], rest[1 + n_state:]
    if creates:
        for r in states:
            for other in range(DEPTH):
                if other != l:
                    r[:, other] = jnp.zeros((n_sub,) + r.shape[2:], F32)
    own = [r.at[sub, l] if creates else r.at[sub] for r in states for sub in range(n_sub)]
    kernel(*ins, main, *own, *scratch)


def _mixer_call(kernel, name, T, n_seq, P, segs, G, in_specs, args, scratch, state_tails=(), l=0, prev=None):
    creates = prev is None
    n_sub = SEQS_PER_STEP if state_tails else 1
    rows = T * n_sub
    row_specs = [pl.BlockSpec((rows, SEG), lambda s, j=j: (s, j)) for j in segs]
    row_args = [P] * len(segs)
    if G is not None:
        row_specs.append(pl.BlockSpec((rows, GATE_LANES), lambda s: (s, 0)))
        row_args.append(G)
    in_specs, args = row_specs + in_specs, row_args + args
    n_in, n_alias = len(args), 0 if creates else len(prev)

    def state_spec(t):
        if creates:
            return pl.BlockSpec((n_sub, DEPTH) + t, lambda s: (s,) + (0,) * (1 + len(t)))
        return pl.BlockSpec((n_sub, None) + t, lambda s: (s, l) + (0,) * len(t))

    out_specs = [pl.BlockSpec((rows, SEG), lambda s: (s, 0))] + [state_spec(t) for t in state_tails]
    out_shape = ([jax.ShapeDtypeStruct((n_seq * T, SEG), F32)]
                 + [jax.ShapeDtypeStruct((n_seq, DEPTH) + t, F32) for t in state_tails])
    aliases = {}
    if not creates:
        in_specs = in_specs + [pl.BlockSpec(memory_space=pl.ANY)] * n_alias
        args = args + list(prev)
        aliases = {n_in + i: 1 + i for i in range(n_alias)}
    body = functools.partial(_all_sequences, kernel, n_sub, n_in, n_alias, len(state_tails), l, creates)
    return pl.pallas_call(
        body, grid=(n_seq // n_sub,), in_specs=in_specs, out_specs=out_specs, out_shape=out_shape,
        scratch_shapes=scratch, input_output_aliases=aliases, compiler_params=_cparams(1), name=name)(*args)


def _head_norm_gate(o, bd, ng, gate):
    ss = _gsum(o * o, bd)
    return o * lax.rsqrt(ss * (1.0 / HEAD_DIM) + EPS) * ng * gate


def _epilogue(T, of_s, ob_s, gate_ref, gate_fn, bd_ref, ng_ref, o_ref):
    rb = 256
    bd = bd_ref[...]
    ng = ng_ref[...]

    def blk(i, carry):
        r0 = pl.multiple_of(i * rb, rb)
        o = of_s[pl.ds(r0, rb), :] + ob_s[pl.ds(r0, rb), :]
        o_ref[pl.ds(r0, rb), :] = _head_norm_gate(o, bd, ng, gate_fn(gate_ref[pl.ds(r0, rb), :]))
        return carry

    lax.fori_loop(0, T // rb, blk, 0)


def _chunk_rows(d, n, n_chunks, base=0):
    c = n if d == 0 else n_chunks - 1 - n
    return pl.multiple_of(base + c * CHUNK, CHUNK)


def _hs(h):
    return slice(h * HEAD_DIM, (h + 1) * HEAD_DIM)


def _head_mask():
    ii = lax.broadcasted_iota(jnp.int32, (SEG, SEG), 0) // HEAD_DIM
    jj = lax.broadcasted_iota(jnp.int32, (SEG, SEG), 1) // HEAD_DIM
    return ii == jj


def _blocksum(x):
    c = CHUNK
    return (x[0:c] + x[c:2 * c]) + (x[2 * c:3 * c] + x[3 * c:4 * c])


def _lane_head():
    return lax.broadcasted_iota(jnp.int32, (1, SEG), 1) // HEAD_DIM


def _tiled_masks():
    ii = lax.broadcasted_iota(jnp.int32, (CHUNK, SEG), 0)
    jj = lax.broadcasted_iota(jnp.int32, (CHUNK, SEG), 1) % HEAD_DIM
    return ((jj <= ii, jj < ii), (jj >= ii, jj > ii)), jj == ii


def _block_diag(x, hm):
    return jnp.where(hm, jnp.concatenate([x] * N_HEADS, axis=0), 0.0)


def _col_dense(g, lane0, lane_head):
    out = jnp.broadcast_to(g[:, lane0:lane0 + 1], (CHUNK, SEG))
    for h in range(1, N_HEADS):
        out = jnp.where(lane_head == h, g[:, lane0 + h:lane0 + h + 1], out)
    return out


def _diag_row(col_dense, eye_t):
    return jnp.sum(jnp.where(eye_t, col_dense, 0.0), axis=0, keepdims=True)


def _seg_max(x, lane_head):
    out = None
    for h in range(N_HEADS):
        m = jnp.max(jnp.where(lane_head == h, x, NEG), axis=1, keepdims=True)
        out = m if out is None else jnp.where(lane_head == h, m, out)
    return jnp.broadcast_to(out, x.shape)


UNITS = tuple((d, h) for d in range(2) for h in range(N_HEADS))


def _ret_kernel(*refs, T, has_cache, n_sub):
    if has_cache:
        (q_ref, k_ref, v_ref, gt_ref, dl_ref, ng_ref, bd_ref, r0_ref, o_ref,
         of_s, ob_s, st_s, dec_s, qdec_s, kdec_s, cdec_s) = refs
    else:
        q_ref, k_ref, v_ref, gt_ref, dl_ref, ng_ref, bd_ref, o_ref = refs[:8]
        rf_refs = refs[8:8 + n_sub]
        of_s, ob_s, st_s, dec_s, qdec_s, kdec_s, cdec_s = refs[8 + n_sub:]
    n_chunks = T // CHUNK
    units = [(s, d) for s in range(n_sub) for d in range(2)]
    masks = _tri_masks()
    hm = _head_mask()

    @pl.when(pl.program_id(0) == 0)
    def _():
        ii = lax.broadcasted_iota(jnp.int32, (CHUNK, CHUNK), 0)
        jj = lax.broadcasted_iota(jnp.int32, (CHUNK, CHUNK), 1)
        rel = (ii - jj).astype(F32)
        pos = lax.broadcasted_iota(jnp.int32, (CHUNK, 1), 0).astype(F32)
        lg_all = _log_sigmoid(dl_ref[...])
        for d in range(2):
            dec, qdec, kdec, cdec = [], [], [], []
            for h in range(N_HEADS):
                lg = lg_all[d:d + 1, h:h + 1]
                if d == 0:
                    e, qd, kd = rel * lg, (pos + 1.0) * lg, (CHUNK - 1.0 - pos) * lg
                else:
                    e, qd, kd = -rel * lg, (CHUNK - pos) * lg, pos * lg
                dec.append(jnp.exp(jnp.where(masks[d][0], e, NEG)))
                qdec.append(jnp.broadcast_to(jnp.exp(qd), (CHUNK, HEAD_DIM)))
                kdec.append(jnp.broadcast_to(jnp.exp(kd), (CHUNK, HEAD_DIM)))
                cdec.append(jnp.broadcast_to(jnp.exp(CHUNK * lg), (HEAD_DIM, SEG)))
            dec_s[d] = jnp.concatenate(dec, axis=0)
            qdec_s[d] = jnp.concatenate(qdec, axis=1)
            kdec_s[d] = jnp.concatenate(kdec, axis=1)
            cdec_s[d] = jnp.concatenate(cdec, axis=0)

    for s, d in units:
        st_s[s, d] = jnp.zeros((SEG, SEG), F32)
        if has_cache:
            for h in range(N_HEADS):
                st_s[s, d, _hs(h), _hs(h)] = r0_ref[d, h]

    def body(n, carry):
        rows = [_chunk_rows(d, n, n_chunks, s * T) for s, d in units]
        q = [q_ref[pl.ds(r0, CHUNK), :] for r0 in rows]
        k = [k_ref[pl.ds(r0, CHUNK), :] * (HEAD_DIM ** -0.5) for r0 in rows]
        v = [v_ref[pl.ds(r0, CHUNK), :] for r0 in rows]
        st = [st_s[u] for u in units]
        q4 = [jnp.where(hm, jnp.concatenate([x] * N_HEADS, axis=0), 0.0) for x in q]
        qk = [_mm1(a, b, NT) for a, b in zip(q4, k)]
        qr = [_mm1(q[i] * qdec_s[d], st[i]) for i, (s, d) in enumerate(units)]
        kv = [_mm1(k[i] * kdec_s[d], v[i], TN) for i, (s, d) in enumerate(units)]
        av = [_mm1(qk[i] * dec_s[d], v[i]) for i, (s, d) in enumerate(units)]
        for i, (s, d) in enumerate(units):
            st_s[s, d] = cdec_s[d] * st[i] + jnp.where(hm, kv[i], 0.0)
            (of_s, ob_s)[d][pl.ds(rows[i], CHUNK), :] = qr[i] + _blocksum(jnp.where(hm, av[i], 0.0))
        return carry

    lax.fori_loop(0, n_chunks, body, 0)
    if not has_cache:
        for s, d in units:
            for h in range(N_HEADS):
                rf_refs[s][d, h] = st_s[s, d, _hs(h), _hs(h)]
    _epilogue(T * n_sub, of_s, ob_s, gt_ref, _silu, bd_ref, ng_ref, o_ref)


def _retention(P, T, n_seq, decay_logit, ng, bd, l, state, prev):
    has_cache = state is not None
    st = (2, N_HEADS, HEAD_DIM, HEAD_DIM)
    in_specs = [_const_spec((2, N_HEADS)), _const_spec((1, SEG)), _const_spec((SEG, SEG))]
    args = [decay_logit, ng, bd]
    if has_cache:
        in_specs.append(_state_in_spec(l, st))
        args.append(state)
    n_sub = 1 if has_cache else SEQS_PER_STEP
    scratch = [pltpu.VMEM((n_sub * T, SEG), F32), pltpu.VMEM((n_sub * T, SEG), F32),
               pltpu.VMEM((n_sub, 2, SEG, SEG), F32),
               pltpu.VMEM((2, SEG, HEAD_DIM), F32), pltpu.VMEM((2, CHUNK, SEG), F32),
               pltpu.VMEM((2, CHUNK, SEG), F32), pltpu.VMEM((2, SEG, SEG), F32)]
    return _mixer_call(functools.partial(_ret_kernel, T=T, has_cache=has_cache, n_sub=n_sub), "retention",
                       T, n_seq, P, (S_DQ, S_DK, S_DV, S_DG), None,
                       in_specs, args, scratch, () if has_cache else (st,), l, prev)


def _mlstm_kernel(*refs, T, has_cache, n_sub):
    if has_cache:
        (q_ref, k_ref, v_ref, og_ref, gt_ref, gp_ref, ng_ref, bd_ref, c0_ref, n0_ref, m0_ref,
         o_ref, of_s, ob_s, c_s, n_s, m_s) = refs
    else:
        q_ref, k_ref, v_ref, og_ref, gt_ref, gp_ref, ng_ref, bd_ref, o_ref = refs[:9]
        cf_refs, nf_refs, mf_refs = (refs[9 + i * n_sub:9 + (i + 1) * n_sub] for i in range(3))
        of_s, ob_s, c_s, n_s, m_s = refs[9 + 3 * n_sub:]
    n_chunks = T // CHUNK
    units = [(s, d) for s in range(n_sub) for d in range(2)]
    masks = _tri_masks()
    tri = (masks[0][0].astype(BF16), masks[1][0].astype(BF16))
    masks_t, eye_t = _tiled_masks()
    tri_t = [masks_t[d][0] for d in range(2)]
    hm = _head_mask()
    lane_head = _lane_head()
    bd = bd_ref[...]
    bias = gp_ref[0:1, :]

    for s, d in units:
        c_s[s, d] = jnp.zeros((SEG, SEG), F32)
        if has_cache:
            for h in range(N_HEADS):
                c_s[s, d, _hs(h), _hs(h)] = c0_ref[d, h]
            n_s[s, d] = n0_ref[d]
            m_s[s, d] = m0_ref[d]
        else:
            n_s[s, d] = jnp.zeros((1, SEG), F32)
            m_s[s, d] = jnp.zeros((1, SEG), F32)

    def body(n, carry):
        rows = [_chunk_rows(d, n, n_chunks, s * T) for s, d in units]
        q = [q_ref[pl.ds(r0, CHUNK), :] * (HEAD_DIM ** -0.5) for r0 in rows]
        k = [k_ref[pl.ds(r0, CHUNK), :] for r0 in rows]
        v = [v_ref[pl.ds(r0, CHUNK), :] for r0 in rows]
        pre = [gt_ref[pl.ds(r0, CHUNK), :] + bias for r0 in rows]
        b = [_sel_mm(tri[d], _log_sigmoid(pre[i])) for i, (s, d) in enumerate(units)]
        c_prev = [c_s[u] for u in units]
        n_prev = [n_s[u] for u in units]

        k4 = [_block_diag(x, hm) for x in k]
        v4 = [_block_diag(x, hm) for x in v]
        qk = [_mm1(a_, b_, NT) for a_, b_ in zip(q, k4)]
        qc = [_mm1(a_, b_) for a_, b_ in zip(q, c_prev)]
        qn = [_gsum(a_ * b_, bd) for a_, b_ in zip(q, n_prev)]

        sc, w_inter, m_i, wk, dec, m_new = [], [], [], [], [], []
        for i, (s, d) in enumerate(units):
            b_col = _col_dense(b[i], L_CF + d * N_HEADS, lane_head)
            ig_col = _col_dense(pre[i], L_CI + d * N_HEADS, lane_head)
            b_row, ig_row = _diag_row(b_col, eye_t), _diag_row(ig_col, eye_t)
            last = CHUNK - 1 if d == 0 else 0
            b_last = b_col[last:last + 1, :]
            m_prev = m_s[s, d]
            dm = jnp.where(tri_t[d], b_col - b_row + ig_row, NEG)
            inter = b_col + m_prev
            mi = jnp.maximum(inter, _seg_max(dm, lane_head))
            sc.append(qk[i] * jnp.exp(dm - mi))
            w_inter.append(jnp.exp(inter - mi))
            m_i.append(mi)
            mn = jnp.maximum(b_last + m_prev, _seg_max(b_last - b_row + ig_row, lane_head))
            wk.append(k[i] * jnp.exp(b_last - b_col + ig_col - mn))
            dec.append(jnp.exp(b_last + m_prev - mn))
            m_new.append(mn)

        sv = [_mm1(a_, b_) for a_, b_ in zip(sc, v4)]
        ssum = [_gsum(a_, bd) for a_ in sc]
        kv = [_mm1(a_, b_, TN) for a_, b_ in zip(wk, v)]
        for i, (s, d) in enumerate(units):
            num = w_inter[i] * qc[i] + sv[i]
            den = w_inter[i] * qn[i] + ssum[i]
            (of_s, ob_s)[d][pl.ds(rows[i], CHUNK), :] = num / jnp.maximum(jnp.abs(den), jnp.exp(-m_i[i]))
            c_s[s, d] = dec[i] * c_prev[i] + jnp.where(hm, kv[i], 0.0)
            n_s[s, d] = dec[i] * n_prev[i] + jnp.sum(wk[i], axis=0, keepdims=True)
            m_s[s, d] = m_new[i]
        return carry

    lax.fori_loop(0, n_chunks, body, 0)
    if not has_cache:
        for s, d in units:
            for h in range(N_HEADS):
                cf_refs[s][d, h] = c_s[s, d, _hs(h), _hs(h)]
            nf_refs[s][d] = n_s[s, d]
            mf_refs[s][d] = m_s[s, d]
    _epilogue(T * n_sub, of_s, ob_s, og_ref, _sigmoid, bd_ref, ng_ref, o_ref)


def _mlstm(P, G, T, n_seq, gate_par, ng, bd, l, state, prev):
    has_cache = state is not None
    st = (2, N_HEADS, HEAD_DIM, HEAD_DIM)
    rw = (2, 1, SEG)
    in_specs = [_pick_spec((8, GATE_LANES), l), _const_spec((1, SEG)), _const_spec((SEG, SEG))]
    args = [gate_par, ng, bd]
    if has_cache:
        in_specs += [_state_in_spec(l, st), _state_in_spec(l, rw), _state_in_spec(l, rw)]
        args += list(state)
    n_sub = 1 if has_cache else SEQS_PER_STEP
    scratch = [pltpu.VMEM((n_sub * T, SEG), F32), pltpu.VMEM((n_sub * T, SEG), F32),
               pltpu.VMEM((n_sub, 2, SEG, SEG), F32), pltpu.VMEM((n_sub,) + rw, F32), pltpu.VMEM((n_sub,) + rw, F32)]
    return _mixer_call(functools.partial(_mlstm_kernel, T=T, has_cache=has_cache, n_sub=n_sub), "mlstm",
                       T, n_seq, P, (S_CQ, S_CK, S_CV, S_CO), G,
                       in_specs, args, scratch, () if has_cache else (st, rw, rw), l, prev)


INV_BASE = 8
SOLVE_GROUP = 4
_MM_INV = _mm1
_MM_APPLY = _mm2r


def _inverse_level_masks():
    ii = lax.broadcasted_iota(jnp.int32, (CHUNK, CHUNK), 0)
    jj = lax.broadcasted_iota(jnp.int32, (CHUNK, CHUNK), 1)
    out = []
    for lo, hi in ((jj, ii), (ii, jj)):
        lv = [(lo // INV_BASE == hi // INV_BASE) & (lo < hi)]
        size = 2 * INV_BASE
        while size <= CHUNK:
            lv.append((lo // size == hi // size) & (hi % size >= size // 2) & (lo % size < size // 2))
            size *= 2
        out.append(lv)
    return out


def _delta_kernel(*refs, T, has_cache, n_sub):
    if has_cache:
        (q_ref, k_ref, v_ref, z_ref, gt_ref, gp_ref, cw_ref, ng_ref, bd_ref, s0_ref,
         o_ref, of_s, ob_s, st_s, qs, ks, vs, u_s, w_s, att_s, qg_s, kd_s, gl_s) = refs
    else:
        q_ref, k_ref, v_ref, z_ref, gt_ref, gp_ref, cw_ref, ng_ref, bd_ref, o_ref = refs[:10]
        sf_refs = refs[10:10 + n_sub]
        of_s, ob_s, st_s, qs, ks, vs, u_s, w_s, att_s, qg_s, kd_s, gl_s = refs[10 + n_sub:]
    n_chunks = T // CHUNK
    group = min(SOLVE_GROUP, n_chunks * n_sub)
    n_blk = T // ROW_BLOCK
    seq_dirs = [(s, d) for s in range(n_sub) for d in range(2)]
    masks = _tri_masks()
    tri = (masks[0][0].astype(BF16), masks[1][0].astype(BF16))
    masks_t, eye_t = _tiled_masks()
    incl_t = [masks_t[d][0] for d in range(2)]
    strict_t = [masks_t[d][1] for d in range(2)]
    hm = _head_mask()
    lane_head = _lane_head()
    lvl_masks = _inverse_level_masks()
    eye64 = (masks[0][0] & masks[1][0]).astype(F32)
    bd = bd_ref[...]
    bias, a_log = gp_ref[0:1, :], gp_ref[1:2, :]
    row = lax.broadcasted_iota(jnp.int32, (ROW_BLOCK, 1), 0)

    def prologue(i, carry):
        r0 = pl.multiple_of(i * ROW_BLOCK, ROW_BLOCK)
        rp = pl.multiple_of(jnp.maximum(r0 - 8, 0), 8)
        rn = pl.multiple_of(jnp.minimum(r0 + ROW_BLOCK, n_sub * T - 8), 8)
        in_seq = lax.rem(i, n_blk)
        for j, (src, dst) in enumerate(((q_ref, qs), (k_ref, ks), (v_ref, vs))):
            cur = src[pl.ds(r0, ROW_BLOCK), :]
            before = jnp.where(in_seq > 0, src[pl.ds(rp, 8), :][7:8, :], 0.0)
            after = jnp.where(in_seq < n_blk - 1, src[pl.ds(rn, 8), :][0:1, :], 0.0)
            down = jnp.where(row == 0, before, pltpu.roll(cur, 1, axis=0))
            up = jnp.where(row == ROW_BLOCK - 1, after, pltpu.roll(cur, ROW_BLOCK - 1, axis=0))
            w = cw_ref[:, j * SEG:(j + 1) * SEG]
            y = _silu(w[0:1, :] * down + w[1:2, :] * cur + w[2:3, :] * up)
            if j < 2:
                y = y * lax.rsqrt(_gsum(y * y, bd) + EPS)
            if j == 0:
                y = y * (HEAD_DIM ** -0.5)
            dst[pl.ds(r0, ROW_BLOCK), :] = y
        return carry

    lax.fori_loop(0, n_sub * n_blk, prologue, 0)

    for s, d in seq_dirs:
        st_s[s, d] = jnp.zeros((SEG, SEG), F32)
        if has_cache:
            for h in range(N_HEADS):
                st_s[s, d, _hs(h), _hs(h)] = s0_ref[d, h]

    def solve_group(g, carry):
        items = []
        q, k, v, beta, cg_col, g_last, decay, kb, k4 = ([] for _ in range(9))
        for cc in range(group):
            c = g * group + cc
            r0 = pl.multiple_of(c * CHUNK, CHUNK)
            qc, kc, vc = qs[pl.ds(r0, CHUNK), :], ks[pl.ds(r0, CHUNK), :], vs[pl.ds(r0, CHUNK), :]
            pre = gt_ref[pl.ds(r0, CHUNK), :]
            g_all = -jnp.exp(a_log) * _softplus(pre + bias)
            beta_all = _sigmoid(pre)
            kc4 = _block_diag(kc, hm)
            for d in range(2):
                items.append((d, r0, pl.multiple_of(c * 8, 8)))
                cg = _sel_mm(tri[d], g_all)
                col = _col_dense(cg, L_ALPHA + d * N_HEADS, lane_head)
                row_ = _diag_row(col, eye_t)
                bt = _col_dense(beta_all, L_BETA + d * N_HEADS, lane_head)
                q.append(qc), k.append(kc), v.append(vc), k4.append(kc4)
                beta.append(bt), cg_col.append(col), kb.append(kc * bt)
                g_last.append(col[CHUNK - 1:CHUNK, :] if d == 0 else col[0:1, :])
                decay.append(jnp.exp(jnp.where(incl_t[d], col - row_, NEG)))
        n_it = len(items)
        kk = [_mm1(kb[i], k4[i], NT) for i in range(n_it)]
        qk = [_mm1(q[i], k4[i], NT) for i in range(n_it)]
        a = [jnp.where(strict_t[items[i][0]], kk[i] * decay[i], 0.0) for i in range(n_it)]
        xu = [v[i] * beta[i] for i in range(n_it)]
        xw = [kb[i] * jnp.exp(cg_col[i]) for i in range(n_it)]

        units = [(i, h) for i in range(n_it) for h in range(N_HEADS)]
        dirs = [items[i][0] for i, h in units]
        ah = [a[i][:, _hs(h)] for i, h in units]
        x = [jnp.concatenate([xu[i][:, _hs(h)], xw[i][:, _hs(h)]], axis=1) for i, h in units]
        dg = [jnp.where(lvl_masks[d][0], a_, 0.0) for a_, d in zip(ah, dirs)]
        t = [eye64 - d_ for d_ in dg]
        p = [_MM_INV(d_, d_) for d_ in dg]
        pt = [_MM_INV(p_, t_) for p_, t_ in zip(p, t)]
        t = [t_ + u_ for t_, u_ in zip(t, pt)]
        p = [_MM_INV(p_, p_) for p_ in p]
        pt = [_MM_INV(p_, t_) for p_, t_ in zip(p, t)]
        t = [t_ + u_ for t_, u_ in zip(t, pt)]
        for lvl in range(1, len(lvl_masks[0])):
            lo = [jnp.where(lvl_masks[d][lvl], a_, 0.0) for a_, d in zip(ah, dirs)]
            lt = [_MM_INV(l_, t_) for l_, t_ in zip(lo, t)]
            tlt = [_MM_INV(t_, u_) for t_, u_ in zip(t, lt)]
            t = [t_ - u_ for t_, u_ in zip(t, tlt)]
        x = [_MM_APPLY(t_, x_) for t_, x_ in zip(t, x)]
        for i, (d, r0, r8) in enumerate(items):
            xi = x[i * N_HEADS:(i + 1) * N_HEADS]
            u_s[d, pl.ds(r0, CHUNK), :] = jnp.concatenate([x_[:, :HEAD_DIM] for x_ in xi], axis=1)
            w_s[d, pl.ds(r0, CHUNK), :] = jnp.concatenate([x_[:, HEAD_DIM:] for x_ in xi], axis=1).astype(BF16)
            att_s[d, pl.ds(r0, CHUNK), :] = (qk[i] * decay[i]).astype(BF16)
            qg_s[d, pl.ds(r0, CHUNK), :] = (q[i] * jnp.exp(cg_col[i])).astype(BF16)
            kd_s[d, pl.ds(r0, CHUNK), :] = (k[i] * jnp.exp(g_last[i] - cg_col[i])).astype(BF16)
            gl_s[d, pl.ds(r8, 8), :] = jnp.broadcast_to(jnp.exp(g_last[i]), (8, SEG))
        return carry

    lax.fori_loop(0, n_sub * n_chunks // group, solve_group, 0)

    def scan(n, carry):
        ds = [d for s, d in seq_dirs]
        rows = [_chunk_rows(d, n, n_chunks, s * T) for s, d in seq_dirs]
        rows8 = [pl.multiple_of((s * n_chunks + (n if d == 0 else n_chunks - 1 - n)) * 8, 8) for s, d in seq_dirs]
        st = [st_s[u] for u in seq_dirs]
        ws = [_mm1(w_s[d, pl.ds(r, CHUNK), :], x_) for d, r, x_ in zip(ds, rows, st)]
        qs_ = [_mm1(qg_s[d, pl.ds(r, CHUNK), :], x_) for d, r, x_ in zip(ds, rows, st)]
        v_new = [u_s[d, pl.ds(r, CHUNK), :] - x_ for d, r, x_ in zip(ds, rows, ws)]
        v4 = [_block_diag(x_, hm) for x_ in v_new]
        av = [_mm1(att_s[d, pl.ds(r, CHUNK), :], x_) for d, r, x_ in zip(ds, rows, v4)]
        kv = [_mm1(kd_s[d, pl.ds(r, CHUNK), :], x_, TN) for d, r, x_ in zip(ds, rows, v_new)]
        for i, (s, d) in enumerate(seq_dirs):
            st_s[s, d] = st[i] * gl_s[d, pl.ds(rows8[i], 1), :] + jnp.where(hm, kv[i], 0.0)
            (of_s, ob_s)[d][pl.ds(rows[i], CHUNK), :] = qs_[i] + av[i]
        return carry

    lax.fori_loop(0, n_chunks, scan, 0)
    if not has_cache:
        for s, d in seq_dirs:
            for h in range(N_HEADS):
                sf_refs[s][d, h] = st_s[s, d, _hs(h), _hs(h)]
    _epilogue(T * n_sub, of_s, ob_s, z_ref, _silu, bd_ref, ng_ref, o_ref)


def _deltanet(P, G, T, n_seq, gate_par, conv_w, ng, bd, l, state, prev):
    has_cache = state is not None
    st = (2, N_HEADS, HEAD_DIM, HEAD_DIM)
    in_specs = [_pick_spec((8, GATE_LANES), l), _pick_spec((3, 3 * SEG), l),
                _const_spec((1, SEG)), _const_spec((SEG, SEG))]
    args = [gate_par, conv_w, ng, bd]
    if has_cache:
        in_specs.append(_state_in_spec(l, st))
        args.append(state)
    n_sub = 1 if has_cache else SEQS_PER_STEP
    rows = n_sub * T
    scratch = ([pltpu.VMEM((rows, SEG), F32)] * 2 + [pltpu.VMEM((n_sub, 2, SEG, SEG), F32)]
               + [pltpu.VMEM((rows, SEG), F32)] * 3
               + [pltpu.VMEM((2, rows, SEG), F32)] + [pltpu.VMEM((2, rows, SEG), BF16)] * 4
               + [pltpu.VMEM((2, rows // CHUNK * 8, SEG), F32)])
    return _mixer_call(functools.partial(_delta_kernel, T=T, has_cache=has_cache, n_sub=n_sub), "deltanet",
                       T, n_seq, P, (S_AQ, S_AK, S_AV, S_AZ), G,
                       in_specs, args, scratch, () if has_cache else (st,), l, prev)


def _diff_kernel(*refs, T, has_cache, lam_init, n_sub):
    if has_cache:
        (q_ref, k_ref, v_ref, qg_ref, kg_ref, lam_ref, ng_ref, bd32_ref, bd64_ref,
         cos_ref, sin_ref, ck_ref, cv_ref, o_ref, qs, kh, vt) = refs
    else:
        q_ref, k_ref, v_ref, qg_ref, kg_ref, lam_ref, ng_ref, bd32_ref, bd64_ref, o_ref = refs[:10]
        ko_refs, vo_refs = refs[10:10 + n_sub], refs[10 + n_sub:10 + 2 * n_sub]
        qs, kh, vt = refs[10 + 2 * n_sub:]
    n_blk = T // ROW_BLOCK
    s0 = ck_ref.shape[1] if has_cache else 0
    bd32 = bd32_ref[...]
    lane = lax.broadcasted_iota(jnp.int32, (1, SEG), 1)
    first_half = (lane % 16) < 8

    if has_cache:
        for h in range(N_HEADS):
            kh[h, 0:s0, :] = ck_ref[h]
            vt[h, :, 0:s0] = cv_ref[h].T

    for i in range(n_sub * n_blk):
        sub, r0, rl = i // n_blk, i * ROW_BLOCK, (i % n_blk) * ROW_BLOCK
        for src, g_ref in ((q_ref, qg_ref), (k_ref, kg_ref)):
            x = src[pl.ds(r0, ROW_BLOCK), :]
            y = x * lax.rsqrt(_gsum(x * x, bd32) * (1.0 / DQK) + EPS) * g_ref[...]
            if has_cache:
                partner = jnp.where(first_half, pltpu.roll(y, SEG - 8, axis=1), pltpu.roll(y, 8, axis=1))
                y = y * cos_ref[pl.ds(r0, ROW_BLOCK), :] + partner * sin_ref[pl.ds(r0, ROW_BLOCK), :]
            if src is q_ref:
                qs[pl.ds(r0, ROW_BLOCK), :] = y
            else:
                for h in range(N_HEADS):
                    kh[sub * N_HEADS + h, pl.ds(s0 + rl, ROW_BLOCK), :] = y[:, _hs(h)]
                    if not has_cache:
                        ko_refs[sub][h, pl.ds(rl, ROW_BLOCK), :] = y[:, _hs(h)]
        xv = v_ref[pl.ds(r0, ROW_BLOCK), :]
        xvt = xv.T
        for h in range(N_HEADS):
            vt[sub * N_HEADS + h, :, pl.ds(s0 + rl, ROW_BLOCK)] = xvt[_hs(h), :]
            if not has_cache:
                vo_refs[sub][h, pl.ds(rl, ROW_BLOCK), :] = xv[:, _hs(h)]

    lp = lam_ref[...]
    lam = (jnp.exp(jnp.sum(lp[0:1, :] * lp[1:2, :], axis=1, keepdims=True))
           - jnp.exp(jnp.sum(lp[2:3, :] * lp[3:4, :], axis=1, keepdims=True)) + lam_init)
    scale = DQK ** -0.5 * math.log2(math.e)
    comp_rows = lax.broadcasted_iota(jnp.int32, (HEAD_DIM, 1), 0) < DQK
    bd64 = bd64_ref[...]
    ng = ng_ref[...]

    def qblock(i, carry):
        rows = [pl.multiple_of(s * T + i * Q_BLOCK, Q_BLOCK) for s in range(n_sub)]
        qct = []
        for r0 in rows:
            qt = (qs[pl.ds(r0, Q_BLOCK), :] * scale).T
            for h in range(N_HEADS):
                qh = qt[_hs(h), :]
                q1 = jnp.where(comp_rows, qh, 0.0)
                qct += [q1, qh - q1]
        n_units = len(qct)

        def softmax_v(u, st):
            e = jnp.exp2(st - st.max(axis=0, keepdims=True))
            return _mm1(vt[u // 2], e) * (1.0 / e.sum(axis=0, keepdims=True))

        if has_cache:
            st = _mm1(kh[0], qct[0])
            res = []
            for u in range(n_units):
                nxt = _mm1(kh[(u + 1) // 2], qct[u + 1]) if u + 1 < n_units else None
                res.append(softmax_v(u, st))
                st = nxt
        else:
            sts = [_mm1(kh[u // 2], qct[u]) for u in range(n_units)]
            res = [softmax_v(u, st) for u, st in enumerate(sts)]
        for s, r0 in enumerate(rows):
            rs = res[s * 2 * N_HEADS:(s + 1) * 2 * N_HEADS]
            outs = [rs[2 * h] - lam * rs[2 * h + 1] for h in range(N_HEADS)]
            o = jnp.concatenate(outs, axis=0).T
            o_ref[pl.ds(r0, Q_BLOCK), :] = _head_norm_gate(o, bd64, ng, 1.0 - lam_init)
        return carry

    lax.fori_loop(0, T // Q_BLOCK, qblock, 0)


def _diffattn(P, T, n_seq, qg, kg, lam_par, ng, bd32, bd64, l, rope, cache, prev):
    has_cache = cache is not None
    lam_init = 0.8 - 0.6 * math.exp(-0.3 * l)
    kv = (N_HEADS, T, HEAD_DIM)
    in_specs = [_const_spec((1, SEG)), _const_spec((1, SEG)), _pick_spec((4, DQK), l), _const_spec((1, SEG)),
                _const_spec((SEG, SEG)), _const_spec((SEG, SEG))]
    args = [qg, kg, lam_par, ng, bd32, bd64]
    if has_cache:
        ckv = cache[0].shape[2:]
        in_specs += [_const_spec((T, SEG)), _const_spec((T, SEG)), _state_in_spec(l, ckv), _state_in_spec(l, ckv)]
        args += [rope[0], rope[1], cache[0], cache[1]]
    n_keys = T + (cache[0].shape[3] if has_cache else 0)
    n_sub = 1 if has_cache else SEQS_PER_STEP
    scratch = [pltpu.VMEM((n_sub * T, SEG), F32), pltpu.VMEM((n_sub * N_HEADS, n_keys, HEAD_DIM), F32),
               pltpu.VMEM((n_sub * N_HEADS, HEAD_DIM, n_keys), F32)]
    return _mixer_call(functools.partial(_diff_kernel, T=T, has_cache=has_cache, lam_init=lam_init, n_sub=n_sub),
                       "diff_attn", T, n_seq, P, (S_BQ, S_BK, S_BV), None,
                       in_specs, args, scratch, () if has_cache else (kv, kv), l, prev)


def _rope_tables(T):
    n_freq = DQK // 4
    t = jnp.arange(T)
    rows = (t // GRID_W).astype(F32)
    cols = (t % GRID_W).astype(F32)
    freqs = ROPE_BASE ** (-jnp.arange(n_freq, dtype=F32) / n_freq)
    ang_r, ang_c = rows[:, None] * freqs, cols[:, None] * freqs

    def comp(fn, sign):
        return jnp.concatenate([fn(ang_r), sign * fn(ang_r), fn(ang_c), sign * fn(ang_c)], axis=1)

    reps = SEG // DQK
    cos = jnp.tile(comp(jnp.cos, 1.0), (1, reps))
    sin = jnp.tile(jnp.concatenate([-jnp.sin(ang_r), jnp.sin(ang_r), -jnp.sin(ang_c), jnp.sin(ang_c)], axis=1),
                   (1, reps))
    return cos, sin


def _block_ones(group):
    i = np.arange(SEG)
    return jnp.asarray(i[:, None] // group == i[None, :] // group, BF16)


def kernel(x_prompt, x_sample, cache_diff_k, cache_diff_v, state_delta, state_mlstm_C, state_mlstm_n, state_mlstm_m, state_ret, c, c_ctx, w_ada, b_ada, norm_g, ffn_w_gate, ffn_w_up, ffn_w_down, w_in, dn_conv_w, dn_a_log, dn_dt_bias, dn_norm_g, da_qn_g, da_kn_g, da_lambda, da_norm_g, ml_i_bias, ml_f_bias, ml_norm_g, ret_decay_logit, ret_norm_g, w_branch, w_out):
    B, T, _ = x_prompt.shape
    Bs, Ts, _ = x_sample.shape
    cond8 = jnp.concatenate([c_ctx[None], c, jnp.zeros((8 - 1 - Bs, D_MODEL), F32)], axis=0)
    mod = _ada(cond8, w_ada, b_ada).reshape(DEPTH, 8, N_MOD, D_MODEL)
    bd32, bd64 = _block_ones(DQK), _block_ones(HEAD_DIM)
    rope = _rope_tables(Ts)
    tile_heads = lambda g: jnp.tile(g, SEG // g.shape[0])[None]

    wg, wu, wd = ffn_w_gate.astype(BF16), ffn_w_up.astype(BF16), ffn_w_down.astype(BF16)
    cols = lambda f: w_in[:, :, _IN_OFFS[f]:_IN_OFFS[f + 1]]
    w_mix = jnp.concatenate([cols(f) for f in _SEG_FIELDS], axis=2).astype(BF16)
    w_gate = jnp.concatenate([cols(f) for f in _GATE_FIELDS]
                             + [jnp.zeros((DEPTH, D_MODEL, GATE_LANES - 32), F32)], axis=2).astype(BF16)
    w_merge = cols(19).astype(BF16)
    wb, wo = w_branch.astype(BF16), w_out.astype(BF16)
    norm_g4 = norm_g.reshape(DEPTH, 3, 1, D_MODEL)
    lanes = lambda *parts: jnp.concatenate([p.reshape(DEPTH, 1, -1) for p in parts], axis=2)
    z8 = jnp.zeros((DEPTH, 8), F32)
    gate_par = jnp.concatenate([
        lanes(z8, dn_dt_bias, ml_i_bias, ml_f_bias, jnp.zeros((DEPTH, GATE_LANES - 32), F32)),
        lanes(z8, dn_a_log, jnp.zeros((DEPTH, GATE_LANES - 16), F32)),
        jnp.zeros((DEPTH, 6, GATE_LANES), F32)], axis=1)
    ml_state = (state_mlstm_C, state_mlstm_n.reshape(Bs, DEPTH, 2, 1, SEG),
                jnp.repeat(state_mlstm_m, HEAD_DIM, axis=-1).reshape(Bs, DEPTH, 2, 1, SEG))

    xs = {"ctx": x_prompt.reshape(B * T, D_MODEL), "smp": x_sample.reshape(Bs * Ts, D_MODEL)}
    geo = {"ctx": (T, B, B * T, 0), "smp": (Ts, Bs, Ts, 1)}
    states = {}
    for l in range(DEPTH):
        for path in ("ctx", "smp"):
            Tp, n_seq, rows_per_cond, first_cond = geo[path]
            mspec = _mod_spec(l, rows_per_cond, first_cond)
            smp = path == "smp"
            prev = states.get if not smp else (lambda name: None)
            x = _ffn(xs[path], mod, mspec, norm_g4, wg, wu, wd, l, 0)
            P, G = _inproj(x, mod, mspec, norm_g4, w_mix, w_gate, l)
            ra = _deltanet(P, G, Tp, n_seq, gate_par, dn_conv_w, tile_heads(dn_norm_g[l]), bd64, l,
                           state_delta if smp else None, prev("dn"))
            rb = _diffattn(P, Tp, n_seq, tile_heads(da_qn_g[l]), tile_heads(da_kn_g[l]), da_lambda,
                           tile_heads(da_norm_g[l]), bd32, bd64, l, rope if smp else None,
                           (cache_diff_k, cache_diff_v) if smp else None, prev("kv"))
            rc = _mlstm(P, G, Tp, n_seq, gate_par, tile_heads(ml_norm_g[l]), bd64, l,
                        ml_state if smp else None, prev("ml"))
            rd = _retention(P, Tp, n_seq, ret_decay_logit[l], tile_heads(ret_norm_g[l]), bd64, l,
                            state_ret if smp else None, prev("r"))
            if not smp:
                states = {"dn": ra[1:], "kv": rb[1:], "ml": rc[1:], "r": rd[1:]}
            x = _merge(x, mod, mspec, norm_g4, (ra[0], rb[0], rc[0], rd[0]), w_merge, wb, wo, l)
            xs[path] = _ffn(x, mod, mspec, norm_g4, wg, wu, wd, l, 1)
    (new_dn,), (new_k, new_v), (new_c, new_n, new_m), (new_r,) = (states[k] for k in ("dn", "kv", "ml", "r"))
    per_head = lambda rows: rows.reshape(B, DEPTH, 2, N_HEADS, HEAD_DIM)
    return (xs["ctx"].reshape(B, T, D_MODEL), xs["smp"].reshape(Bs, Ts, D_MODEL),
            new_k, new_v, new_dn, new_c, per_head(new_n), per_head(new_m)[..., 0], new_r)
```
